```python
import jax
import jax.numpy as jnp
from jax import lax
import numpy as np

D_MODEL = 1024
BATCH = 2
SEQ = 8192
DEPTH = 1

DSA_PATTERNS = ((128, 1), (512, 4), (2048, 16))
DSA_GROUPS = 3
DSA_HEADS = 8
DSA_HEAD_DIM = 64
DSA_WIDTH = DSA_HEADS * DSA_HEAD_DIM
DSA_BLOCK = 128
ROPE_THETA = 10000.0

GDN_HEADS = 8
GDN_KEY_DIM = 64
GDN_VAL_DIM = 64
GDN_K_WIDTH = GDN_HEADS * GDN_KEY_DIM
GDN_V_WIDTH = GDN_HEADS * GDN_VAL_DIM
GDN_CONV = 4
GDN_CHUNK = 64

NORM_EPS = 1e-6

IN_SIZES = (
    DSA_GROUPS * 3 * DSA_WIDTH,
    DSA_WIDTH,
    2 * GDN_K_WIDTH + GDN_V_WIDTH,
    GDN_V_WIDTH,
    GDN_HEADS,
    GDN_HEADS,
    D_MODEL,
    D_MODEL,
)
IN_WIDTH = sum(IN_SIZES)

kernel_name = 'hybrid_dilated_attn_gated_deltanet_block'


def _rmsnorm(x, w):
    xf = x.astype(jnp.float32)
    y = xf * lax.rsqrt(jnp.mean(xf * xf, axis=-1, keepdims=True) + NORM_EPS)
    return (y * w.astype(jnp.float32)).astype(x.dtype)


def _l2norm(t):
    return t * lax.rsqrt(jnp.sum(t * t, axis=-1, keepdims=True) + NORM_EPS)


def _split_columns(h):
    parts, start = [], 0
    for size in IN_SIZES:
        parts.append(h[..., start:start + size])
        start += size
    return parts


def _rope_tables(seq, dim):
    inv_freq = ROPE_THETA ** (-jnp.arange(0, dim, 2, dtype=jnp.float32) / dim)
    ang = jnp.arange(seq, dtype=jnp.float32)[:, None] * inv_freq[None, :]
    ang = jnp.concatenate([ang, ang], axis=-1)
    return jnp.cos(ang), jnp.sin(ang)


def _apply_rope(t, cos, sin):
    half = t.shape[-1] // 2
    rot = jnp.concatenate([-t[..., half:], t[..., :half]], axis=-1)
    return t * cos[:, None, None, :] + rot * sin[:, None, None, :]


def _dilated_window_attention(q, k, v, window, dilation):
    b, s, h, dh = q.shape
    n_back = window // dilation
    sub_len = s // dilation
    n_blk = -(-sub_len // DSA_BLOCK)
    pad = n_blk * DSA_BLOCK - sub_len

    def to_blocks(t):
        t = t.reshape(b, sub_len, dilation, h, dh).transpose(0, 2, 1, 3, 4)
        t = jnp.pad(t, ((0, 0), (0, 0), (0, pad), (0, 0), (0, 0)))
        return t.reshape(b, dilation, n_blk, DSA_BLOCK, h, dh)

    def with_previous_block(t):
        prev = jnp.pad(t, ((0, 0), (0, 0), (1, 0), (0, 0), (0, 0), (0, 0)))[:, :, :-1]
        return jnp.concatenate([prev, t], axis=3)

    qb = to_blocks(q)
    kb = with_previous_block(to_blocks(k))
    vb = with_previous_block(to_blocks(v))
    scores = jnp.einsum('brnqhd,brnkhd->brnhqk', qb, kb) * (dh ** -0.5)
    qi = jnp.arange(DSA_BLOCK)[:, None]
    kj = jnp.arange(2 * DSA_BLOCK)[None, :]
    dist = qi + DSA_BLOCK - kj
    key_idx = jnp.arange(n_blk)[:, None, None] * DSA_BLOCK + kj - DSA_BLOCK
    valid = (dist >= 0) & (dist <= n_back) & (key_idx >= 0)
    scores = jnp.where(valid[:, None], scores, -jnp.inf)
    m = jnp.max(scores, axis=-1, keepdims=True)
    p = jnp.exp(scores - m)
    den = jnp.sum(p, axis=-1, keepdims=True)
    o = jnp.einsum('brnhqk,brnkhd->brnqhd', p / den, vb)
    lse = (m + jnp.log(den))[..., 0].transpose(0, 1, 2, 4, 3)

    def from_blocks(t):
        t = t.reshape(b, dilation, n_blk * DSA_BLOCK, *t.shape[4:])[:, :, :sub_len]
        t = jnp.moveaxis(t, 1, 2)
        return t.reshape(b, s, *t.shape[3:])

    return from_blocks(o), from_blocks(lse)


def _dilated_mixture(q, k, v):
    outs, lses = [], []
    for g, (window, dilation) in enumerate(DSA_PATTERNS):
        o, lse = _dilated_window_attention(q[:, :, g], k[:, :, g], v[:, :, g], window, dilation)
        outs.append(o)
        lses.append(lse)
    wts = jax.nn.softmax(jnp.stack(lses), axis=0)
    o = jnp.einsum('gbsh,gbshd->bshd', wts, jnp.stack(outs))
    return o.reshape(o.shape[0], o.shape[1], -1)


def _causal_depthwise_conv(x, w):
    k, c = w.shape
    return lax.conv_general_dilated(
        x, w[:, None, :].astype(x.dtype), window_strides=(1,), padding=((k - 1, 0),),
        dimension_numbers=('NWC', 'WIO', 'NWC'), feature_group_count=c)


def _gated_delta_rule(q, k, v, g, beta):
    b, s, h, dk = q.shape
    dv = v.shape[-1]
    c = GDN_CHUNK
    n = s // c

    def chunk(t):
        return t.reshape(b, n, c, h, -1).transpose(0, 3, 1, 2, 4)

    qc, kc, vc = chunk(q), chunk(k), chunk(v)
    gc = g.reshape(b, n, c, h).transpose(0, 3, 1, 2)
    bc = beta.reshape(b, n, c, h).transpose(0, 3, 1, 2)
    G = jnp.cumsum(gc, axis=-1)
    causal = jnp.tril(jnp.ones((c, c), dtype=bool))
    strict = jnp.tril(jnp.ones((c, c), dtype=bool), k=-1)
    decay_incl = jnp.exp(jnp.where(causal, G[..., :, None] - G[..., None, :], -jnp.inf))
    decay_strict = jnp.where(strict, decay_incl, 0.0)
    k_beta = kc * bc[..., None]
    a = jnp.einsum('bhnid,bhnjd->bhnij', k_beta, kc) * decay_strict
    eye = jnp.eye(c, dtype=a.dtype)
    t_inv = lax.linalg.triangular_solve(eye + a, jnp.broadcast_to(eye, a.shape),
                                        left_side=True, lower=True, unit_diagonal=True)
    u = t_inv @ (vc * bc[..., None])
    w = t_inv @ (k_beta * jnp.exp(G)[..., None])
    attn = jnp.einsum('bhnid,bhnjd->bhnij', qc, kc) * decay_incl
    q_dec = qc * jnp.exp(G)[..., None]
    g_last = G[..., -1:]
    k_dec = kc * jnp.exp(g_last - G)[..., None]
    chunk_decay = jnp.exp(g_last[..., 0])
    xs = tuple(jnp.moveaxis(t, 2, 0) for t in (q_dec, attn, u, w, k_dec, chunk_decay))

    def step(state, inp):
        q_e, at, u_c, w_c, k_d, dec = inp
        v_new = u_c - jnp.einsum('bhck,bhkv->bhcv', w_c, state)
        o = jnp.einsum('bhck,bhkv->bhcv', q_e, state) + jnp.einsum('bhij,bhjv->bhiv', at, v_new)
        state = state * dec[..., None, None] + jnp.einsum('bhck,bhcv->bhkv', k_d, v_new)
        return state, o

    state0 = jnp.zeros((b, h, dk, dv), dtype=q.dtype)
    _, o = lax.scan(step, state0, xs)
    return o.transpose(1, 0, 3, 2, 4).reshape(b, s, h, dv)


def setup_inputs(seed: int = 0) -> dict:
    key = jax.random.key(seed)
    ks = jax.random.split(key, 12)
    f32 = jnp.float32
    x = jax.random.normal(ks[0], (BATCH, SEQ, D_MODEL), f32)
    norm_w = 1.0 + 0.05 * jax.random.normal(ks[1], (DEPTH, D_MODEL), f32)
    w_in = jax.random.normal(ks[2], (DEPTH, D_MODEL, IN_WIDTH), f32) * D_MODEL ** -0.5
    conv_w = jax.random.normal(ks[3], (DEPTH, GDN_CONV, 2 * GDN_K_WIDTH + GDN_V_WIDTH), f32) * GDN_CONV ** -0.5
    a_log = jnp.log(jax.random.uniform(ks[4], (DEPTH, GDN_HEADS), f32, 1.0, 16.0))
    dt_bias = 0.5 * jax.random.normal(ks[5], (DEPTH, GDN_HEADS), f32)
    gdn_norm_w = 1.0 + 0.05 * jax.random.normal(ks[6], (DEPTH, GDN_VAL_DIM), f32)
    w_up_a = jax.random.normal(ks[7], (DEPTH, DSA_WIDTH, D_MODEL), f32) * DSA_WIDTH ** -0.5
    w_up_b = jax.random.normal(ks[8], (DEPTH, GDN_V_WIDTH, D_MODEL), f32) * GDN_V_WIDTH ** -0.5
    w_out = jax.random.normal(ks[9], (DEPTH, D_MODEL, D_MODEL), f32) * D_MODEL ** -0.5
    final_norm_w = 1.0 + 0.05 * jax.random.normal(ks[10], (D_MODEL,), f32)
    return {'x': x, 'norm_w': norm_w, 'w_in': w_in, 'conv_w': conv_w, 'a_log': a_log,
            'dt_bias': dt_bias, 'gdn_norm_w': gdn_norm_w, 'w_up_a': w_up_a, 'w_up_b': w_up_b,
            'w_out': w_out, 'final_norm_w': final_norm_w}


def reference(x, norm_w, w_in, conv_w, a_log, dt_bias, gdn_norm_w, w_up_a, w_up_b, w_out, final_norm_w):
    f32 = jnp.float32
    b, s, _ = x.shape
    cos, sin = _rope_tables(s, DSA_HEAD_DIM)
    for layer in range(DEPTH):
        h = _rmsnorm(x, norm_w[layer])
        proj = h @ w_in[layer]
        dsa_qkv, dsa_z, gdn_qkv, gdn_z, gdn_b, gdn_a, gate_a, gate_b = _split_columns(proj)

        qkv = dsa_qkv.astype(f32).reshape(b, s, DSA_GROUPS, 3, DSA_HEADS, DSA_HEAD_DIM)
        q_a = _apply_rope(qkv[:, :, :, 0], cos, sin)
        k_a = _apply_rope(qkv[:, :, :, 1], cos, sin)
        v_a = qkv[:, :, :, 2]
        o_a = _dilated_mixture(q_a, k_a, v_a).astype(x.dtype)
        y_a = (o_a * jax.nn.silu(dsa_z)) @ w_up_a[layer]

        cqkv = jax.nn.silu(_causal_depthwise_conv(gdn_qkv, conv_w[layer])).astype(f32)
        gq, gk, gv = jnp.split(cqkv, [GDN_K_WIDTH, 2 * GDN_K_WIDTH], axis=-1)
        gq = _l2norm(gq.reshape(b, s, GDN_HEADS, GDN_KEY_DIM)) * GDN_KEY_DIM ** -0.5
        gk = _l2norm(gk.reshape(b, s, GDN_HEADS, GDN_KEY_DIM))
        gv = gv.reshape(b, s, GDN_HEADS, GDN_VAL_DIM)
        beta = jax.nn.sigmoid(gdn_b.astype(f32))
        g = -jnp.exp(a_log[layer].astype(f32)) * jax.nn.softplus(gdn_a.astype(f32) + dt_bias[layer].astype(f32))
        o_b = _gated_delta_rule(gq, gk, gv, g, beta)
        o_b = _rmsnorm(o_b, gdn_norm_w[layer]).reshape(b, s, GDN_V_WIDTH).astype(x.dtype)
        y_b = (o_b * jax.nn.silu(gdn_z)) @ w_up_b[layer]

        merged = jax.nn.sigmoid(gate_a) * y_a + jax.nn.sigmoid(gate_b) * y_b
        x = x + merged @ w_out[layer]
    return _rmsnorm(x, final_norm_w)
```

```python
import functools

import numpy as np
import jax
import jax.numpy as jnp
from jax import lax
from jax.experimental import pallas as pl
from jax.experimental.pallas import tpu as pltpu

F32 = jnp.float32
BF16 = jnp.bfloat16

D_MODEL = 1024
DSA_PATTERNS = ((128, 1), (512, 4), (2048, 16))
DSA_HEADS = 8
DSA_HEAD_DIM = 64
DSA_WIDTH = DSA_HEADS * DSA_HEAD_DIM
DSA_BLOCK = 128
ROPE_THETA = 10000.0
GDN_HEADS = 8
GDN_DIM = 64
GDN_WIDTH = GDN_HEADS * GDN_DIM
GDN_CONV = 4
GDN_CHUNK = 64
NORM_EPS = 1e-6

QKV_A_COLS = 3 * 3 * DSA_WIDTH
MAIN_COLS = 9216
LOGIT_START = 7168
LOGIT_PAD = 128
COL_TILE = 1536
N_COL_TILES = MAIN_COLS // COL_TILE
N_COL_BLOCKS_512 = MAIN_COLS // 512

VMEM_LIMIT_BYTES = 56 * 1024 * 1024
LANES = 128
NEG_BIG = -1e30


def _lane_iota(shape):
    return lax.broadcasted_iota(jnp.int32, shape, len(shape) - 1)


def _in_proj_kernel(x_ref, nw_ref, w_ref, wl_ref, cos_ref, sin_ref, main_ref, logit_ref, h_ref):
    j = pl.program_id(1)

    @pl.when(j == 0)
    def _():
        x = x_ref[...]
        ms = jnp.mean(x * x, axis=-1, keepdims=True)
        h = (x * lax.rsqrt(ms + NORM_EPS) * nw_ref[...]).astype(BF16)
        h_ref[...] = h
        logit_ref[...] = jnp.dot(h, wl_ref[...], preferred_element_type=F32)

    acc = jnp.dot(h_ref[...], w_ref[...], preferred_element_type=F32)

    @pl.when(j < 3)
    def _():
        cos = cos_ref[...]
        sin = sin_ref[...]
        low_half = (_lane_iota(cos.shape) % DSA_HEAD_DIM) < (DSA_HEAD_DIM // 2)
        for c in range(8):
            t = acc[:, c * LANES:(c + 1) * LANES]
            rot = jnp.where(low_half, pltpu.roll(t, LANES - 32, 1), pltpu.roll(t, 32, 1))
            r = t * cos + rot * sin
            if c < 4:
                r = r * (DSA_HEAD_DIM ** -0.5)
            main_ref[:, c * LANES:(c + 1) * LANES] = r.astype(BF16)
        main_ref[:, 8 * LANES:] = acc[:, 8 * LANES:].astype(BF16)

    @pl.when(j >= 3)
    def _():
        main_ref[...] = acc.astype(BF16)


def _in_proj(x2, norm_w, w_main, w_logit, cos_t, sin_t, seq):
    rows = x2.shape[0]
    tm = min(1024, seq)
    n_seq_tiles = seq // tm
    return pl.pallas_call(
        _in_proj_kernel,
        grid=(rows // tm, N_COL_TILES),
        in_specs=[
            pl.BlockSpec((tm, D_MODEL), lambda i, j: (i, 0)),
            pl.BlockSpec((1, D_MODEL), lambda i, j: (0, 0)),
            pl.BlockSpec((D_MODEL, COL_TILE), lambda i, j: (0, j)),
            pl.BlockSpec((D_MODEL, LOGIT_PAD), lambda i, j: (0, 0)),
            pl.BlockSpec((tm, LANES), lambda i, j: (i % n_seq_tiles, 0)),
            pl.BlockSpec((tm, LANES), lambda i, j: (i % n_seq_tiles, 0)),
        ],
        out_specs=[
            pl.BlockSpec((tm, COL_TILE), lambda i, j: (i, j)),
            pl.BlockSpec((tm, LOGIT_PAD), lambda i, j: (i, 0)),
        ],
        out_shape=[
            jax.ShapeDtypeStruct((rows, MAIN_COLS), BF16),
            jax.ShapeDtypeStruct((rows, LOGIT_PAD), F32),
        ],
        scratch_shapes=[pltpu.VMEM((tm, D_MODEL), BF16)],
        compiler_params=pltpu.CompilerParams(
            dimension_semantics=("parallel", "arbitrary"), vmem_limit_bytes=VMEM_LIMIT_BYTES),
        name="in_proj",
    )(x2, norm_w, w_main, w_logit, cos_t, sin_t)


def _dsa_kernel(q_ref, kc_ref, kp_ref, vc_ref, vp_ref, o_ref, lse_ref, *, tq):
    first = pl.program_id(2) == 0
    n_sub = tq // DSA_BLOCK
    qi = lax.broadcasted_iota(jnp.int32, (DSA_BLOCK, 2 * DSA_BLOCK), 0)
    kj = lax.broadcasted_iota(jnp.int32, (DSA_BLOCK, 2 * DSA_BLOCK), 1)
    band = (kj >= qi) & (kj <= qi + DSA_BLOCK)
    band_first = band & (kj >= jnp.where(first, DSA_BLOCK, 0))
    head_a = _lane_iota((DSA_BLOCK, LANES)) < DSA_HEAD_DIM

    for i in range(n_sub):
        rows = slice(i * DSA_BLOCK, (i + 1) * DSA_BLOCK)
        mask = band_first if i == 0 else band
        for hp in range(DSA_HEADS // 2):
            cols = slice(hp * LANES, (hp + 1) * LANES)
            q2 = q_ref[0, rows, cols]
            if i == 0:
                k_lo, v_lo = kp_ref[0, :, cols], vp_ref[0, :, cols]
            else:
                prev = slice((i - 1) * DSA_BLOCK, i * DSA_BLOCK)
                k_lo, v_lo = kc_ref[0, prev, cols], vc_ref[0, prev, cols]
            k2 = jnp.concatenate([k_lo, kc_ref[0, rows, cols]], axis=0)
            v2 = jnp.concatenate([v_lo, vc_ref[0, rows, cols]], axis=0)
            outs, lses = [], []
            for is_a in (True, False):
                sel = head_a if is_a else jnp.logical_not(head_a)
                qh = jnp.where(sel, q2, jnp.zeros_like(q2))
                s = lax.dot_general(qh, k2, (((1,), (1,)), ((), ())), preferred_element_type=F32)
                s = jnp.where(mask, s, NEG_BIG)
                m = jnp.max(s, axis=-1, keepdims=True)
                p = jnp.exp(s - m)
                den = jnp.sum(p, axis=-1, keepdims=True)
                pv = jnp.dot(p.astype(BF16), v2, preferred_element_type=F32)
                outs.append(pv / den)
                lses.append(m + jnp.log(den))
            o_ref[0, rows, cols] = jnp.where(head_a, outs[0], outs[1]).astype(o_ref.dtype)
            lse_ref[0, rows, cols] = jnp.where(head_a, lses[0], lses[1])


def _dsa_attention(main3, group, dilation):
    b, seq, _ = main3.shape
    sub_len = seq // dilation
    tq = min(512, sub_len)
    nq = sub_len // tq
    blocks_per_tile = tq // DSA_BLOCK
    view = main3.reshape(b, sub_len, dilation * MAIN_COLS)
    base = group * 3

    def col(r, t):
        return r * N_COL_BLOCKS_512 + base + t

    cur = lambda t: pl.BlockSpec((1, tq, 512), lambda bi, r, n: (bi, n, col(r, t)))
    prev = lambda t: pl.BlockSpec(
        (1, DSA_BLOCK, 512), lambda bi, r, n: (bi, jnp.maximum(n * blocks_per_tile - 1, 0), col(r, t)))
    out_spec = pl.BlockSpec((1, tq, 512), lambda bi, r, n: (bi, n, r))
    o, lse = pl.pallas_call(
        functools.partial(_dsa_kernel, tq=tq),
        grid=(b, dilation, nq),
        in_specs=[cur(0), cur(1), prev(1), cur(2), prev(2)],
        out_specs=[out_spec, out_spec],
        out_shape=[
            jax.ShapeDtypeStruct((b, sub_len, dilation * DSA_WIDTH), BF16),
            jax.ShapeDtypeStruct((b, sub_len, dilation * DSA_WIDTH), F32),
        ],
        compiler_params=pltpu.CompilerParams(
            dimension_semantics=("parallel", "parallel", "arbitrary"), vmem_limit_bytes=VMEM_LIMIT_BYTES),
        name=f"dsa_attn_d{dilation}",
    )(view, view, view, view, view)
    return o.reshape(b * seq, DSA_WIDTH), lse.reshape(b * seq, DSA_WIDTH)


def _silu(x):
    return x * (1.0 / (1.0 + jnp.exp(-x)))


def _sigmoid(x):
    return 1.0 / (1.0 + jnp.exp(-x))


def _out_proj_kernel(o0_ref, o1_ref, o2_ref, l0_ref, l1_ref, l2_ref, za_ref, ob_ref, zb_ref, ga_ref, gb_ref,
                     x_ref, wa_ref, wb_ref, wo_ref, fw_ref, out_ref):
    l0, l1, l2 = l0_ref[...], l1_ref[...], l2_ref[...]
    mx = jnp.maximum(jnp.maximum(l0, l1), l2)
    e0, e1, e2 = jnp.exp(l0 - mx), jnp.exp(l1 - mx), jnp.exp(l2 - mx)
    o_a = (e0 * o0_ref[...].astype(F32) + e1 * o1_ref[...].astype(F32) + e2 * o2_ref[...].astype(F32)) / (e0 + e1 + e2)
    a_in = (o_a * _silu(za_ref[...].astype(F32))).astype(BF16)
    b_in = (ob_ref[...].astype(F32) * _silu(zb_ref[...].astype(F32))).astype(BF16)
    y_a = jnp.dot(a_in, wa_ref[...], preferred_element_type=F32)
    y_b = jnp.dot(b_in, wb_ref[...], preferred_element_type=F32)
    merged = _sigmoid(ga_ref[...].astype(F32)) * y_a + _sigmoid(gb_ref[...].astype(F32)) * y_b
    y = x_ref[...] + jnp.dot(merged.astype(BF16), wo_ref[...], preferred_element_type=F32)
    ms = jnp.mean(y * y, axis=-1, keepdims=True)
    out_ref[...] = y * lax.rsqrt(ms + NORM_EPS) * fw_ref[...]


def _out_proj(o_groups, lse_groups, main2, o_b, x2, w_up_a, w_up_b, w_out, final_w):
    rows = x2.shape[0]
    tm = min(512, rows)
    row512 = lambda c: pl.BlockSpec((tm, 512), lambda i: (i, c))
    row1024 = lambda c: pl.BlockSpec((tm, 1024), lambda i: (i, c))
    whole = lambda a: pl.BlockSpec(a.shape, lambda i: (0,) * a.ndim)
    return pl.pallas_call(
        _out_proj_kernel,
        grid=(rows // tm,),
        in_specs=[row512(0)] * 6 + [row512(9), row512(0), row512(13), row1024(7), row1024(8), row1024(0),
                                    whole(w_up_a), whole(w_up_b), whole(w_out), whole(final_w)],
        out_specs=pl.BlockSpec((tm, D_MODEL), lambda i: (i, 0)),
        out_shape=jax.ShapeDtypeStruct((rows, D_MODEL), F32),
        compiler_params=pltpu.CompilerParams(
            dimension_semantics=("parallel",), vmem_limit_bytes=VMEM_LIMIT_BYTES),
        name="out_proj",
    )(*o_groups, *lse_groups, main2, o_b, main2, main2, main2, x2, w_up_a, w_up_b, w_out, final_w)


QUAD = 4 * GDN_DIM
GDN_GROUP = 4


def _split_hi_lo(x):
    hi = x.astype(BF16)
    lo = (x - hi.astype(F32)).astype(BF16)
    return hi, lo


def _softplus(x):
    return jnp.maximum(x, 0.0) + jnp.log1p(jnp.exp(-jnp.abs(x)))


def _bmm(a, b):
    return lax.dot_general(a, b, (((2,), (1,)), ((0,), (0,))), preferred_element_type=F32)


def _bmm_nt(a, b):
    return lax.dot_general(a, b, (((2,), (2,)), ((0,), (0,))), preferred_element_type=F32)


def _block_diag(x, bd_mask):
    t = jnp.concatenate([x, x, x, x], axis=1)
    return jnp.where(bd_mask, t, jnp.zeros_like(t))


def _gdn_kernel(q_ref, k_ref, v_ref, qh_ref, kh_ref, vh_ref, lg_ref, cw_ref, alog_ref, dtb_ref, nw_ref,
                e512_ref, eb_ref, eg_ref, tri_ref, o_ref,
                xs_ref, qn_ref, kn_ref, vn_ref, beta_ref, gh_ref, gl_ref,
                wq_ref, u_ref, attn_ref, kdt_ref, dec_ref, state_ref, *, tg):
    first = pl.program_id(1) == 0
    nc = tg // GDN_CHUNK
    nb = 2 * nc

    @pl.when(first)
    def _():
        state_ref[...] = jnp.zeros_like(state_ref)

    def conv_silu(x_ref, halo_ref, col0):
        xs_ref[8:8 + tg, :] = x_ref[...].astype(F32)
        halo = halo_ref[...].astype(F32)[8:16, :]
        xs_ref[0:8, :] = jnp.where(first, jnp.zeros_like(halo), halo)
        w = cw_ref[:, col0:col0 + GDN_WIDTH]
        y = w[0:1, :] * xs_ref[5:5 + tg, :]
        for j in range(1, GDN_CONV):
            y = y + w[j:j + 1, :] * xs_ref[5 + j:5 + j + tg, :]
        return _silu(y)

    def l2norm(y, scale):
        ss = jnp.dot((y * y).astype(BF16), e512_ref[...], preferred_element_type=F32)
        return y * (lax.rsqrt(ss + NORM_EPS) * scale)

    qn_ref[...] = l2norm(conv_silu(q_ref, qh_ref, 0), GDN_DIM ** -0.5)
    kn_ref[...] = l2norm(conv_silu(k_ref, kh_ref, GDN_WIDTH), 1.0)
    vn_ref[...] = conv_silu(v_ref, vh_ref, 2 * GDN_WIDTH)

    logits = lg_ref[...]
    beta_hi, beta_lo = _split_hi_lo(_sigmoid(logits))
    g_hi, g_lo = _split_hi_lo(-jnp.exp(alog_ref[...]) * _softplus(logits + dtb_ref[...]))
    beta_ref[...] = (jnp.dot(beta_hi, eb_ref[...], preferred_element_type=F32)
                     + jnp.dot(beta_lo, eb_ref[...], preferred_element_type=F32))
    gh_ref[...] = jnp.dot(g_hi, eg_ref[...], preferred_element_type=F32).astype(BF16)
    gl_ref[...] = jnp.dot(g_lo, eg_ref[...], preferred_element_type=F32).astype(BF16)

    row = lax.broadcasted_iota(jnp.int32, (GDN_CHUNK, QUAD), 0)
    col = lax.broadcasted_iota(jnp.int32, (GDN_CHUNK, QUAD), 1) % GDN_CHUNK
    incl = row >= col
    strict = row > col
    eye = (row == col).astype(F32)
    bd_mask = (lax.broadcasted_iota(jnp.int32, (QUAD, QUAD), 0) // GDN_DIM
               == lax.broadcasted_iota(jnp.int32, (QUAD, QUAD), 1) // GDN_DIM)
    tri = tri_ref[...]

    def chunk_quads(ref, c0, n):
        a = ref[pl.ds(pl.multiple_of(c0 * GDN_CHUNK, GDN_CHUNK), n * GDN_CHUNK), :]
        a = a.reshape(n, GDN_CHUNK, GDN_WIDTH)
        return jnp.stack([a[:, :, :QUAD], a[:, :, QUAD:]], axis=1).reshape(2 * n, GDN_CHUNK, QUAD)

    n_per = GDN_GROUP // 2

    def local_pass(it, carry):
        c0 = it * n_per
        q, k, v = (chunk_quads(r, c0, n_per) for r in (qn_ref, kn_ref, vn_ref))
        beta = chunk_quads(beta_ref, c0, n_per)
        gh, gl = chunk_quads(gh_ref, c0, n_per), chunk_quads(gl_ref, c0, n_per)

        zero = jnp.zeros_like(gh)
        rhs_hi = jnp.concatenate([gh, jnp.where(strict, gh, zero)], axis=-1)
        rhs_lo = jnp.concatenate([gl, jnp.where(strict, gl, zero)], axis=-1)
        tri_b = jnp.broadcast_to(tri, (GDN_GROUP, GDN_CHUNK, GDN_CHUNK))
        gd = _bmm(tri_b, rhs_hi) + _bmm(tri_b, rhs_lo)
        g_cum, d_pair = gd[:, :, :QUAD], gd[:, :, QUAD:]
        decay_incl = jnp.where(incl, jnp.exp(d_pair), 0.0)
        decay_strict = jnp.where(strict, decay_incl, 0.0)
        exp_g = jnp.exp(g_cum)
        g_last = g_cum[:, GDN_CHUNK - 1:GDN_CHUNK, :]
        k_dec = k * jnp.exp(g_last - g_cum)
        k_beta = k * beta

        k_bd = _block_diag(k.astype(BF16), bd_mask)
        aa = _bmm_nt(jnp.concatenate([k_beta, q], axis=1).astype(BF16), k_bd)
        a = aa[:, :GDN_CHUNK] * decay_strict
        attn = aa[:, GDN_CHUNK:] * decay_incl

        p = a
        t = eye - a
        p = _bmm(p.astype(BF16), _block_diag(p.astype(BF16), bd_mask))
        for _ in range(4):
            r = _bmm(jnp.concatenate([p, t], axis=1).astype(BF16), _block_diag(p.astype(BF16), bd_mask))
            p = r[:, :GDN_CHUNK]
            t = t + r[:, GDN_CHUNK:]
        t = t + _bmm(t.astype(BF16), _block_diag(p.astype(BF16), bd_mask))
        t16 = t.astype(BF16)
        u = _bmm(t16, _block_diag((v * beta).astype(BF16), bd_mask))
        w = _bmm(t16, _block_diag((k_beta * exp_g).astype(BF16), bd_mask))

        sl = pl.ds(pl.multiple_of(it * GDN_GROUP, GDN_GROUP), GDN_GROUP)
        wq_ref[sl] = jnp.concatenate([w, q * exp_g], axis=1).astype(BF16)
        u_ref[sl] = u
        attn_ref[sl] = attn.astype(BF16)
        kdt_ref[sl] = jnp.swapaxes(k_dec, 1, 2).astype(BF16)
        dec_ref[sl] = jnp.broadcast_to(jnp.exp(g_last), (GDN_GROUP, 8, QUAD))
        return carry

    lax.fori_loop(0, nb // GDN_GROUP, local_pass, 0)

    e256 = e512_ref[0:QUAD, 0:QUAD]
    nw = nw_ref[:, 0:QUAD]

    def scan_step(c, carry):
        sl = pl.ds(pl.multiple_of(2 * c, 2), 2)
        state = state_ref[...]
        ws = _bmm(wq_ref[sl], state.astype(BF16))
        v_new = u_ref[sl] - ws[:, :GDN_CHUNK]
        v16 = v_new.astype(BF16)
        o = ws[:, GDN_CHUNK:] + _bmm(attn_ref[sl], _block_diag(v16, bd_mask))
        upd = _bmm(kdt_ref[sl], v16)
        state_ref[...] = state * dec_ref[sl][:, 0:1, :] + jnp.where(bd_mask, upd, 0.0)

        o2 = o.reshape(2 * GDN_CHUNK, QUAD)
        ms = jnp.dot((o2 * o2).astype(BF16), e256, preferred_element_type=F32) * (1.0 / GDN_DIM)
        o2 = o2 * lax.rsqrt(ms + NORM_EPS) * nw
        rows = pl.ds(pl.multiple_of(c * GDN_CHUNK, GDN_CHUNK), GDN_CHUNK)
        o_ref[rows, 0:QUAD] = o2[:GDN_CHUNK].astype(o_ref.dtype)
        o_ref[rows, QUAD:] = o2[GDN_CHUNK:].astype(o_ref.dtype)
        return carry

    lax.fori_loop(0, nc, scan_step, 0)


def _gdn(main2, logits, conv_w, alog_row, dtb_row, norm_row, batch, seq):
    rows = main2.shape[0]
    tg = min(512, seq)
    steps = seq // tg
    nb = 2 * (tg // GDN_CHUNK)
    head_of_lane = np.arange(GDN_WIDTH) // GDN_DIM
    e512 = jnp.asarray(head_of_lane[:, None] == head_of_lane[None, :], BF16)
    e_beta = jnp.asarray(np.arange(LOGIT_PAD)[:, None] == head_of_lane[None, :], BF16)
    e_g = jnp.asarray(np.arange(LOGIT_PAD)[:, None] == head_of_lane[None, :] + GDN_HEADS, BF16)
    tri = jnp.asarray(np.tril(np.ones((GDN_CHUNK, GDN_CHUNK))), BF16)

    cur = lambda c: pl.BlockSpec((tg, 512), lambda b, n: (b * steps + n, c))
    halo = lambda c: pl.BlockSpec(
        (16, 512), lambda b, n: (jnp.maximum((b * steps + n) * (tg // 16) - 1, 0), c))
    whole = lambda a: pl.BlockSpec(a.shape, lambda b, n: (0,) * a.ndim)
    consts = (conv_w, alog_row, dtb_row, norm_row, e512, e_beta, e_g, tri)
    return pl.pallas_call(
        functools.partial(_gdn_kernel, tg=tg),
        grid=(batch, steps),
        in_specs=[cur(10), cur(11), cur(12), halo(10), halo(11), halo(12),
                  pl.BlockSpec((tg, LOGIT_PAD), lambda b, n: (b * steps + n, 0))] + [whole(a) for a in consts],
        out_specs=pl.BlockSpec((tg, GDN_WIDTH), lambda b, n: (b * steps + n, 0)),
        out_shape=jax.ShapeDtypeStruct((rows, GDN_WIDTH), BF16),
        scratch_shapes=[
            pltpu.VMEM((tg + 8, GDN_WIDTH), F32),
            pltpu.VMEM((tg, GDN_WIDTH), F32),
            pltpu.VMEM((tg, GDN_WIDTH), F32),
            pltpu.VMEM((tg, GDN_WIDTH), F32),
            pltpu.VMEM((tg, GDN_WIDTH), F32),
            pltpu.VMEM((tg, GDN_WIDTH), BF16),
            pltpu.VMEM((tg, GDN_WIDTH), BF16),
            pltpu.VMEM((nb, 2 * GDN_CHUNK, QUAD), BF16),
            pltpu.VMEM((nb, GDN_CHUNK, QUAD), F32),
            pltpu.VMEM((nb, GDN_CHUNK, QUAD), BF16),
            pltpu.VMEM((nb, QUAD, GDN_CHUNK), BF16),
            pltpu.VMEM((nb, 8, QUAD), F32),
            pltpu.VMEM((2, QUAD, QUAD), F32),
        ],
        compiler_params=pltpu.CompilerParams(
            dimension_semantics=("parallel", "arbitrary"), vmem_limit_bytes=VMEM_LIMIT_BYTES),
        name="gdn",
    )(main2, main2, main2, main2, main2, main2, logits, *consts)


def _rope_tables(seq):
    inv_freq = ROPE_THETA ** (-jnp.arange(0, DSA_HEAD_DIM, 2, dtype=F32) / DSA_HEAD_DIM)
    ang = jnp.arange(seq, dtype=F32)[:, None] * inv_freq[None, :]
    ang = jnp.concatenate([ang, ang, ang, ang], axis=-1)
    sign = jnp.where((jnp.arange(LANES) % DSA_HEAD_DIM) < DSA_HEAD_DIM // 2, -1.0, 1.0).astype(F32)
    return jnp.cos(ang), jnp.sin(ang) * sign[None, :]


def kernel(x, norm_w, w_in, conv_w, a_log, dt_bias, gdn_norm_w, w_up_a, w_up_b, w_out, final_norm_w):
    batch, seq, _ = x.shape
    assert norm_w.shape[0] == 1, "the final RMSNorm is fused into the (single) layer's output kernel"
    cos_t, sin_t = _rope_tables(seq)
    x2 = x.reshape(batch * seq, D_MODEL)
    w = w_in[0]
    w_main = jnp.concatenate([w[:, :LOGIT_START], w[:, LOGIT_START + 2 * GDN_HEADS:]], axis=1).astype(BF16)
    w_logit = jnp.pad(w[:, LOGIT_START:LOGIT_START + 2 * GDN_HEADS],
                      ((0, 0), (0, LOGIT_PAD - 2 * GDN_HEADS))).astype(BF16)
    main2, logits = _in_proj(x2, norm_w[0][None, :], w_main, w_logit, cos_t, sin_t, seq)

    main3 = main2.reshape(batch, seq, MAIN_COLS)
    o_groups, lse_groups = [], []
    for g, (_, dilation) in enumerate(DSA_PATTERNS):
        o_g, lse_g = _dsa_attention(main3, g, dilation)
        o_groups.append(o_g)
        lse_groups.append(lse_g)

    pad8 = lambda p: jnp.pad(p.astype(F32), (GDN_HEADS, LOGIT_PAD - 2 * GDN_HEADS))[None, :]
    o_b = _gdn(main2, logits, conv_w[0], pad8(a_log[0]), pad8(dt_bias[0]),
               jnp.tile(gdn_norm_w[0].astype(F32), GDN_HEADS)[None, :], batch, seq)

    out = _out_proj(o_groups, lse_groups, main2, o_b, x2,
                    w_up_a[0].astype(BF16), w_up_b[0].astype(BF16), w_out[0].astype(BF16),
                    final_norm_w[None, :])
    return out.reshape(batch, seq, D_MODEL)
```

```python
import functools

import numpy as np
import jax
import jax.numpy as jnp
from jax import lax
from jax.experimental import pallas as pl
from jax.experimental.pallas import tpu as pltpu

F32 = jnp.float32
BF16 = jnp.bfloat16

D_MODEL = 1024
DSA_PATTERNS = ((128, 1), (512, 4), (2048, 16))
DSA_HEADS = 8
DSA_HEAD_DIM = 64
DSA_WIDTH = DSA_HEADS * DSA_HEAD_DIM
DSA_BLOCK = 128
ROPE_THETA = 10000.0
GDN_HEADS = 8
GDN_DIM = 64
GDN_WIDTH = GDN_HEADS * GDN_DIM
GDN_CONV = 4
GDN_CHUNK = 64
NORM_EPS = 1e-6

QKV_A_COLS = 3 * 3 * DSA_WIDTH
MAIN_COLS = 9216
LOGIT_START = 7168
LOGIT_PAD = 128
COL_TILE = 1536
N_COL_TILES = MAIN_COLS // COL_TILE
N_COL_BLOCKS_512 = MAIN_COLS // 512

VMEM_LIMIT_BYTES = 56 * 1024 * 1024
LANES = 128
NEG_BIG = -1e30


def _lane_iota(shape):
    return lax.broadcasted_iota(jnp.int32, shape, len(shape) - 1)


def _in_proj_kernel(x_ref, nw_ref, w_ref, wl_ref, cos_ref, sin_ref, main_ref, logit_ref, h_ref):
    j = pl.program_id(1)

    @pl.when(j == 0)
    def _():
        x = x_ref[...]
        ms = jnp.mean(x * x, axis=-1, keepdims=True)
        h = (x * lax.rsqrt(ms + NORM_EPS) * nw_ref[...]).astype(BF16)
        h_ref[...] = h
        logit_ref[...] = jnp.dot(h, wl_ref[...], preferred_element_type=F32)

    acc = jnp.dot(h_ref[...], w_ref[...], preferred_element_type=F32)

    @pl.when(j < 3)
    def _():
        cos = cos_ref[...]
        sin = sin_ref[...]
        low_half = (_lane_iota(cos.shape) % DSA_HEAD_DIM) < (DSA_HEAD_DIM // 2)
        for c in range(8):
            t = acc[:, c * LANES:(c + 1) * LANES]
            rot = jnp.where(low_half, pltpu.roll(t, LANES - 32, 1), pltpu.roll(t, 32, 1))
            r = t * cos + rot * sin
            if c < 4:
                r = r * (DSA_HEAD_DIM ** -0.5)
            main_ref[:, c * LANES:(c + 1) * LANES] = r.astype(BF16)
        main_ref[:, 8 * LANES:] = acc[:, 8 * LANES:].astype(BF16)

    @pl.when(j >= 3)
    def _():
        main_ref[...] = acc.astype(BF16)


def _in_proj(x2, norm_w, w_main, w_logit, cos_t, sin_t, seq):
    rows = x2.shape[0]
    tm = min(1024, seq)
    n_seq_tiles = seq // tm
    return pl.pallas_call(
        _in_proj_kernel,
        grid=(rows // tm, N_COL_TILES),
        in_specs=[
            pl.BlockSpec((tm, D_MODEL), lambda i, j: (i, 0)),
            pl.BlockSpec((1, D_MODEL), lambda i, j: (0, 0)),
            pl.BlockSpec((D_MODEL, COL_TILE), lambda i, j: (0, j)),
            pl.BlockSpec((D_MODEL, LOGIT_PAD), lambda i, j: (0, 0)),
            pl.BlockSpec((tm, LANES), lambda i, j: (i % n_seq_tiles, 0)),
            pl.BlockSpec((tm, LANES), lambda i, j: (i % n_seq_tiles, 0)),
        ],
        out_specs=[
            pl.BlockSpec((tm, COL_TILE), lambda i, j: (i, j)),
            pl.BlockSpec((tm, LOGIT_PAD), lambda i, j: (i, 0)),
        ],
        out_shape=[
            jax.ShapeDtypeStruct((rows, MAIN_COLS), BF16),
            jax.ShapeDtypeStruct((rows, LOGIT_PAD), F32),
        ],
        scratch_shapes=[pltpu.VMEM((tm, D_MODEL), BF16)],
        compiler_params=pltpu.CompilerParams(
            dimension_semantics=("parallel", "arbitrary"), vmem_limit_bytes=VMEM_LIMIT_BYTES),
        name="in_proj",
    )(x2, norm_w, w_main, w_logit, cos_t, sin_t)


DSA_SPAN = 2048
LSE_REP = LANES // DSA_HEADS


def _dsa_kernel(q_ref, k_ref, v_ref, o_ref, lse_ref, stage_ref, qp_ref, kp_ref, vp_ref, lstage_ref, *, d, span):
    first = pl.program_id(1) == 0
    npr = span // d
    n_sub = npr // DSA_BLOCK
    kstride = npr + DSA_BLOCK

    def regroup(src_ref, dst_ref, dst_stride, dst_off):
        if d == 1:
            dst_ref[dst_off:dst_off + npr, :] = src_ref[...]
            return
        for c in range(4):
            stage_ref[c] = src_ref[:, c * LANES:(c + 1) * LANES].astype(F32)
        for c in range(4):
            for r in range(d):
                lo = r * dst_stride + dst_off
                dst_ref[lo:lo + npr, c * LANES:(c + 1) * LANES] = (
                    stage_ref[c, pl.ds(r, npr, stride=d), :].astype(BF16))

    @pl.when(first)
    def _():
        for r in range(d):
            kp_ref[r * kstride:r * kstride + DSA_BLOCK, :] = jnp.zeros((DSA_BLOCK, DSA_WIDTH), BF16)
            vp_ref[r * kstride:r * kstride + DSA_BLOCK, :] = jnp.zeros((DSA_BLOCK, DSA_WIDTH), BF16)

    regroup(q_ref, qp_ref, npr, 0)
    regroup(k_ref, kp_ref, kstride, DSA_BLOCK)
    regroup(v_ref, vp_ref, kstride, DSA_BLOCK)

    qi = lax.broadcasted_iota(jnp.int32, (DSA_BLOCK, 2 * DSA_BLOCK), 0)
    kj = lax.broadcasted_iota(jnp.int32, (DSA_BLOCK, 2 * DSA_BLOCK), 1)
    band = (kj >= qi) & (kj <= qi + DSA_BLOCK)
    head_a = _lane_iota((DSA_BLOCK, LANES)) < DSA_HEAD_DIM
    lse_head = _lane_iota((DSA_BLOCK, LANES)) // LSE_REP

    def block(blk, carry):
        r, i = blk // n_sub, blk % n_sub
        q0 = pl.multiple_of(r * npr + i * DSA_BLOCK, DSA_BLOCK)
        k0 = pl.multiple_of(r * kstride + i * DSA_BLOCK, DSA_BLOCK)
        nat0 = r + d * DSA_BLOCK * i
        mask = band & (kj >= jnp.where(first & (i == 0), DSA_BLOCK, 0))
        lse_tile = jnp.zeros((DSA_BLOCK, LANES), F32)
        for hp in range(DSA_HEADS // 2):
            cols = slice(hp * LANES, (hp + 1) * LANES)
            q2 = qp_ref[pl.ds(q0, DSA_BLOCK), cols]
            k2 = kp_ref[pl.ds(k0, 2 * DSA_BLOCK), cols]
            v2 = vp_ref[pl.ds(k0, 2 * DSA_BLOCK), cols]
            outs, lses = [], []
            for is_a in (True, False):
                sel = head_a if is_a else jnp.logical_not(head_a)
                qh = jnp.where(sel, q2, jnp.zeros_like(q2))
                s = lax.dot_general(qh, k2, (((1,), (1,)), ((), ())), preferred_element_type=F32)
                s = jnp.where(mask, s, NEG_BIG)
                m = jnp.max(s, axis=-1, keepdims=True)
                p = jnp.exp(s - m)
                den = jnp.sum(p, axis=-1, keepdims=True)
                pv = jnp.dot(p.astype(BF16), v2, preferred_element_type=F32)
                outs.append(pv / den)
                lses.append(m + jnp.log(den))
            o_pair = jnp.where(head_a, outs[0], outs[1])
            lse_tile = jnp.where(lse_head == 2 * hp, lses[0], jnp.where(lse_head == 2 * hp + 1, lses[1], lse_tile))
            if d == 1:
                o_ref[pl.ds(q0, DSA_BLOCK), cols] = o_pair.astype(o_ref.dtype)
            else:
                stage_ref[hp, pl.ds(nat0, DSA_BLOCK, stride=d), :] = o_pair
        if d == 1:
            lse_ref[pl.ds(q0, DSA_BLOCK), :] = lse_tile
        else:
            lstage_ref[pl.ds(nat0, DSA_BLOCK, stride=d), :] = lse_tile
        return carry

    lax.fori_loop(0, d * n_sub, block, 0)

    for r in range(d):
        lo = r * kstride
        kp_ref[lo:lo + DSA_BLOCK, :] = kp_ref[lo + npr:lo + npr + DSA_BLOCK, :]
        vp_ref[lo:lo + DSA_BLOCK, :] = vp_ref[lo + npr:lo + npr + DSA_BLOCK, :]
    if d > 1:
        for c in range(4):
            o_ref[:, c * LANES:(c + 1) * LANES] = stage_ref[c].astype(o_ref.dtype)
        lse_ref[...] = lstage_ref[...]


def _dsa_attention(main2, group, dilation, batch, seq):
    span = min(DSA_SPAN, seq)
    steps = seq // span
    kv_rows = span + dilation * DSA_BLOCK
    col = lambda t: pl.BlockSpec((span, 512), lambda b, n: (b * steps + n, group * 3 + t))
    row = lambda width: pl.BlockSpec((span, width), lambda b, n: (b * steps + n, 0))
    return pl.pallas_call(
        functools.partial(_dsa_kernel, d=dilation, span=span),
        grid=(batch, steps),
        in_specs=[col(0), col(1), col(2)],
        out_specs=[row(DSA_WIDTH), row(LANES)],
        out_shape=[
            jax.ShapeDtypeStruct((batch * seq, DSA_WIDTH), BF16),
            jax.ShapeDtypeStruct((batch * seq, LANES), F32),
        ],
        scratch_shapes=[
            pltpu.VMEM((4, span, LANES), F32),
            pltpu.VMEM((span, DSA_WIDTH), BF16),
            pltpu.VMEM((kv_rows, DSA_WIDTH), BF16),
            pltpu.VMEM((kv_rows, DSA_WIDTH), BF16),
            pltpu.VMEM((span, LANES), F32),
        ],
        compiler_params=pltpu.CompilerParams(
            dimension_semantics=("parallel", "arbitrary"), vmem_limit_bytes=VMEM_LIMIT_BYTES),
        name=f"dsa_attn_d{dilation}",
    )(main2, main2, main2)


def _silu(x):
    return x * (1.0 / (1.0 + jnp.exp(-x)))


def _sigmoid(x):
    return 1.0 / (1.0 + jnp.exp(-x))


def _out_proj_kernel(o0_ref, o1_ref, o2_ref, l0_ref, l1_ref, l2_ref, za_ref, ob_ref, zb_ref, ga_ref, gb_ref,
                     x_ref, wa_ref, wb_ref, wo_ref, fw_ref, ex_ref, out_ref):
    l0, l1, l2 = l0_ref[...], l1_ref[...], l2_ref[...]
    mx = jnp.maximum(jnp.maximum(l0, l1), l2)
    e0, e1, e2 = jnp.exp(l0 - mx), jnp.exp(l1 - mx), jnp.exp(l2 - mx)
    inv = 1.0 / (e0 + e1 + e2)

    def per_lane(w):
        hi, lo = _split_hi_lo(w)
        return (jnp.dot(hi, ex_ref[...], preferred_element_type=F32)
                + jnp.dot(lo, ex_ref[...], preferred_element_type=F32))

    o_a = (per_lane(e0 * inv) * o0_ref[...].astype(F32) + per_lane(e1 * inv) * o1_ref[...].astype(F32)
           + per_lane(e2 * inv) * o2_ref[...].astype(F32))
    a_in = (o_a * _silu(za_ref[...].astype(F32))).astype(BF16)
    b_in = (ob_ref[...].astype(F32) * _silu(zb_ref[...].astype(F32))).astype(BF16)
    y_a = jnp.dot(a_in, wa_ref[...], preferred_element_type=F32)
    y_b = jnp.dot(b_in, wb_ref[...], preferred_element_type=F32)
    merged = _sigmoid(ga_ref[...].astype(F32)) * y_a + _sigmoid(gb_ref[...].astype(F32)) * y_b
    y = x_ref[...] + jnp.dot(merged.astype(BF16), wo_ref[...], preferred_element_type=F32)
    ms = jnp.mean(y * y, axis=-1, keepdims=True)
    out_ref[...] = y * lax.rsqrt(ms + NORM_EPS) * fw_ref[...]


def _out_proj(o_groups, lse_groups, main2, o_b, x2, w_up_a, w_up_b, w_out, final_w):
    rows = x2.shape[0]
    tm = min(512, rows)
    row128 = pl.BlockSpec((tm, LANES), lambda i: (i, 0))
    row512 = lambda c: pl.BlockSpec((tm, 512), lambda i: (i, c))
    row1024 = lambda c: pl.BlockSpec((tm, 1024), lambda i: (i, c))
    whole = lambda a: pl.BlockSpec(a.shape, lambda i: (0,) * a.ndim)
    expand = jnp.asarray(np.arange(LANES)[:, None] == LSE_REP * (np.arange(DSA_WIDTH)[None, :] // DSA_HEAD_DIM), BF16)
    return pl.pallas_call(
        _out_proj_kernel,
        grid=(rows // tm,),
        in_specs=[row512(0)] * 3 + [row128] * 3 + [
            row512(9), row512(0), row512(13), row1024(7), row1024(8), row1024(0),
            whole(w_up_a), whole(w_up_b), whole(w_out), whole(final_w), whole(expand)],
        out_specs=pl.BlockSpec((tm, D_MODEL), lambda i: (i, 0)),
        out_shape=jax.ShapeDtypeStruct((rows, D_MODEL), F32),
        compiler_params=pltpu.CompilerParams(
            dimension_semantics=("parallel",), vmem_limit_bytes=VMEM_LIMIT_BYTES),
        name="out_proj",
    )(*o_groups, *lse_groups, main2, o_b, main2, main2, main2, x2, w_up_a, w_up_b, w_out, final_w, expand)


QUAD = 4 * GDN_DIM
GDN_GROUP = 4


def _split_hi_lo(x):
    hi = x.astype(BF16)
    lo = (x - hi.astype(F32)).astype(BF16)
    return hi, lo


def _softplus(x):
    return jnp.maximum(x, 0.0) + jnp.log1p(jnp.exp(-jnp.abs(x)))


def _bmm(a, b):
    return lax.dot_general(a, b, (((2,), (1,)), ((0,), (0,))), preferred_element_type=F32)


def _bmm_nt(a, b):
    return lax.dot_general(a, b, (((2,), (2,)), ((0,), (0,))), preferred_element_type=F32)


def _block_diag(x, bd_mask):
    t = jnp.concatenate([x, x, x, x], axis=1)
    return jnp.where(bd_mask, t, jnp.zeros_like(t))


def _gdn_kernel(q_ref, k_ref, v_ref, qh_ref, kh_ref, vh_ref, lg_ref, cw_ref, alog_ref, dtb_ref, nw_ref,
                e512_ref, eb_ref, eg_ref, tri_ref, o_ref,
                xs_ref, qn_ref, kn_ref, vn_ref, beta_ref, gh_ref, gl_ref,
                wq_ref, u_ref, attn_ref, kdt_ref, dec_ref, state_ref, *, tg):
    first = pl.program_id(1) == 0
    nc = tg // GDN_CHUNK
    nb = 2 * nc

    @pl.when(first)
    def _():
        state_ref[...] = jnp.zeros_like(state_ref)

    def conv_silu(x_ref, halo_ref, col0):
        xs_ref[8:8 + tg, :] = x_ref[...].astype(F32)
        halo = halo_ref[...].astype(F32)[8:16, :]
        xs_ref[0:8, :] = jnp.where(first, jnp.zeros_like(halo), halo)
        w = cw_ref[:, col0:col0 + GDN_WIDTH]
        y = w[0:1, :] * xs_ref[5:5 + tg, :]
        for j in range(1, GDN_CONV):
            y = y + w[j:j + 1, :] * xs_ref[5 + j:5 + j + tg, :]
        return _silu(y)

    def l2norm(y, scale):
        ss = jnp.dot((y * y).astype(BF16), e512_ref[...], preferred_element_type=F32)
        return y * (lax.rsqrt(ss + NORM_EPS) * scale)

    qn_ref[...] = l2norm(conv_silu(q_ref, qh_ref, 0), GDN_DIM ** -0.5)
    kn_ref[...] = l2norm(conv_silu(k_ref, kh_ref, GDN_WIDTH), 1.0)
    vn_ref[...] = conv_silu(v_ref, vh_ref, 2 * GDN_WIDTH)

    logits = lg_ref[...]
    beta_hi, beta_lo = _split_hi_lo(_sigmoid(logits))
    g_hi, g_lo = _split_hi_lo(-jnp.exp(alog_ref[...]) * _softplus(logits + dtb_ref[...]))
    beta_ref[...] = (jnp.dot(beta_hi, eb_ref[...], preferred_element_type=F32)
                     + jnp.dot(beta_lo, eb_ref[...], preferred_element_type=F32))
    gh_ref[...] = jnp.dot(g_hi, eg_ref[...], preferred_element_type=F32).astype(BF16)
    gl_ref[...] = jnp.dot(g_lo, eg_ref[...], preferred_element_type=F32).astype(BF16)

    row = lax.broadcasted_iota(jnp.int32, (GDN_CHUNK, QUAD), 0)
    col = lax.broadcasted_iota(jnp.int32, (GDN_CHUNK, QUAD), 1) % GDN_CHUNK
    incl = row >= col
    strict = row > col
    eye = (row == col).astype(F32)
    bd_mask = (lax.broadcasted_iota(jnp.int32, (QUAD, QUAD), 0) // GDN_DIM
               == lax.broadcasted_iota(jnp.int32, (QUAD, QUAD), 1) // GDN_DIM)
    tri = tri_ref[...]

    def chunk_quads(ref, c0, n):
        a = ref[pl.ds(pl.multiple_of(c0 * GDN_CHUNK, GDN_CHUNK), n * GDN_CHUNK), :]
        a = a.reshape(n, GDN_CHUNK, GDN_WIDTH)
        return jnp.stack([a[:, :, :QUAD], a[:, :, QUAD:]], axis=1).reshape(2 * n, GDN_CHUNK, QUAD)

    n_per = GDN_GROUP // 2

    def local_pass(it, carry):
        c0 = it * n_per
        q, k, v = (chunk_quads(r, c0, n_per) for r in (qn_ref, kn_ref, vn_ref))
        beta = chunk_quads(beta_ref, c0, n_per)
        gh, gl = chunk_quads(gh_ref, c0, n_per), chunk_quads(gl_ref, c0, n_per)

        zero = jnp.zeros_like(gh)
        rhs_hi = jnp.concatenate([gh, jnp.where(strict, gh, zero)], axis=-1)
        rhs_lo = jnp.concatenate([gl, jnp.where(strict, gl, zero)], axis=-1)
        tri_b = jnp.broadcast_to(tri, (GDN_GROUP, GDN_CHUNK, GDN_CHUNK))
        gd = _bmm(tri_b, rhs_hi) + _bmm(tri_b, rhs_lo)
        g_cum, d_pair = gd[:, :, :QUAD], gd[:, :, QUAD:]
        decay_incl = jnp.where(incl, jnp.exp(d_pair), 0.0)
        decay_strict = jnp.where(strict, decay_incl, 0.0)
        exp_g = jnp.exp(g_cum)
        g_last = g_cum[:, GDN_CHUNK - 1:GDN_CHUNK, :]
        k_dec = k * jnp.exp(g_last - g_cum)
        k_beta = k * beta

        k_bd = _block_diag(k.astype(BF16), bd_mask)
        aa = _bmm_nt(jnp.concatenate([k_beta, q], axis=1).astype(BF16), k_bd)
        a = aa[:, :GDN_CHUNK] * decay_strict
        attn = aa[:, GDN_CHUNK:] * decay_incl

        p = a
        t = eye - a
        p = _bmm(p.astype(BF16), _block_diag(p.astype(BF16), bd_mask))
        for _ in range(4):
            r = _bmm(jnp.concatenate([p, t], axis=1).astype(BF16), _block_diag(p.astype(BF16), bd_mask))
            p = r[:, :GDN_CHUNK]
            t = t + r[:, GDN_CHUNK:]
        t = t + _bmm(t.astype(BF16), _block_diag(p.astype(BF16), bd_mask))
        t16 = t.astype(BF16)
        u = _bmm(t16, _block_diag((v * beta).astype(BF16), bd_mask))
        w = _bmm(t16, _block_diag((k_beta * exp_g).astype(BF16), bd_mask))

        sl = pl.ds(pl.multiple_of(it * GDN_GROUP, GDN_GROUP), GDN_GROUP)
        wq_ref[sl] = jnp.concatenate([w, q * exp_g], axis=1).astype(BF16)
        u_ref[sl] = u
        attn_ref[sl] = attn.astype(BF16)
        kdt_ref[sl] = jnp.swapaxes(k_dec, 1, 2).astype(BF16)
        dec_ref[sl] = jnp.broadcast_to(jnp.exp(g_last), (GDN_GROUP, 8, QUAD))
        return carry

    lax.fori_loop(0, nb // GDN_GROUP, local_pass, 0)

    e256 = e512_ref[0:QUAD, 0:QUAD]
    nw = nw_ref[:, 0:QUAD]

    def scan_step(c, carry):
        sl = pl.ds(pl.multiple_of(2 * c, 2), 2)
        state = state_ref[...]
        ws = _bmm(wq_ref[sl], state.astype(BF16))
        v_new = u_ref[sl] - ws[:, :GDN_CHUNK]
        v16 = v_new.astype(BF16)
        o = ws[:, GDN_CHUNK:] + _bmm(attn_ref[sl], _block_diag(v16, bd_mask))
        upd = _bmm(kdt_ref[sl], v16)
        state_ref[...] = state * dec_ref[sl][:, 0:1, :] + jnp.where(bd_mask, upd, 0.0)

        o2 = o.reshape(2 * GDN_CHUNK, QUAD)
        ms = jnp.dot((o2 * o2).astype(BF16), e256, preferred_element_type=F32) * (1.0 / GDN_DIM)
        o2 = o2 * lax.rsqrt(ms + NORM_EPS) * nw
        rows = pl.ds(pl.multiple_of(c * GDN_CHUNK, GDN_CHUNK), GDN_CHUNK)
        o_ref[rows, 0:QUAD] = o2[:GDN_CHUNK].astype(o_ref.dtype)
        o_ref[rows, QUAD:] = o2[GDN_CHUNK:].astype(o_ref.dtype)
        return carry

    lax.fori_loop(0, nc, scan_step, 0)


def _gdn(main2, logits, conv_w, alog_row, dtb_row, norm_row, batch, seq):
    rows = main2.shape[0]
    tg = min(512, seq)
    steps = seq // tg
    nb = 2 * (tg // GDN_CHUNK)
    head_of_lane = np.arange(GDN_WIDTH) // GDN_DIM
    e512 = jnp.asarray(head_of_lane[:, None] == head_of_lane[None, :], BF16)
    e_beta = jnp.asarray(np.arange(LOGIT_PAD)[:, None] == head_of_lane[None, :], BF16)
    e_g = jnp.asarray(np.arange(LOGIT_PAD)[:, None] == head_of_lane[None, :] + GDN_HEADS, BF16)
    tri = jnp.asarray(np.tril(np.ones((GDN_CHUNK, GDN_CHUNK))), BF16)

    cur = lambda c: pl.BlockSpec((tg, 512), lambda b, n: (b * steps + n, c))
    halo = lambda c: pl.BlockSpec(
        (16, 512), lambda b, n: (jnp.maximum((b * steps + n) * (tg // 16) - 1, 0), c))
    whole = lambda a: pl.BlockSpec(a.shape, lambda b, n: (0,) * a.ndim)
    consts = (conv_w, alog_row, dtb_row, norm_row, e512, e_beta, e_g, tri)
    return pl.pallas_call(
        functools.partial(_gdn_kernel, tg=tg),
        grid=(batch, steps),
        in_specs=[cur(10), cur(11), cur(12), halo(10), halo(11), halo(12),
                  pl.BlockSpec((tg, LOGIT_PAD), lambda b, n: (b * steps + n, 0))] + [whole(a) for a in consts],
        out_specs=pl.BlockSpec((tg, GDN_WIDTH), lambda b, n: (b * steps + n, 0)),
        out_shape=jax.ShapeDtypeStruct((rows, GDN_WIDTH), BF16),
        scratch_shapes=[
            pltpu.VMEM((tg + 8, GDN_WIDTH), F32),
            pltpu.VMEM((tg, GDN_WIDTH), F32),
            pltpu.VMEM((tg, GDN_WIDTH), F32),
            pltpu.VMEM((tg, GDN_WIDTH), F32),
            pltpu.VMEM((tg, GDN_WIDTH), F32),
            pltpu.VMEM((tg, GDN_WIDTH), BF16),
            pltpu.VMEM((tg, GDN_WIDTH), BF16),
            pltpu.VMEM((nb, 2 * GDN_CHUNK, QUAD), BF16),
            pltpu.VMEM((nb, GDN_CHUNK, QUAD), F32),
            pltpu.VMEM((nb, GDN_CHUNK, QUAD), BF16),
            pltpu.VMEM((nb, QUAD, GDN_CHUNK), BF16),
            pltpu.VMEM((nb, 8, QUAD), F32),
            pltpu.VMEM((2, QUAD, QUAD), F32),
        ],
        compiler_params=pltpu.CompilerParams(
            dimension_semantics=("parallel", "arbitrary"), vmem_limit_bytes=VMEM_LIMIT_BYTES),
        name="gdn",
    )(main2, main2, main2, main2, main2, main2, logits, *consts)


def _rope_tables(seq):
    inv_freq = ROPE_THETA ** (-jnp.arange(0, DSA_HEAD_DIM, 2, dtype=F32) / DSA_HEAD_DIM)
    ang = jnp.arange(seq, dtype=F32)[:, None] * inv_freq[None, :]
    ang = jnp.concatenate([ang, ang, ang, ang], axis=-1)
    sign = jnp.where((jnp.arange(LANES) % DSA_HEAD_DIM) < DSA_HEAD_DIM // 2, -1.0, 1.0).astype(F32)
    return jnp.cos(ang), jnp.sin(ang) * sign[None, :]


def kernel(x, norm_w, w_in, conv_w, a_log, dt_bias, gdn_norm_w, w_up_a, w_up_b, w_out, final_norm_w):
    batch, seq, _ = x.shape
    assert norm_w.shape[0] == 1, "the final RMSNorm is fused into the (single) layer's output kernel"
    cos_t, sin_t = _rope_tables(seq)
    x2 = x.reshape(batch * seq, D_MODEL)
    w = w_in[0]
    w_main = jnp.concatenate([w[:, :LOGIT_START], w[:, LOGIT_START + 2 * GDN_HEADS:]], axis=1).astype(BF16)
    w_logit = jnp.pad(w[:, LOGIT_START:LOGIT_START + 2 * GDN_HEADS],
                      ((0, 0), (0, LOGIT_PAD - 2 * GDN_HEADS))).astype(BF16)
    main2, logits = _in_proj(x2, norm_w[0][None, :], w_main, w_logit, cos_t, sin_t, seq)

    o_groups, lse_groups = [], []
    for g, (_, dilation) in enumerate(DSA_PATTERNS):
        o_g, lse_g = _dsa_attention(main2, g, dilation, batch, seq)
        o_groups.append(o_g)
        lse_groups.append(lse_g)

    pad8 = lambda p: jnp.pad(p.astype(F32), (GDN_HEADS, LOGIT_PAD - 2 * GDN_HEADS))[None, :]
    o_b = _gdn(main2, logits, conv_w[0], pad8(a_log[0]), pad8(dt_bias[0]),
               jnp.tile(gdn_norm_w[0].astype(F32), GDN_HEADS)[None, :], batch, seq)

    out = _out_proj(o_groups, lse_groups, main2, o_b, x2,
                    w_up_a[0].astype(BF16), w_up_b[0].astype(BF16), w_out[0].astype(BF16),
                    final_norm_w[None, :])
    return out.reshape(batch, seq, D_MODEL)
```

```python
import functools

import numpy as np
import jax
import jax.numpy as jnp
from jax import lax
from jax.experimental import pallas as pl
from jax.experimental.pallas import tpu as pltpu

F32 = jnp.float32
BF16 = jnp.bfloat16

D_MODEL = 1024
DSA_PATTERNS = ((128, 1), (512, 4), (2048, 16))
DSA_HEADS = 8
DSA_HEAD_DIM = 64
DSA_WIDTH = DSA_HEADS * DSA_HEAD_DIM
DSA_BLOCK = 128
ROPE_THETA = 10000.0
GDN_HEADS = 8
GDN_DIM = 64
GDN_WIDTH = GDN_HEADS * GDN_DIM
GDN_CONV = 4
GDN_CHUNK = 64
NORM_EPS = 1e-6

QKV_COLS = 3 * 3 * DSA_WIDTH
GROUP_COLS = 3 * DSA_WIDTH
REST_COLS = 4608
REST_GATE_A, REST_GATE_B = 0, 1
REST_DSA_Z, REST_GDN_Q, REST_GDN_Z = 4, 5, 8
LOGIT_START = 7168
LOGIT_PAD = 128
PROJ_ROWS = 512

VMEM_LIMIT_BYTES = 56 * 1024 * 1024
LANES = 128
NEG_BIG = -1e30


def _lane_iota(shape):
    return lax.broadcasted_iota(jnp.int32, shape, len(shape) - 1)


def _in_proj_qkv_kernel(x_ref, nw_ref, w_ref, cos_ref, sin_ref, qkv_ref, h_ref):
    x = x_ref[...]
    ms = jnp.mean(x * x, axis=-1, keepdims=True)
    h = (x * lax.rsqrt(ms + NORM_EPS) * nw_ref[...]).astype(BF16)
    h_ref[...] = h
    cos = cos_ref[...]
    sin = sin_ref[...]
    low_half = (_lane_iota(cos.shape) % DSA_HEAD_DIM) < (DSA_HEAD_DIM // 2)
    for g in range(len(DSA_PATTERNS)):
        c0 = g * GROUP_COLS
        acc = jnp.dot(h, w_ref[:, c0:c0 + GROUP_COLS], preferred_element_type=F32)
        for c in range(8):
            t = acc[:, c * LANES:(c + 1) * LANES]
            rot = jnp.where(low_half, pltpu.roll(t, LANES - 32, 1), pltpu.roll(t, 32, 1))
            r = t * cos + rot * sin
            if c < 4:
                r = r * (DSA_HEAD_DIM ** -0.5)
            qkv_ref[:, c0 + c * LANES:c0 + (c + 1) * LANES] = r.astype(BF16)
        qkv_ref[:, c0 + 8 * LANES:c0 + GROUP_COLS] = acc[:, 8 * LANES:].astype(BF16)


def _in_proj_rest_kernel(h_ref, w_ref, wl_ref, rest_ref, logit_ref):
    h = h_ref[...]
    logit_ref[...] = jnp.dot(h, wl_ref[...], preferred_element_type=F32)
    for g in range(REST_COLS // GROUP_COLS):
        c0 = g * GROUP_COLS
        rest_ref[:, c0:c0 + GROUP_COLS] = jnp.dot(
            h, w_ref[:, c0:c0 + GROUP_COLS], preferred_element_type=F32).astype(BF16)


def _resident(shape):
    return pl.BlockSpec(shape, lambda i: (0,) * len(shape), pipeline_mode=pl.Buffered(1))


def _in_proj(x2, norm_w, w_qkv, w_rest, w_logit, cos_t, sin_t, seq):
    rows = x2.shape[0]
    tm = min(PROJ_ROWS, seq)
    n_seq_tiles = seq // tm
    row = lambda width: pl.BlockSpec((tm, width), lambda i: (i, 0))
    table = pl.BlockSpec((tm, LANES), lambda i: (i % n_seq_tiles, 0))
    params = pltpu.CompilerParams(dimension_semantics=("parallel",), vmem_limit_bytes=VMEM_LIMIT_BYTES)
    qkv, h = pl.pallas_call(
        _in_proj_qkv_kernel,
        grid=(rows // tm,),
        in_specs=[row(D_MODEL), _resident((1, D_MODEL)), _resident((D_MODEL, QKV_COLS)), table, table],
        out_specs=[row(QKV_COLS), row(D_MODEL)],
        out_shape=[jax.ShapeDtypeStruct((rows, QKV_COLS), BF16), jax.ShapeDtypeStruct((rows, D_MODEL), BF16)],
        compiler_params=params,
        name="in_proj_qkv",
    )(x2, norm_w, w_qkv, cos_t, sin_t)
    rest, logits = pl.pallas_call(
        _in_proj_rest_kernel,
        grid=(rows // tm,),
        in_specs=[row(D_MODEL), _resident((D_MODEL, REST_COLS)), _resident((D_MODEL, LOGIT_PAD))],
        out_specs=[row(REST_COLS), row(LOGIT_PAD)],
        out_shape=[jax.ShapeDtypeStruct((rows, REST_COLS), BF16), jax.ShapeDtypeStruct((rows, LOGIT_PAD), F32)],
        compiler_params=params,
        name="in_proj_rest",
    )(h, w_rest, w_logit)
    return qkv, rest, logits


DSA_SPAN = 2048
LSE_REP = LANES // DSA_HEADS


def _dsa_kernel(q_ref, k_ref, v_ref, o_ref, lse_ref, stage_ref, qp_ref, kp_ref, vp_ref, lstage_ref, tmp_ref=None,
                *, d, span):
    first = pl.program_id(1) == 0
    npr = span // d
    n_sub = npr // DSA_BLOCK
    kstride = npr + DSA_BLOCK

    def regroup(src_ref, dst_ref, dst_stride, dst_off):
        if d == 1:
            dst_ref[dst_off:dst_off + npr, :] = src_ref[...]
            return
        for c in range(4):
            stage_ref[c] = src_ref[:, c * LANES:(c + 1) * LANES].astype(F32)
        if d > 4:
            quarter = span // 4
            for c in range(4):
                for r4 in range(4):
                    tmp_ref[c, r4 * quarter:(r4 + 1) * quarter, :] = stage_ref[c, pl.ds(r4, quarter, stride=4), :]
            slabs, stride, start = tmp_ref, d // 4, lambda r: (r % 4) * quarter + r // 4
        else:
            slabs, stride, start = stage_ref, d, lambda r: r
        for c in range(4):
            for r in range(d):
                lo = r * dst_stride + dst_off
                dst_ref[lo:lo + npr, c * LANES:(c + 1) * LANES] = (
                    slabs[c, pl.ds(start(r), npr, stride=stride), :].astype(BF16))

    @pl.when(first)
    def _():
        for r in range(d):
            kp_ref[r * kstride:r * kstride + DSA_BLOCK, :] = jnp.zeros((DSA_BLOCK, DSA_WIDTH), BF16)
            vp_ref[r * kstride:r * kstride + DSA_BLOCK, :] = jnp.zeros((DSA_BLOCK, DSA_WIDTH), BF16)

    regroup(q_ref, qp_ref, npr, 0)
    regroup(k_ref, kp_ref, kstride, DSA_BLOCK)
    regroup(v_ref, vp_ref, kstride, DSA_BLOCK)

    qi = lax.broadcasted_iota(jnp.int32, (DSA_BLOCK, 2 * DSA_BLOCK), 0)
    kj = lax.broadcasted_iota(jnp.int32, (DSA_BLOCK, 2 * DSA_BLOCK), 1)
    band = (kj >= qi) & (kj <= qi + DSA_BLOCK)
    head_a = _lane_iota((DSA_BLOCK, LANES)) < DSA_HEAD_DIM
    lse_head = _lane_iota((DSA_BLOCK, LANES)) // LSE_REP

    def block(blk, carry):
        r, i = blk // n_sub, blk % n_sub
        q0 = pl.multiple_of(r * npr + i * DSA_BLOCK, DSA_BLOCK)
        k0 = pl.multiple_of(r * kstride + i * DSA_BLOCK, DSA_BLOCK)
        nat0 = r + d * DSA_BLOCK * i
        mask = band & (kj >= jnp.where(first & (i == 0), DSA_BLOCK, 0))
        lse_tile = jnp.zeros((DSA_BLOCK, LANES), F32)
        for hp in range(DSA_HEADS // 2):
            cols = slice(hp * LANES, (hp + 1) * LANES)
            q2 = qp_ref[pl.ds(q0, DSA_BLOCK), cols]
            k2 = kp_ref[pl.ds(k0, 2 * DSA_BLOCK), cols]
            v2 = vp_ref[pl.ds(k0, 2 * DSA_BLOCK), cols]
            outs, lses = [], []
            for is_a in (True, False):
                sel = head_a if is_a else jnp.logical_not(head_a)
                qh = jnp.where(sel, q2, jnp.zeros_like(q2))
                s = lax.dot_general(qh, k2, (((1,), (1,)), ((), ())), preferred_element_type=F32)
                s = jnp.where(mask, s, NEG_BIG)
                m = jnp.max(s, axis=-1, keepdims=True)
                p = jnp.exp(s - m)
                den = jnp.sum(p, axis=-1, keepdims=True)
                pv = jnp.dot(p.astype(BF16), v2, preferred_element_type=F32)
                outs.append(pv / den)
                lses.append(m + jnp.log(den))
            o_pair = jnp.where(head_a, outs[0], outs[1])
            lse_tile = jnp.where(lse_head == 2 * hp, lses[0], jnp.where(lse_head == 2 * hp + 1, lses[1], lse_tile))
            if d == 1:
                o_ref[pl.ds(q0, DSA_BLOCK), cols] = o_pair.astype(o_ref.dtype)
            else:
                stage_ref[hp, pl.ds(nat0, DSA_BLOCK, stride=d), :] = o_pair
        if d == 1:
            lse_ref[pl.ds(q0, DSA_BLOCK), :] = lse_tile
        else:
            lstage_ref[pl.ds(nat0, DSA_BLOCK, stride=d), :] = lse_tile
        return carry

    lax.fori_loop(0, d * n_sub, block, 0, unroll=4)

    for r in range(d):
        lo = r * kstride
        kp_ref[lo:lo + DSA_BLOCK, :] = kp_ref[lo + npr:lo + npr + DSA_BLOCK, :]
        vp_ref[lo:lo + DSA_BLOCK, :] = vp_ref[lo + npr:lo + npr + DSA_BLOCK, :]
    if d > 1:
        for c in range(4):
            o_ref[:, c * LANES:(c + 1) * LANES] = stage_ref[c].astype(o_ref.dtype)
        lse_ref[...] = lstage_ref[...]


def _dsa_attention(main2, group, dilation, batch, seq):
    span = min(DSA_SPAN, seq)
    steps = seq // span
    kv_rows = span + dilation * DSA_BLOCK
    col = lambda t: pl.BlockSpec((span, 512), lambda b, n: (b * steps + n, group * 3 + t))
    row = lambda width: pl.BlockSpec((span, width), lambda b, n: (b * steps + n, 0))
    return pl.pallas_call(
        functools.partial(_dsa_kernel, d=dilation, span=span),
        grid=(batch, steps),
        in_specs=[col(0), col(1), col(2)],
        out_specs=[row(DSA_WIDTH), row(LANES)],
        out_shape=[
            jax.ShapeDtypeStruct((batch * seq, DSA_WIDTH), BF16),
            jax.ShapeDtypeStruct((batch * seq, LANES), F32),
        ],
        scratch_shapes=[
            pltpu.VMEM((4, span, LANES), F32),
            pltpu.VMEM((span, DSA_WIDTH), BF16),
            pltpu.VMEM((kv_rows, DSA_WIDTH), BF16),
            pltpu.VMEM((kv_rows, DSA_WIDTH), BF16),
            pltpu.VMEM((span, LANES), F32),
        ] + ([pltpu.VMEM((4, span, LANES), F32)] if dilation > 4 else []),
        compiler_params=pltpu.CompilerParams(
            dimension_semantics=("parallel", "arbitrary"), vmem_limit_bytes=VMEM_LIMIT_BYTES),
        name=f"dsa_attn_d{dilation}",
    )(main2, main2, main2)


def _silu(x):
    return x * (1.0 / (1.0 + jnp.exp(-x)))


def _sigmoid(x):
    return 1.0 / (1.0 + jnp.exp(-x))


def _out_proj_kernel(o0_ref, o1_ref, o2_ref, l0_ref, l1_ref, l2_ref, za_ref, ob_ref, zb_ref, ga_ref, gb_ref,
                     x_ref, wa_ref, wb_ref, wo_ref, fw_ref, ex_ref, out_ref):
    l0, l1, l2 = l0_ref[...], l1_ref[...], l2_ref[...]
    mx = jnp.maximum(jnp.maximum(l0, l1), l2)
    e0, e1, e2 = jnp.exp(l0 - mx), jnp.exp(l1 - mx), jnp.exp(l2 - mx)
    inv = 1.0 / (e0 + e1 + e2)

    def per_lane(w):
        hi, lo = _split_hi_lo(w)
        return (jnp.dot(hi, ex_ref[...], preferred_element_type=F32)
                + jnp.dot(lo, ex_ref[...], preferred_element_type=F32))

    o_a = (per_lane(e0 * inv) * o0_ref[...].astype(F32) + per_lane(e1 * inv) * o1_ref[...].astype(F32)
           + per_lane(e2 * inv) * o2_ref[...].astype(F32))
    a_in = (o_a * _silu(za_ref[...].astype(F32))).astype(BF16)
    b_in = (ob_ref[...].astype(F32) * _silu(zb_ref[...].astype(F32))).astype(BF16)
    y_a = jnp.dot(a_in, wa_ref[...], preferred_element_type=F32)
    y_b = jnp.dot(b_in, wb_ref[...], preferred_element_type=F32)
    merged = _sigmoid(ga_ref[...].astype(F32)) * y_a + _sigmoid(gb_ref[...].astype(F32)) * y_b
    y = x_ref[...] + jnp.dot(merged.astype(BF16), wo_ref[...], preferred_element_type=F32)
    ms = jnp.mean(y * y, axis=-1, keepdims=True)
    out_ref[...] = y * lax.rsqrt(ms + NORM_EPS) * fw_ref[...]


def _out_proj(o_groups, lse_groups, rest, o_b, x2, w_up_a, w_up_b, w_out, final_w):
    rows = x2.shape[0]
    tm = min(512, rows)
    row128 = pl.BlockSpec((tm, LANES), lambda i: (i, 0))
    row512 = lambda c: pl.BlockSpec((tm, 512), lambda i: (i, c))
    row1024 = lambda c: pl.BlockSpec((tm, 1024), lambda i: (i, c))
    whole = lambda a: pl.BlockSpec(a.shape, lambda i: (0,) * a.ndim)
    expand = jnp.asarray(np.arange(LANES)[:, None] == LSE_REP * (np.arange(DSA_WIDTH)[None, :] // DSA_HEAD_DIM), BF16)
    return pl.pallas_call(
        _out_proj_kernel,
        grid=(rows // tm,),
        in_specs=[row512(0)] * 3 + [row128] * 3 + [
            row512(REST_DSA_Z), row512(0), row512(REST_GDN_Z), row1024(REST_GATE_A), row1024(REST_GATE_B),
            row1024(0),
            whole(w_up_a), whole(w_up_b), whole(w_out), whole(final_w), whole(expand)],
        out_specs=pl.BlockSpec((tm, D_MODEL), lambda i: (i, 0)),
        out_shape=jax.ShapeDtypeStruct((rows, D_MODEL), F32),
        compiler_params=pltpu.CompilerParams(
            dimension_semantics=("parallel",), vmem_limit_bytes=VMEM_LIMIT_BYTES),
        name="out_proj",
    )(*o_groups, *lse_groups, rest, o_b, rest, rest, rest, x2, w_up_a, w_up_b, w_out, final_w, expand)


QUAD = 4 * GDN_DIM
GDN_GROUP = 4


def _split_hi_lo(x):
    hi = x.astype(BF16)
    lo = (x - hi.astype(F32)).astype(BF16)
    return hi, lo


def _softplus(x):
    return jnp.maximum(x, 0.0) + jnp.log1p(jnp.exp(-jnp.abs(x)))


def _bmm(a, b):
    return lax.dot_general(a, b, (((2,), (1,)), ((0,), (0,))), preferred_element_type=F32)


def _bmm_nt(a, b):
    return lax.dot_general(a, b, (((2,), (2,)), ((0,), (0,))), preferred_element_type=F32)


def _block_diag(x, bd_mask):
    t = jnp.concatenate([x, x, x, x], axis=1)
    return jnp.where(bd_mask, t, jnp.zeros_like(t))


def _gdn_kernel(q_ref, k_ref, v_ref, qh_ref, kh_ref, vh_ref, lg_ref, cw_ref, alog_ref, dtb_ref, nw_ref,
                e512_ref, eb_ref, eg_ref, tri_ref, o_ref,
                xs_ref, qn_ref, kn_ref, vn_ref, beta_ref, gh_ref, gl_ref,
                wq_ref, u_ref, attn_ref, kdt_ref, dec_ref, state_ref, *, tg):
    first = pl.program_id(1) == 0
    nc = tg // GDN_CHUNK
    nb = 2 * nc

    @pl.when(first)
    def _():
        state_ref[...] = jnp.zeros_like(state_ref)

    def conv_silu(x_ref, halo_ref, col0):
        xs_ref[8:8 + tg, :] = x_ref[...].astype(F32)
        halo = halo_ref[...].astype(F32)[8:16, :]
        xs_ref[0:8, :] = jnp.where(first, jnp.zeros_like(halo), halo)
        w = cw_ref[:, col0:col0 + GDN_WIDTH]
        y = w[0:1, :] * xs_ref[5:5 + tg, :]
        for j in range(1, GDN_CONV):
            y = y + w[j:j + 1, :] * xs_ref[5 + j:5 + j + tg, :]
        return _silu(y)

    def l2norm(y, scale):
        ss = jnp.dot((y * y).astype(BF16), e512_ref[...], preferred_element_type=F32)
        return y * (lax.rsqrt(ss + NORM_EPS) * scale)

    qn_ref[...] = l2norm(conv_silu(q_ref, qh_ref, 0), GDN_DIM ** -0.5)
    kn_ref[...] = l2norm(conv_silu(k_ref, kh_ref, GDN_WIDTH), 1.0)
    vn_ref[...] = conv_silu(v_ref, vh_ref, 2 * GDN_WIDTH)

    logits = lg_ref[...]
    beta_hi, beta_lo = _split_hi_lo(_sigmoid(logits))
    g_hi, g_lo = _split_hi_lo(-jnp.exp(alog_ref[...]) * _softplus(logits + dtb_ref[...]))
    beta_ref[...] = (jnp.dot(beta_hi, eb_ref[...], preferred_element_type=F32)
                     + jnp.dot(beta_lo, eb_ref[...], preferred_element_type=F32))
    gh_ref[...] = jnp.dot(g_hi, eg_ref[...], preferred_element_type=F32).astype(BF16)
    gl_ref[...] = jnp.dot(g_lo, eg_ref[...], preferred_element_type=F32).astype(BF16)

    row = lax.broadcasted_iota(jnp.int32, (GDN_CHUNK, QUAD), 0)
    col = lax.broadcasted_iota(jnp.int32, (GDN_CHUNK, QUAD), 1) % GDN_CHUNK
    incl = row >= col
    strict = row > col
    eye = (row == col).astype(F32)
    bd_mask = (lax.broadcasted_iota(jnp.int32, (QUAD, QUAD), 0) // GDN_DIM
               == lax.broadcasted_iota(jnp.int32, (QUAD, QUAD), 1) // GDN_DIM)
    tri = tri_ref[...]

    def chunk_quads(ref, c0, n):
        a = ref[pl.ds(pl.multiple_of(c0 * GDN_CHUNK, GDN_CHUNK), n * GDN_CHUNK), :]
        a = a.reshape(n, GDN_CHUNK, GDN_WIDTH)
        return jnp.stack([a[:, :, :QUAD], a[:, :, QUAD:]], axis=1).reshape(2 * n, GDN_CHUNK, QUAD)

    n_per = GDN_GROUP // 2

    def local_pass(it, carry):
        c0 = it * n_per
        q, k, v = (chunk_quads(r, c0, n_per) for r in (qn_ref, kn_ref, vn_ref))
        beta = chunk_quads(beta_ref, c0, n_per)
        gh, gl = chunk_quads(gh_ref, c0, n_per), chunk_quads(gl_ref, c0, n_per)

        zero = jnp.zeros_like(gh)
        rhs_hi = jnp.concatenate([gh, jnp.where(strict, gh, zero)], axis=-1)
        rhs_lo = jnp.concatenate([gl, jnp.where(strict, gl, zero)], axis=-1)
        tri_b = jnp.broadcast_to(tri, (GDN_GROUP, GDN_CHUNK, GDN_CHUNK))
        gd = _bmm(tri_b, rhs_hi) + _bmm(tri_b, rhs_lo)
        g_cum, d_pair = gd[:, :, :QUAD], gd[:, :, QUAD:]
        decay_incl = jnp.where(incl, jnp.exp(d_pair), 0.0)
        decay_strict = jnp.where(strict, decay_incl, 0.0)
        exp_g = jnp.exp(g_cum)
        g_last = g_cum[:, GDN_CHUNK - 1:GDN_CHUNK, :]
        k_dec = k * jnp.exp(g_last - g_cum)
        k_beta = k * beta

        k_bd = _block_diag(k.astype(BF16), bd_mask)
        aa = _bmm_nt(jnp.concatenate([k_beta, q], axis=1).astype(BF16), k_bd)
        a = aa[:, :GDN_CHUNK] * decay_strict
        attn = aa[:, GDN_CHUNK:] * decay_incl

        p = a
        t = eye - a
        p = _bmm(p.astype(BF16), _block_diag(p.astype(BF16), bd_mask))
        for _ in range(4):
            r = _bmm(jnp.concatenate([p, t], axis=1).astype(BF16), _block_diag(p.astype(BF16), bd_mask))
            p = r[:, :GDN_CHUNK]
            t = t + r[:, GDN_CHUNK:]
        t = t + _bmm(t.astype(BF16), _block_diag(p.astype(BF16), bd_mask))
        t16 = t.astype(BF16)
        u = _bmm(t16, _block_diag((v * beta).astype(BF16), bd_mask))
        w = _bmm(t16, _block_diag((k_beta * exp_g).astype(BF16), bd_mask))

        sl = pl.ds(pl.multiple_of(it * GDN_GROUP, GDN_GROUP), GDN_GROUP)
        wq_ref[sl] = jnp.concatenate([w, q * exp_g], axis=1).astype(BF16)
        u_ref[sl] = u
        attn_ref[sl] = attn.astype(BF16)
        kdt_ref[sl] = jnp.swapaxes(k_dec, 1, 2).astype(BF16)
        dec_ref[sl] = jnp.broadcast_to(jnp.exp(g_last), (GDN_GROUP, 8, QUAD))
        return carry

    lax.fori_loop(0, nb // GDN_GROUP, local_pass, 0)

    e256 = e512_ref[0:QUAD, 0:QUAD]
    nw = nw_ref[:, 0:QUAD]

    def scan_step(c, carry):
        sl = pl.ds(pl.multiple_of(2 * c, 2), 2)
        state = state_ref[...]
        ws = _bmm(wq_ref[sl], state.astype(BF16))
        v_new = u_ref[sl] - ws[:, :GDN_CHUNK]
        v16 = v_new.astype(BF16)
        o = ws[:, GDN_CHUNK:] + _bmm(attn_ref[sl], _block_diag(v16, bd_mask))
        upd = _bmm(kdt_ref[sl], v16)
        state_ref[...] = state * dec_ref[sl][:, 0:1, :] + jnp.where(bd_mask, upd, 0.0)

        o2 = o.reshape(2 * GDN_CHUNK, QUAD)
        ms = jnp.dot((o2 * o2).astype(BF16), e256, preferred_element_type=F32) * (1.0 / GDN_DIM)
        o2 = o2 * lax.rsqrt(ms + NORM_EPS) * nw
        rows = pl.ds(pl.multiple_of(c * GDN_CHUNK, GDN_CHUNK), GDN_CHUNK)
        o_ref[rows, 0:QUAD] = o2[:GDN_CHUNK].astype(o_ref.dtype)
        o_ref[rows, QUAD:] = o2[GDN_CHUNK:].astype(o_ref.dtype)
        return carry

    lax.fori_loop(0, nc, scan_step, 0)


def _gdn(main2, logits, conv_w, alog_row, dtb_row, norm_row, batch, seq):
    rows = main2.shape[0]
    tg = min(512, seq)
    steps = seq // tg
    nb = 2 * (tg // GDN_CHUNK)
    head_of_lane = np.arange(GDN_WIDTH) // GDN_DIM
    e512 = jnp.asarray(head_of_lane[:, None] == head_of_lane[None, :], BF16)
    e_beta = jnp.asarray(np.arange(LOGIT_PAD)[:, None] == head_of_lane[None, :], BF16)
    e_g = jnp.asarray(np.arange(LOGIT_PAD)[:, None] == head_of_lane[None, :] + GDN_HEADS, BF16)
    tri = jnp.asarray(np.tril(np.ones((GDN_CHUNK, GDN_CHUNK))), BF16)

    cur = lambda c: pl.BlockSpec((tg, 512), lambda b, n: (b * steps + n, c))
    halo = lambda c: pl.BlockSpec(
        (16, 512), lambda b, n: (jnp.maximum((b * steps + n) * (tg // 16) - 1, 0), c))
    whole = lambda a: pl.BlockSpec(a.shape, lambda b, n: (0,) * a.ndim)
    consts = (conv_w, alog_row, dtb_row, norm_row, e512, e_beta, e_g, tri)
    return pl.pallas_call(
        functools.partial(_gdn_kernel, tg=tg),
        grid=(batch, steps),
        in_specs=[cur(REST_GDN_Q), cur(REST_GDN_Q + 1), cur(REST_GDN_Q + 2),
                  halo(REST_GDN_Q), halo(REST_GDN_Q + 1), halo(REST_GDN_Q + 2),
                  pl.BlockSpec((tg, LOGIT_PAD), lambda b, n: (b * steps + n, 0))] + [whole(a) for a in consts],
        out_specs=pl.BlockSpec((tg, GDN_WIDTH), lambda b, n: (b * steps + n, 0)),
        out_shape=jax.ShapeDtypeStruct((rows, GDN_WIDTH), BF16),
        scratch_shapes=[
            pltpu.VMEM((tg + 8, GDN_WIDTH), F32),
            pltpu.VMEM((tg, GDN_WIDTH), F32),
            pltpu.VMEM((tg, GDN_WIDTH), F32),
            pltpu.VMEM((tg, GDN_WIDTH), F32),
            pltpu.VMEM((tg, GDN_WIDTH), F32),
            pltpu.VMEM((tg, GDN_WIDTH), BF16),
            pltpu.VMEM((tg, GDN_WIDTH), BF16),
            pltpu.VMEM((nb, 2 * GDN_CHUNK, QUAD), BF16),
            pltpu.VMEM((nb, GDN_CHUNK, QUAD), F32),
            pltpu.VMEM((nb, GDN_CHUNK, QUAD), BF16),
            pltpu.VMEM((nb, QUAD, GDN_CHUNK), BF16),
            pltpu.VMEM((nb, 8, QUAD), F32),
            pltpu.VMEM((2, QUAD, QUAD), F32),
        ],
        compiler_params=pltpu.CompilerParams(
            dimension_semantics=("parallel", "arbitrary"), vmem_limit_bytes=VMEM_LIMIT_BYTES),
        name="gdn",
    )(main2, main2, main2, main2, main2, main2, logits, *consts)


def _rope_tables(seq):
    inv_freq = ROPE_THETA ** (-jnp.arange(0, DSA_HEAD_DIM, 2, dtype=F32) / DSA_HEAD_DIM)
    ang = jnp.arange(seq, dtype=F32)[:, None] * inv_freq[None, :]
    ang = jnp.concatenate([ang, ang, ang, ang], axis=-1)
    sign = jnp.where((jnp.arange(LANES) % DSA_HEAD_DIM) < DSA_HEAD_DIM // 2, -1.0, 1.0).astype(F32)
    return jnp.cos(ang), jnp.sin(ang) * sign[None, :]


def kernel(x, norm_w, w_in, conv_w, a_log, dt_bias, gdn_norm_w, w_up_a, w_up_b, w_out, final_norm_w):
    batch, seq, _ = x.shape
    assert norm_w.shape[0] == 1, "the final RMSNorm is fused into the (single) layer's output kernel"
    cos_t, sin_t = _rope_tables(seq)
    x2 = x.reshape(batch * seq, D_MODEL)
    w = w_in[0]
    gates = LOGIT_START + 2 * GDN_HEADS
    w_qkv = w[:, :QKV_COLS].astype(BF16)
    w_rest = jnp.concatenate([w[:, gates:], w[:, QKV_COLS:LOGIT_START]], axis=1).astype(BF16)
    w_logit = jnp.pad(w[:, LOGIT_START:gates], ((0, 0), (0, LOGIT_PAD - 2 * GDN_HEADS))).astype(BF16)
    qkv, rest, logits = _in_proj(x2, norm_w[0][None, :], w_qkv, w_rest, w_logit, cos_t, sin_t, seq)

    o_groups, lse_groups = [], []
    for g, (_, dilation) in enumerate(DSA_PATTERNS):
        o_g, lse_g = _dsa_attention(qkv, g, dilation, batch, seq)
        o_groups.append(o_g)
        lse_groups.append(lse_g)

    pad8 = lambda p: jnp.pad(p.astype(F32), (GDN_HEADS, LOGIT_PAD - 2 * GDN_HEADS))[None, :]
    o_b = _gdn(rest, logits, conv_w[0], pad8(a_log[0]), pad8(dt_bias[0]),
               jnp.tile(gdn_norm_w[0].astype(F32), GDN_HEADS)[None, :], batch, seq)

    out = _out_proj(o_groups, lse_groups, rest, o_b, x2,
                    w_up_a[0].astype(BF16), w_up_b[0].astype(BF16), w_out[0].astype(BF16),
                    final_norm_w[None, :])
    return out.reshape(batch, seq, D_MODEL)
```

```python
import functools

import numpy as np
import jax
import jax.numpy as jnp
from jax import lax
from jax.experimental import pallas as pl
from jax.experimental.pallas import tpu as pltpu

F32 = jnp.float32
BF16 = jnp.bfloat16

D_MODEL = 1024
DSA_PATTERNS = ((128, 1), (512, 4), (2048, 16))
DSA_HEADS = 8
DSA_HEAD_DIM = 64
DSA_WIDTH = DSA_HEADS * DSA_HEAD_DIM
DSA_BLOCK = 128
ROPE_THETA = 10000.0
GDN_HEADS = 8
GDN_DIM = 64
GDN_WIDTH = GDN_HEADS * GDN_DIM
GDN_CONV = 4
GDN_CHUNK = 64
NORM_EPS = 1e-6

QKV_COLS = 3 * 3 * DSA_WIDTH
GROUP_COLS = 3 * DSA_WIDTH
GATE_COLS = 3072
GATE_A, GATE_B = 0, 1
GATE_DSA_Z, GATE_GDN_Z = 4, 5
LOGIT_START = 7168
LOGIT_PAD = 128
PROJ_ROWS = 512

VMEM_LIMIT_BYTES = 56 * 1024 * 1024
LANES = 128
NEG_BIG = -1e30


def _lane_iota(shape):
    return lax.broadcasted_iota(jnp.int32, shape, len(shape) - 1)


def _in_proj_qkv_kernel(x_ref, nw_ref, w_ref, cos_ref, sin_ref, qkv_ref, h_ref):
    x = x_ref[...]
    ms = jnp.mean(x * x, axis=-1, keepdims=True)
    h = (x * lax.rsqrt(ms + NORM_EPS) * nw_ref[...]).astype(BF16)
    h_ref[...] = h
    cos = cos_ref[...]
    sin = sin_ref[...]
    low_half = (_lane_iota(cos.shape) % DSA_HEAD_DIM) < (DSA_HEAD_DIM // 2)
    for g in range(len(DSA_PATTERNS)):
        c0 = g * GROUP_COLS
        acc = jnp.dot(h, w_ref[:, c0:c0 + GROUP_COLS], preferred_element_type=F32)
        for c in range(8):
            t = acc[:, c * LANES:(c + 1) * LANES]
            rot = jnp.where(low_half, pltpu.roll(t, LANES - 32, 1), pltpu.roll(t, 32, 1))
            r = t * cos + rot * sin
            if c < 4:
                r = r * (DSA_HEAD_DIM ** -0.5)
            qkv_ref[:, c0 + c * LANES:c0 + (c + 1) * LANES] = r.astype(BF16)
        qkv_ref[:, c0 + 8 * LANES:c0 + GROUP_COLS] = acc[:, 8 * LANES:].astype(BF16)


def _in_proj_rest_kernel(h_ref, w_ref, wl_ref, cw_ref, e512_ref, gdn_ref, gates_ref, logit_ref, xs_ref, *,
                         tiles_per_seq):
    tm = h_ref.shape[0]
    first = pl.program_id(0) % tiles_per_seq == 0

    @pl.when(first)
    def _():
        xs_ref[0:8, :] = jnp.zeros((8, GROUP_COLS), F32)

    @pl.when(jnp.logical_not(first))
    def _():
        xs_ref[0:8, :] = xs_ref[tm:tm + 8, :]

    h = h_ref[...]
    logit_ref[...] = jnp.dot(h, wl_ref[...], preferred_element_type=F32)
    xs_ref[8:8 + tm, :] = jnp.dot(h, w_ref[:, 0:GROUP_COLS], preferred_element_type=F32)
    for g in range(GATE_COLS // GROUP_COLS):
        c0 = g * GROUP_COLS
        gates_ref[:, c0:c0 + GROUP_COLS] = jnp.dot(
            h, w_ref[:, GROUP_COLS + c0:2 * GROUP_COLS + c0], preferred_element_type=F32).astype(BF16)
    for part in range(3):
        cols = slice(part * GDN_WIDTH, (part + 1) * GDN_WIDTH)
        w = cw_ref[:, cols]
        y = w[0:1, :] * xs_ref[5:5 + tm, cols]
        for j in range(1, GDN_CONV):
            y = y + w[j:j + 1, :] * xs_ref[5 + j:5 + j + tm, cols]
        y = _silu(y)
        if part < 2:
            ss = jnp.dot((y * y).astype(BF16), e512_ref[...], preferred_element_type=F32)
            y = y * (lax.rsqrt(ss + NORM_EPS) * (GDN_DIM ** -0.5 if part == 0 else 1.0))
        gdn_ref[:, cols] = y.astype(BF16)


def _resident(shape):
    return pl.BlockSpec(shape, lambda i: (0,) * len(shape), pipeline_mode=pl.Buffered(1))


def _in_proj(x2, norm_w, w_qkv, w_rest, w_logit, conv_w, cos_t, sin_t, seq):
    rows = x2.shape[0]
    tm = min(PROJ_ROWS, seq)
    n_seq_tiles = seq // tm
    row = lambda width: pl.BlockSpec((tm, width), lambda i: (i, 0))
    table = pl.BlockSpec((tm, LANES), lambda i: (i % n_seq_tiles, 0))
    params = pltpu.CompilerParams(dimension_semantics=("parallel",), vmem_limit_bytes=VMEM_LIMIT_BYTES)
    head_of_lane = np.arange(GDN_WIDTH) // GDN_DIM
    e512 = jnp.asarray(head_of_lane[:, None] == head_of_lane[None, :], BF16)
    qkv, h = pl.pallas_call(
        _in_proj_qkv_kernel,
        grid=(rows // tm,),
        in_specs=[row(D_MODEL), _resident((1, D_MODEL)), _resident((D_MODEL, QKV_COLS)), table, table],
        out_specs=[row(QKV_COLS), row(D_MODEL)],
        out_shape=[jax.ShapeDtypeStruct((rows, QKV_COLS), BF16), jax.ShapeDtypeStruct((rows, D_MODEL), BF16)],
        compiler_params=params,
        name="in_proj_qkv",
    )(x2, norm_w, w_qkv, cos_t, sin_t)
    gdn_qkv, gates, logits = pl.pallas_call(
        functools.partial(_in_proj_rest_kernel, tiles_per_seq=n_seq_tiles),
        grid=(rows // tm,),
        in_specs=[row(D_MODEL), _resident((D_MODEL, GROUP_COLS + GATE_COLS)), _resident((D_MODEL, LOGIT_PAD)),
                  _resident(conv_w.shape), _resident(e512.shape)],
        out_specs=[row(GROUP_COLS), row(GATE_COLS), row(LOGIT_PAD)],
        out_shape=[jax.ShapeDtypeStruct((rows, GROUP_COLS), BF16), jax.ShapeDtypeStruct((rows, GATE_COLS), BF16),
                   jax.ShapeDtypeStruct((rows, LOGIT_PAD), F32)],
        scratch_shapes=[pltpu.VMEM((tm + 8, GROUP_COLS), F32)],
        compiler_params=pltpu.CompilerParams(
            dimension_semantics=("arbitrary",), vmem_limit_bytes=VMEM_LIMIT_BYTES),
        name="in_proj_rest",
    )(h, w_rest, w_logit, conv_w, e512)
    return qkv, gdn_qkv, gates, logits


DSA_SPAN = 2048
LSE_REP = LANES // DSA_HEADS


def _dsa_kernel(q_ref, k_ref, v_ref, o_ref, lse_ref, stage_ref, qp_ref, kp_ref, vp_ref, lstage_ref, tmp_ref=None,
                *, d, span):
    first = pl.program_id(1) == 0
    npr = span // d
    n_sub = npr // DSA_BLOCK
    kstride = npr + DSA_BLOCK

    def regroup(src_ref, dst_ref, dst_stride, dst_off):
        if d == 1:
            dst_ref[dst_off:dst_off + npr, :] = src_ref[...]
            return
        for c in range(4):
            stage_ref[c] = src_ref[:, c * LANES:(c + 1) * LANES].astype(F32)
        if d > 4:
            quarter = span // 4
            for c in range(4):
                for r4 in range(4):
                    tmp_ref[c, r4 * quarter:(r4 + 1) * quarter, :] = stage_ref[c, pl.ds(r4, quarter, stride=4), :]
            slabs, stride, start = tmp_ref, d // 4, lambda r: (r % 4) * quarter + r // 4
        else:
            slabs, stride, start = stage_ref, d, lambda r: r
        for c in range(4):
            for r in range(d):
                lo = r * dst_stride + dst_off
                dst_ref[lo:lo + npr, c * LANES:(c + 1) * LANES] = (
                    slabs[c, pl.ds(start(r), npr, stride=stride), :].astype(BF16))

    @pl.when(first)
    def _():
        for r in range(d):
            kp_ref[r * kstride:r * kstride + DSA_BLOCK, :] = jnp.zeros((DSA_BLOCK, DSA_WIDTH), BF16)
            vp_ref[r * kstride:r * kstride + DSA_BLOCK, :] = jnp.zeros((DSA_BLOCK, DSA_WIDTH), BF16)

    regroup(q_ref, qp_ref, npr, 0)
    regroup(k_ref, kp_ref, kstride, DSA_BLOCK)
    regroup(v_ref, vp_ref, kstride, DSA_BLOCK)

    qi = lax.broadcasted_iota(jnp.int32, (DSA_BLOCK, 2 * DSA_BLOCK), 0)
    kj = lax.broadcasted_iota(jnp.int32, (DSA_BLOCK, 2 * DSA_BLOCK), 1)
    band = (kj >= qi) & (kj <= qi + DSA_BLOCK)
    head_a = _lane_iota((DSA_BLOCK, LANES)) < DSA_HEAD_DIM
    lse_head = _lane_iota((DSA_BLOCK, LANES)) // LSE_REP

    def block(blk, carry):
        r, i = blk // n_sub, blk % n_sub
        q0 = pl.multiple_of(r * npr + i * DSA_BLOCK, DSA_BLOCK)
        k0 = pl.multiple_of(r * kstride + i * DSA_BLOCK, DSA_BLOCK)
        nat0 = r + d * DSA_BLOCK * i
        mask = band & (kj >= jnp.where(first & (i == 0), DSA_BLOCK, 0))
        lse_tile = jnp.zeros((DSA_BLOCK, LANES), F32)
        for hp in range(DSA_HEADS // 2):
            cols = slice(hp * LANES, (hp + 1) * LANES)
            q2 = qp_ref[pl.ds(q0, DSA_BLOCK), cols]
            k2 = kp_ref[pl.ds(k0, 2 * DSA_BLOCK), cols]
            v2 = vp_ref[pl.ds(k0, 2 * DSA_BLOCK), cols]
            outs, lses = [], []
            for is_a in (True, False):
                sel = head_a if is_a else jnp.logical_not(head_a)
                qh = jnp.where(sel, q2, jnp.zeros_like(q2))
                s = lax.dot_general(qh, k2, (((1,), (1,)), ((), ())), preferred_element_type=F32)
                s = jnp.where(mask, s, NEG_BIG)
                m = jnp.max(s, axis=-1, keepdims=True)
                p = jnp.exp(s - m)
                den = jnp.sum(p, axis=-1, keepdims=True)
                pv = jnp.dot(p.astype(BF16), v2, preferred_element_type=F32)
                outs.append(pv / den)
                lses.append(m + jnp.log(den))
            o_pair = jnp.where(head_a, outs[0], outs[1])
            lse_tile = jnp.where(lse_head == 2 * hp, lses[0], jnp.where(lse_head == 2 * hp + 1, lses[1], lse_tile))
            if d == 1:
                o_ref[pl.ds(q0, DSA_BLOCK), cols] = o_pair.astype(o_ref.dtype)
            else:
                stage_ref[hp, pl.ds(nat0, DSA_BLOCK, stride=d), :] = o_pair
        if d == 1:
            lse_ref[pl.ds(q0, DSA_BLOCK), :] = lse_tile
        else:
            lstage_ref[pl.ds(nat0, DSA_BLOCK, stride=d), :] = lse_tile
        return carry

    lax.fori_loop(0, d * n_sub, block, 0, unroll=4)

    for r in range(d):
        lo = r * kstride
        kp_ref[lo:lo + DSA_BLOCK, :] = kp_ref[lo + npr:lo + npr + DSA_BLOCK, :]
        vp_ref[lo:lo + DSA_BLOCK, :] = vp_ref[lo + npr:lo + npr + DSA_BLOCK, :]
    if d > 1:
        for c in range(4):
            o_ref[:, c * LANES:(c + 1) * LANES] = stage_ref[c].astype(o_ref.dtype)
        lse_ref[...] = lstage_ref[...]


def _dsa_attention(main2, group, dilation, batch, seq):
    span = min(DSA_SPAN, seq)
    steps = seq // span
    kv_rows = span + dilation * DSA_BLOCK
    col = lambda t: pl.BlockSpec((span, 512), lambda b, n: (b * steps + n, group * 3 + t))
    row = lambda width: pl.BlockSpec((span, width), lambda b, n: (b * steps + n, 0))
    return pl.pallas_call(
        functools.partial(_dsa_kernel, d=dilation, span=span),
        grid=(batch, steps),
        in_specs=[col(0), col(1), col(2)],
        out_specs=[row(DSA_WIDTH), row(LANES)],
        out_shape=[
            jax.ShapeDtypeStruct((batch * seq, DSA_WIDTH), BF16),
            jax.ShapeDtypeStruct((batch * seq, LANES), F32),
        ],
        scratch_shapes=[
            pltpu.VMEM((4, span, LANES), F32),
            pltpu.VMEM((span, DSA_WIDTH), BF16),
            pltpu.VMEM((kv_rows, DSA_WIDTH), BF16),
            pltpu.VMEM((kv_rows, DSA_WIDTH), BF16),
            pltpu.VMEM((span, LANES), F32),
        ] + ([pltpu.VMEM((4, span, LANES), F32)] if dilation > 4 else []),
        compiler_params=pltpu.CompilerParams(
            dimension_semantics=("parallel", "arbitrary"), vmem_limit_bytes=VMEM_LIMIT_BYTES),
        name=f"dsa_attn_d{dilation}",
    )(main2, main2, main2)


def _silu(x):
    return x * (1.0 / (1.0 + jnp.exp(-x)))


def _sigmoid(x):
    return 1.0 / (1.0 + jnp.exp(-x))


def _out_proj_kernel(o0_ref, o1_ref, o2_ref, l0_ref, l1_ref, l2_ref, za_ref, ob_ref, zb_ref, ga_ref, gb_ref,
                     x_ref, wa_ref, wb_ref, wo_ref, fw_ref, ex_ref, out_ref):
    l0, l1, l2 = l0_ref[...], l1_ref[...], l2_ref[...]
    mx = jnp.maximum(jnp.maximum(l0, l1), l2)
    e0, e1, e2 = jnp.exp(l0 - mx), jnp.exp(l1 - mx), jnp.exp(l2 - mx)
    inv = 1.0 / (e0 + e1 + e2)

    def per_lane(w):
        hi, lo = _split_hi_lo(w)
        return (jnp.dot(hi, ex_ref[...], preferred_element_type=F32)
                + jnp.dot(lo, ex_ref[...], preferred_element_type=F32))

    o_a = (per_lane(e0 * inv) * o0_ref[...].astype(F32) + per_lane(e1 * inv) * o1_ref[...].astype(F32)
           + per_lane(e2 * inv) * o2_ref[...].astype(F32))
    a_in = (o_a * _silu(za_ref[...].astype(F32))).astype(BF16)
    b_in = (ob_ref[...].astype(F32) * _silu(zb_ref[...].astype(F32))).astype(BF16)
    y_a = jnp.dot(a_in, wa_ref[...], preferred_element_type=F32)
    y_b = jnp.dot(b_in, wb_ref[...], preferred_element_type=F32)
    merged = _sigmoid(ga_ref[...].astype(F32)) * y_a + _sigmoid(gb_ref[...].astype(F32)) * y_b
    y = x_ref[...] + jnp.dot(merged.astype(BF16), wo_ref[...], preferred_element_type=F32)
    ms = jnp.mean(y * y, axis=-1, keepdims=True)
    out_ref[...] = y * lax.rsqrt(ms + NORM_EPS) * fw_ref[...]


def _out_proj(o_groups, lse_groups, rest, o_b, x2, w_up_a, w_up_b, w_out, final_w):
    rows = x2.shape[0]
    tm = min(512, rows)
    row128 = pl.BlockSpec((tm, LANES), lambda i: (i, 0))
    row512 = lambda c: pl.BlockSpec((tm, 512), lambda i: (i, c))
    row1024 = lambda c: pl.BlockSpec((tm, 1024), lambda i: (i, c))
    whole = lambda a: pl.BlockSpec(a.shape, lambda i: (0,) * a.ndim)
    expand = jnp.asarray(np.arange(LANES)[:, None] == LSE_REP * (np.arange(DSA_WIDTH)[None, :] // DSA_HEAD_DIM), BF16)
    return pl.pallas_call(
        _out_proj_kernel,
        grid=(rows // tm,),
        in_specs=[row512(0)] * 3 + [row128] * 3 + [
            row512(GATE_DSA_Z), row512(0), row512(GATE_GDN_Z), row1024(GATE_A), row1024(GATE_B), row1024(0),
            whole(w_up_a), whole(w_up_b), whole(w_out), whole(final_w), whole(expand)],
        out_specs=pl.BlockSpec((tm, D_MODEL), lambda i: (i, 0)),
        out_shape=jax.ShapeDtypeStruct((rows, D_MODEL), F32),
        compiler_params=pltpu.CompilerParams(
            dimension_semantics=("parallel",), vmem_limit_bytes=VMEM_LIMIT_BYTES),
        name="out_proj",
    )(*o_groups, *lse_groups, rest, o_b, rest, rest, rest, x2, w_up_a, w_up_b, w_out, final_w, expand)


QUAD = 4 * GDN_DIM
GDN_GROUP = 16


def _split_hi_lo(x):
    hi = x.astype(BF16)
    lo = (x - hi.astype(F32)).astype(BF16)
    return hi, lo


def _softplus(x):
    return jnp.maximum(x, 0.0) + jnp.log1p(jnp.exp(-jnp.abs(x)))


def _bmm(a, b):
    return lax.dot_general(a, b, (((2,), (1,)), ((0,), (0,))), preferred_element_type=F32)


def _bmm_nt(a, b):
    return lax.dot_general(a, b, (((2,), (2,)), ((0,), (0,))), preferred_element_type=F32)


def _block_diag(x, bd_mask):
    t = jnp.concatenate([x, x, x, x], axis=1)
    return jnp.where(bd_mask, t, jnp.zeros_like(t))


def _gdn_kernel(q_ref, k_ref, v_ref, lg_ref, alog_ref, dtb_ref, nw_ref, e512_ref, eb_ref, eg_ref, tri_ref, o_ref,
                beta_ref, gh_ref, gl_ref, lhs_ref, oloc_ref, snew_ref, dec_ref, oraw_ref, state_ref, *, tg):
    first = pl.program_id(1) == 0
    nc = tg // GDN_CHUNK
    nb = 2 * nc

    @pl.when(first)
    def _():
        state_ref[...] = jnp.zeros_like(state_ref)

    logits = lg_ref[...]
    beta_hi, beta_lo = _split_hi_lo(_sigmoid(logits))
    g_hi, g_lo = _split_hi_lo(-jnp.exp(alog_ref[...]) * _softplus(logits + dtb_ref[...]))
    beta_ref[...] = (jnp.dot(beta_hi, eb_ref[...], preferred_element_type=F32)
                     + jnp.dot(beta_lo, eb_ref[...], preferred_element_type=F32))
    gh_ref[...] = jnp.dot(g_hi, eg_ref[...], preferred_element_type=F32).astype(BF16)
    gl_ref[...] = jnp.dot(g_lo, eg_ref[...], preferred_element_type=F32).astype(BF16)

    row = lax.broadcasted_iota(jnp.int32, (GDN_CHUNK, QUAD), 0)
    col = lax.broadcasted_iota(jnp.int32, (GDN_CHUNK, QUAD), 1) % GDN_CHUNK
    incl = row >= col
    strict = row > col
    eye = (row == col).astype(F32)
    bd_mask = (lax.broadcasted_iota(jnp.int32, (QUAD, QUAD), 0) // GDN_DIM
               == lax.broadcasted_iota(jnp.int32, (QUAD, QUAD), 1) // GDN_DIM)
    tri = tri_ref[...]

    def chunk_quads(ref, c0, n, dtype):
        a = ref[pl.ds(pl.multiple_of(c0 * GDN_CHUNK, GDN_CHUNK), n * GDN_CHUNK), :]
        a = a.reshape(n, GDN_CHUNK, GDN_WIDTH).astype(dtype)
        return jnp.stack([a[:, :, :QUAD], a[:, :, QUAD:]], axis=1).reshape(2 * n, GDN_CHUNK, QUAD)

    n_per = GDN_GROUP // 2

    def local_pass(it, carry):
        c0 = it * n_per
        q, k, v = (chunk_quads(r, c0, n_per, F32) for r in (q_ref, k_ref, v_ref))
        beta = chunk_quads(beta_ref, c0, n_per, F32)
        gh, gl = chunk_quads(gh_ref, c0, n_per, BF16), chunk_quads(gl_ref, c0, n_per, BF16)

        zero = jnp.zeros_like(gh)
        rhs_hi = jnp.concatenate([gh, jnp.where(strict, gh, zero)], axis=-1)
        rhs_lo = jnp.concatenate([gl, jnp.where(strict, gl, zero)], axis=-1)
        tri_b = jnp.broadcast_to(tri, (GDN_GROUP, GDN_CHUNK, GDN_CHUNK))
        gd = _bmm(tri_b, rhs_hi) + _bmm(tri_b, rhs_lo)
        g_cum, d_pair = gd[:, :, :QUAD], gd[:, :, QUAD:]
        decay_incl = jnp.where(incl, jnp.exp(d_pair), 0.0)
        decay_strict = jnp.where(strict, decay_incl, 0.0)
        exp_g = jnp.exp(g_cum)
        g_last = g_cum[:, GDN_CHUNK - 1:GDN_CHUNK, :]
        k_dec = k * jnp.exp(g_last - g_cum)
        k_beta = k * beta

        k_bd = _block_diag(k.astype(BF16), bd_mask)
        aa = _bmm_nt(jnp.concatenate([k_beta, q], axis=1).astype(BF16), k_bd)
        a = aa[:, :GDN_CHUNK] * decay_strict
        attn = aa[:, GDN_CHUNK:] * decay_incl

        p = a
        t = eye - a
        p = _bmm(p.astype(BF16), _block_diag(p.astype(BF16), bd_mask))
        for _ in range(4):
            r = _bmm(jnp.concatenate([p, t], axis=1).astype(BF16), _block_diag(p.astype(BF16), bd_mask))
            p = r[:, :GDN_CHUNK]
            t = t + r[:, GDN_CHUNK:]
        t = t + _bmm(t.astype(BF16), _block_diag(p.astype(BF16), bd_mask))
        t16 = t.astype(BF16)
        u = _bmm(t16, _block_diag((v * beta).astype(BF16), bd_mask))
        w = _bmm(t16, _block_diag((k_beta * exp_g).astype(BF16), bd_mask))

        w16, u16 = w.astype(BF16), u.astype(BF16)
        kt = _bmm(jnp.swapaxes(k_dec, 1, 2).astype(BF16), jnp.concatenate([w16, u16], axis=-1))
        ao = _bmm(attn.astype(BF16),
                  jnp.concatenate([_block_diag(w16, bd_mask), _block_diag(u16, bd_mask)], axis=-1))
        sl = pl.ds(pl.multiple_of(it * GDN_GROUP, GDN_GROUP), GDN_GROUP)
        lhs_ref[sl] = jnp.concatenate(
            [q * exp_g - ao[:, :, :QUAD], jnp.where(bd_mask, -kt[:, :, :QUAD], 0.0)], axis=1).astype(BF16)
        oloc_ref[sl] = ao[:, :, QUAD:]
        snew_ref[sl] = jnp.where(bd_mask, kt[:, :, QUAD:], 0.0)
        dec_ref[sl] = jnp.broadcast_to(jnp.exp(g_last), (GDN_GROUP, 8, QUAD))
        return carry

    lax.fori_loop(0, nb // GDN_GROUP, local_pass, 0)

    def scan_step(c, carry):
        sl = pl.ds(pl.multiple_of(2 * c, 2), 2)
        state = state_ref[...]
        r = _bmm(lhs_ref[sl], state.astype(BF16))
        state_ref[...] = state * dec_ref[sl][:, 0:1, :] + r[:, GDN_CHUNK:] + snew_ref[sl]
        o = r[:, :GDN_CHUNK] + oloc_ref[sl]
        rows = pl.ds(pl.multiple_of(c * GDN_CHUNK, GDN_CHUNK), GDN_CHUNK)
        oraw_ref[rows, 0:QUAD] = o[0]
        oraw_ref[rows, QUAD:] = o[1]
        return carry

    lax.fori_loop(0, nc, scan_step, 0)

    o_all = oraw_ref[...]
    ms = jnp.dot((o_all * o_all).astype(BF16), e512_ref[...], preferred_element_type=F32) * (1.0 / GDN_DIM)
    o_ref[...] = (o_all * lax.rsqrt(ms + NORM_EPS) * nw_ref[...]).astype(o_ref.dtype)


def _gdn(gdn_qkv, logits, alog_row, dtb_row, norm_row, batch, seq):
    rows = gdn_qkv.shape[0]
    tg = min(512, seq)
    steps = seq // tg
    nb = 2 * (tg // GDN_CHUNK)
    head_of_lane = np.arange(GDN_WIDTH) // GDN_DIM
    e512 = jnp.asarray(head_of_lane[:, None] == head_of_lane[None, :], BF16)
    e_beta = jnp.asarray(np.arange(LOGIT_PAD)[:, None] == head_of_lane[None, :], BF16)
    e_g = jnp.asarray(np.arange(LOGIT_PAD)[:, None] == head_of_lane[None, :] + GDN_HEADS, BF16)
    tri = jnp.asarray(np.tril(np.ones((GDN_CHUNK, GDN_CHUNK))), BF16)

    cur = lambda c: pl.BlockSpec((tg, 512), lambda b, n: (b * steps + n, c))
    whole = lambda a: pl.BlockSpec(a.shape, lambda b, n: (0,) * a.ndim)
    consts = (alog_row, dtb_row, norm_row, e512, e_beta, e_g, tri)
    return pl.pallas_call(
        functools.partial(_gdn_kernel, tg=tg),
        grid=(batch, steps),
        in_specs=[cur(0), cur(1), cur(2),
                  pl.BlockSpec((tg, LOGIT_PAD), lambda b, n: (b * steps + n, 0))] + [whole(a) for a in consts],
        out_specs=pl.BlockSpec((tg, GDN_WIDTH), lambda b, n: (b * steps + n, 0)),
        out_shape=jax.ShapeDtypeStruct((rows, GDN_WIDTH), BF16),
        scratch_shapes=[
            pltpu.VMEM((tg, GDN_WIDTH), F32),
            pltpu.VMEM((tg, GDN_WIDTH), BF16),
            pltpu.VMEM((tg, GDN_WIDTH), BF16),
            pltpu.VMEM((nb, GDN_CHUNK + QUAD, QUAD), BF16),
            pltpu.VMEM((nb, GDN_CHUNK, QUAD), F32),
            pltpu.VMEM((nb, QUAD, QUAD), F32),
            pltpu.VMEM((nb, 8, QUAD), F32),
            pltpu.VMEM((tg, GDN_WIDTH), F32),
            pltpu.VMEM((2, QUAD, QUAD), F32),
        ],
        compiler_params=pltpu.CompilerParams(
            dimension_semantics=("parallel", "arbitrary"), vmem_limit_bytes=VMEM_LIMIT_BYTES),
        name="gdn",
    )(gdn_qkv, gdn_qkv, gdn_qkv, logits, *consts)


def _rope_tables(seq):
    inv_freq = ROPE_THETA ** (-jnp.arange(0, DSA_HEAD_DIM, 2, dtype=F32) / DSA_HEAD_DIM)
    ang = jnp.arange(seq, dtype=F32)[:, None] * inv_freq[None, :]
    ang = jnp.concatenate([ang, ang, ang, ang], axis=-1)
    sign = jnp.where((jnp.arange(LANES) % DSA_HEAD_DIM) < DSA_HEAD_DIM // 2, -1.0, 1.0).astype(F32)
    return jnp.cos(ang), jnp.sin(ang) * sign[None, :]


def kernel(x, norm_w, w_in, conv_w, a_log, dt_bias, gdn_norm_w, w_up_a, w_up_b, w_out, final_norm_w):
    batch, seq, _ = x.shape
    assert norm_w.shape[0] == 1, "the final RMSNorm is fused into the (single) layer's output kernel"
    cos_t, sin_t = _rope_tables(seq)
    x2 = x.reshape(batch * seq, D_MODEL)
    w = w_in[0]
    gates = LOGIT_START + 2 * GDN_HEADS
    w_qkv = w[:, :QKV_COLS].astype(BF16)
    dsa_z, gdn_in, gdn_z = QKV_COLS, QKV_COLS + DSA_WIDTH, QKV_COLS + DSA_WIDTH + GROUP_COLS
    w_rest = jnp.concatenate(
        [w[:, gdn_in:gdn_z], w[:, gates:], w[:, dsa_z:gdn_in], w[:, gdn_z:LOGIT_START]], axis=1).astype(BF16)
    w_logit = jnp.pad(w[:, LOGIT_START:gates], ((0, 0), (0, LOGIT_PAD - 2 * GDN_HEADS))).astype(BF16)
    qkv, gdn_qkv, rest, logits = _in_proj(
        x2, norm_w[0][None, :], w_qkv, w_rest, w_logit, conv_w[0], cos_t, sin_t, seq)

    o_groups, lse_groups = [], []
    for g, (_, dilation) in enumerate(DSA_PATTERNS):
        o_g, lse_g = _dsa_attention(qkv, g, dilation, batch, seq)
        o_groups.append(o_g)
        lse_groups.append(lse_g)

    pad8 = lambda p: jnp.pad(p.astype(F32), (GDN_HEADS, LOGIT_PAD - 2 * GDN_HEADS))[None, :]
    o_b = _gdn(gdn_qkv, logits, pad8(a_log[0]), pad8(dt_bias[0]),
               jnp.tile(gdn_norm_w[0].astype(F32), GDN_HEADS)[None, :], batch, seq)

    out = _out_proj(o_groups, lse_groups, rest, o_b, x2,
                    w_up_a[0].astype(BF16), w_up_b[0].astype(BF16), w_out[0].astype(BF16),
                    final_norm_w[None, :])
    return out.reshape(batch, seq, D_MODEL)
```

```python
import functools

import numpy as np
import jax
import jax.numpy as jnp
from jax import lax
from jax.experimental import pallas as pl
from jax.experimental.pallas import tpu as pltpu

F32 = jnp.float32
BF16 = jnp.bfloat16

D_MODEL = 1024
DSA_PATTERNS = ((128, 1), (512, 4), (2048, 16))
DSA_HEADS = 8
DSA_HEAD_DIM = 64
DSA_WIDTH = DSA_HEADS * DSA_HEAD_DIM
DSA_BLOCK = 128
ROPE_THETA = 10000.0
GDN_HEADS = 8
GDN_DIM = 64
GDN_WIDTH = GDN_HEADS * GDN_DIM
GDN_CONV = 4
GDN_CHUNK = 64
NORM_EPS = 1e-6

QKV_COLS = 3 * 3 * DSA_WIDTH
GROUP_COLS = 3 * DSA_WIDTH
GATE_COLS = 3072
GATE_A, GATE_B = 0, 1
GATE_DSA_Z, GATE_GDN_Z = 4, 5
LOGIT_START = 7168
LOGIT_PAD = 128
PROJ_ROWS = 512
LOG2_E = 1.4426950408889634
LN_2 = 0.6931471805599453
DSA_Q_SCALE = DSA_HEAD_DIM ** -0.5 * LOG2_E

VMEM_LIMIT_BYTES = 56 * 1024 * 1024
LANES = 128
NEG_BIG = -1e30


def _lane_iota(shape):
    return lax.broadcasted_iota(jnp.int32, shape, len(shape) - 1)


def _in_proj_qkv_kernel(x_ref, nw_ref, w_ref, cos_ref, sin_ref, qkv_ref, h_ref):
    x = x_ref[...]
    ms = jnp.mean(x * x, axis=-1, keepdims=True)
    h = (x * lax.rsqrt(ms + NORM_EPS) * nw_ref[...]).astype(BF16)
    h_ref[...] = h
    cos = cos_ref[...]
    sin = sin_ref[...]
    low_half = (_lane_iota(cos.shape) % DSA_HEAD_DIM) < (DSA_HEAD_DIM // 2)
    for g in range(len(DSA_PATTERNS)):
        c0 = g * GROUP_COLS
        acc = jnp.dot(h, w_ref[:, c0:c0 + GROUP_COLS], preferred_element_type=F32)
        for c in range(8):
            t = acc[:, c * LANES:(c + 1) * LANES]
            rot = jnp.where(low_half, pltpu.roll(t, LANES - 32, 1), pltpu.roll(t, 32, 1))
            r = t * cos + rot * sin
            if c < 4:
                r = r * DSA_Q_SCALE
            qkv_ref[:, c0 + c * LANES:c0 + (c + 1) * LANES] = r.astype(BF16)
        qkv_ref[:, c0 + 8 * LANES:c0 + GROUP_COLS] = acc[:, 8 * LANES:].astype(BF16)


def _in_proj_rest_kernel(h_ref, w_ref, wl_ref, cw_ref, e512_ref, gdn_ref, gates_ref, logit_ref, xs_ref, *,
                         tiles_per_seq):
    tm = h_ref.shape[0]
    first = pl.program_id(0) % tiles_per_seq == 0

    @pl.when(first)
    def _():
        xs_ref[0:8, :] = jnp.zeros((8, GROUP_COLS), F32)

    @pl.when(jnp.logical_not(first))
    def _():
        xs_ref[0:8, :] = xs_ref[tm:tm + 8, :]

    h = h_ref[...]
    logit_ref[...] = jnp.dot(h, wl_ref[...], preferred_element_type=F32)
    xs_ref[8:8 + tm, :] = jnp.dot(h, w_ref[:, 0:GROUP_COLS], preferred_element_type=F32)
    for g in range(GATE_COLS // GROUP_COLS):
        c0 = g * GROUP_COLS
        gates_ref[:, c0:c0 + GROUP_COLS] = jnp.dot(
            h, w_ref[:, GROUP_COLS + c0:2 * GROUP_COLS + c0], preferred_element_type=F32).astype(BF16)
    for part in range(3):
        cols = slice(part * GDN_WIDTH, (part + 1) * GDN_WIDTH)
        w = cw_ref[:, cols]
        y = w[0:1, :] * xs_ref[5:5 + tm, cols]
        for j in range(1, GDN_CONV):
            y = y + w[j:j + 1, :] * xs_ref[5 + j:5 + j + tm, cols]
        y = _silu(y)
        if part < 2:
            ss = jnp.dot((y * y).astype(BF16), e512_ref[...], preferred_element_type=F32)
            y = y * (lax.rsqrt(ss + NORM_EPS) * (GDN_DIM ** -0.5 if part == 0 else 1.0))
        gdn_ref[:, cols] = y.astype(BF16)


def _resident(shape, col_block=0):
    return pl.BlockSpec(shape, lambda i: (0,) * (len(shape) - 1) + (col_block,), pipeline_mode=pl.Buffered(1))


def _in_proj(x2, norm_w, w_all, conv_w, cos_t, sin_t, seq):
    rest_cols = GROUP_COLS + GATE_COLS
    assert rest_cols == QKV_COLS and (QKV_COLS + rest_cols) % LOGIT_PAD == 0
    rows = x2.shape[0]
    tm = min(PROJ_ROWS, seq)
    n_seq_tiles = seq // tm
    row = lambda width: pl.BlockSpec((tm, width), lambda i: (i, 0))
    table = pl.BlockSpec((tm, LANES), lambda i: (i % n_seq_tiles, 0))
    params = pltpu.CompilerParams(dimension_semantics=("parallel",), vmem_limit_bytes=VMEM_LIMIT_BYTES)
    head_of_lane = np.arange(GDN_WIDTH) // GDN_DIM
    e512 = jnp.asarray(head_of_lane[:, None] == head_of_lane[None, :], BF16)
    qkv, h = pl.pallas_call(
        _in_proj_qkv_kernel,
        grid=(rows // tm,),
        in_specs=[row(D_MODEL), _resident((1, D_MODEL)), _resident((D_MODEL, QKV_COLS)), table, table],
        out_specs=[row(QKV_COLS), row(D_MODEL)],
        out_shape=[jax.ShapeDtypeStruct((rows, QKV_COLS), BF16), jax.ShapeDtypeStruct((rows, D_MODEL), BF16)],
        compiler_params=params,
        name="in_proj_qkv",
    )(x2, norm_w, w_all, cos_t, sin_t)
    gdn_qkv, gates, logits = pl.pallas_call(
        functools.partial(_in_proj_rest_kernel, tiles_per_seq=n_seq_tiles),
        grid=(rows // tm,),
        in_specs=[row(D_MODEL), _resident((D_MODEL, rest_cols), 1),
                  _resident((D_MODEL, LOGIT_PAD), (QKV_COLS + rest_cols) // LOGIT_PAD),
                  _resident(conv_w.shape), _resident(e512.shape)],
        out_specs=[row(GROUP_COLS), row(GATE_COLS), row(LOGIT_PAD)],
        out_shape=[jax.ShapeDtypeStruct((rows, GROUP_COLS), BF16), jax.ShapeDtypeStruct((rows, GATE_COLS), BF16),
                   jax.ShapeDtypeStruct((rows, LOGIT_PAD), F32)],
        scratch_shapes=[pltpu.VMEM((tm + 8, GROUP_COLS), F32)],
        compiler_params=pltpu.CompilerParams(
            dimension_semantics=("arbitrary",), vmem_limit_bytes=VMEM_LIMIT_BYTES),
        name="in_proj_rest",
    )(h, w_all, w_all, conv_w, e512)
    return qkv, gdn_qkv, gates, logits


DSA_SPAN = 2048
LSE_REP = LANES // DSA_HEADS


def _dsa_kernel(q_ref, k_ref, v_ref, o_ref, lse_ref, stage_ref, qp_ref, kp_ref, vp_ref, lstage_ref, tmp_ref=None,
                *, d, span):
    first = pl.program_id(1) == 0
    npr = span // d
    n_sub = npr // DSA_BLOCK
    kstride = npr + DSA_BLOCK

    def regroup(src_ref, dst_ref, dst_stride, dst_off):
        if d == 1:
            dst_ref[dst_off:dst_off + npr, :] = src_ref[...]
            return
        for c in range(4):
            stage_ref[c] = src_ref[:, c * LANES:(c + 1) * LANES].astype(F32)
        if d > 4:
            quarter = span // 4
            for c in range(4):
                for r4 in range(4):
                    tmp_ref[c, r4 * quarter:(r4 + 1) * quarter, :] = stage_ref[c, pl.ds(r4, quarter, stride=4), :]
            slabs, stride, start = tmp_ref, d // 4, lambda r: (r % 4) * quarter + r // 4
        else:
            slabs, stride, start = stage_ref, d, lambda r: r
        for c in range(4):
            for r in range(d):
                lo = r * dst_stride + dst_off
                dst_ref[lo:lo + npr, c * LANES:(c + 1) * LANES] = (
                    slabs[c, pl.ds(start(r), npr, stride=stride), :].astype(BF16))

    @pl.when(first)
    def _():
        for r in range(d):
            kp_ref[r * kstride:r * kstride + DSA_BLOCK, :] = jnp.zeros((DSA_BLOCK, DSA_WIDTH), BF16)
            vp_ref[r * kstride:r * kstride + DSA_BLOCK, :] = jnp.zeros((DSA_BLOCK, DSA_WIDTH), BF16)

    regroup(q_ref, qp_ref, npr, 0)
    regroup(k_ref, kp_ref, kstride, DSA_BLOCK)
    regroup(v_ref, vp_ref, kstride, DSA_BLOCK)

    qi = lax.broadcasted_iota(jnp.int32, (DSA_BLOCK, 2 * DSA_BLOCK), 0)
    kj = lax.broadcasted_iota(jnp.int32, (DSA_BLOCK, 2 * DSA_BLOCK), 1)
    band = (kj >= qi) & (kj <= qi + DSA_BLOCK)
    head_a = _lane_iota((DSA_BLOCK, LANES)) < DSA_HEAD_DIM
    lse_head = _lane_iota((DSA_BLOCK, LANES)) // LSE_REP

    def block(blk, carry):
        r, i = blk // n_sub, blk % n_sub
        q0 = pl.multiple_of(r * npr + i * DSA_BLOCK, DSA_BLOCK)
        k0 = pl.multiple_of(r * kstride + i * DSA_BLOCK, DSA_BLOCK)
        nat0 = r + d * DSA_BLOCK * i
        mask = band & (kj >= jnp.where(first & (i == 0), DSA_BLOCK, 0))
        lse_tile = jnp.zeros((DSA_BLOCK, LANES), F32)
        for hp in range(DSA_HEADS // 2):
            cols = slice(hp * LANES, (hp + 1) * LANES)
            q2 = qp_ref[pl.ds(q0, DSA_BLOCK), cols]
            k2 = kp_ref[pl.ds(k0, 2 * DSA_BLOCK), cols]
            v2 = vp_ref[pl.ds(k0, 2 * DSA_BLOCK), cols]
            outs, lses = [], []
            for is_a in (True, False):
                sel = head_a if is_a else jnp.logical_not(head_a)
                qh = jnp.where(sel, q2, jnp.zeros_like(q2))
                s = lax.dot_general(qh, k2, (((1,), (1,)), ((), ())), preferred_element_type=F32)
                s = jnp.where(mask, s, NEG_BIG)
                m = jnp.max(s, axis=-1, keepdims=True)
                p = jnp.exp2(s - m)
                den = jnp.sum(p, axis=-1, keepdims=True)
                pv = jnp.dot(p.astype(BF16), v2, preferred_element_type=F32)
                outs.append(pv / den)
                lses.append(m * LN_2 + jnp.log(den))
            o_pair = jnp.where(head_a, outs[0], outs[1])
            lse_tile = jnp.where(lse_head == 2 * hp, lses[0], jnp.where(lse_head == 2 * hp + 1, lses[1], lse_tile))
            if d == 1:
                o_ref[pl.ds(q0, DSA_BLOCK), cols] = o_pair.astype(o_ref.dtype)
            else:
                stage_ref[hp, pl.ds(nat0, DSA_BLOCK, stride=d), :] = o_pair
        if d == 1:
            lse_ref[pl.ds(q0, DSA_BLOCK), :] = lse_tile
        else:
            lstage_ref[pl.ds(nat0, DSA_BLOCK, stride=d), :] = lse_tile
        return carry

    lax.fori_loop(0, d * n_sub, block, 0, unroll=4)

    for r in range(d):
        lo = r * kstride
        kp_ref[lo:lo + DSA_BLOCK, :] = kp_ref[lo + npr:lo + npr + DSA_BLOCK, :]
        vp_ref[lo:lo + DSA_BLOCK, :] = vp_ref[lo + npr:lo + npr + DSA_BLOCK, :]
    if d > 1:
        for c in range(4):
            o_ref[:, c * LANES:(c + 1) * LANES] = stage_ref[c].astype(o_ref.dtype)
        lse_ref[...] = lstage_ref[...]


def _dsa_attention(main2, group, dilation, batch, seq):
    span = min(DSA_SPAN, seq)
    steps = seq // span
    kv_rows = span + dilation * DSA_BLOCK
    col = lambda t: pl.BlockSpec((span, 512), lambda b, n: (b * steps + n, group * 3 + t))
    row = lambda width: pl.BlockSpec((span, width), lambda b, n: (b * steps + n, 0))
    return pl.pallas_call(
        functools.partial(_dsa_kernel, d=dilation, span=span),
        grid=(batch, steps),
        in_specs=[col(0), col(1), col(2)],
        out_specs=[row(DSA_WIDTH), row(LANES)],
        out_shape=[
            jax.ShapeDtypeStruct((batch * seq, DSA_WIDTH), BF16),
            jax.ShapeDtypeStruct((batch * seq, LANES), F32),
        ],
        scratch_shapes=[
            pltpu.VMEM((4, span, LANES), F32),
            pltpu.VMEM((span, DSA_WIDTH), BF16),
            pltpu.VMEM((kv_rows, DSA_WIDTH), BF16),
            pltpu.VMEM((kv_rows, DSA_WIDTH), BF16),
            pltpu.VMEM((span, LANES), F32),
        ] + ([pltpu.VMEM((4, span, LANES), F32)] if dilation > 4 else []),
        compiler_params=pltpu.CompilerParams(
            dimension_semantics=("parallel", "arbitrary"), vmem_limit_bytes=VMEM_LIMIT_BYTES),
        name=f"dsa_attn_d{dilation}",
    )(main2, main2, main2)


def _silu(x):
    return x * (1.0 / (1.0 + jnp.exp(-x)))


def _sigmoid(x):
    return 1.0 / (1.0 + jnp.exp(-x))


def _out_proj_kernel(o0_ref, o1_ref, o2_ref, l0_ref, l1_ref, l2_ref, za_ref, ob_ref, zb_ref, ga_ref, gb_ref,
                     x_ref, wa_ref, wb_ref, wo_ref, fw_ref, ex_ref, out_ref):
    l0, l1, l2 = l0_ref[...], l1_ref[...], l2_ref[...]
    mx = jnp.maximum(jnp.maximum(l0, l1), l2)
    e0, e1, e2 = jnp.exp(l0 - mx), jnp.exp(l1 - mx), jnp.exp(l2 - mx)
    inv = 1.0 / (e0 + e1 + e2)

    def per_lane(w):
        return jnp.dot(w.astype(BF16), ex_ref[...], preferred_element_type=F32)

    o_a = (per_lane(e0 * inv) * o0_ref[...].astype(F32) + per_lane(e1 * inv) * o1_ref[...].astype(F32)
           + per_lane(e2 * inv) * o2_ref[...].astype(F32))
    a_in = (o_a * _silu(za_ref[...].astype(F32))).astype(BF16)
    b_in = (ob_ref[...].astype(F32) * _silu(zb_ref[...].astype(F32))).astype(BF16)
    y_a = jnp.dot(a_in, wa_ref[...], preferred_element_type=F32)
    y_b = jnp.dot(b_in, wb_ref[...], preferred_element_type=F32)
    merged = _sigmoid(ga_ref[...].astype(F32)) * y_a + _sigmoid(gb_ref[...].astype(F32)) * y_b
    y = x_ref[...] + jnp.dot(merged.astype(BF16), wo_ref[...], preferred_element_type=F32)
    ms = jnp.mean(y * y, axis=-1, keepdims=True)
    out_ref[...] = y * lax.rsqrt(ms + NORM_EPS) * fw_ref[...]


def _out_proj(o_groups, lse_groups, rest, o_b, x2, w_up_a, w_up_b, w_out, final_w):
    rows = x2.shape[0]
    tm = min(512, rows)
    row128 = pl.BlockSpec((tm, LANES), lambda i: (i, 0))
    row512 = lambda c: pl.BlockSpec((tm, 512), lambda i: (i, c))
    row1024 = lambda c: pl.BlockSpec((tm, 1024), lambda i: (i, c))
    whole = lambda a: pl.BlockSpec(a.shape, lambda i: (0,) * a.ndim)
    expand = jnp.asarray(np.arange(LANES)[:, None] == LSE_REP * (np.arange(DSA_WIDTH)[None, :] // DSA_HEAD_DIM), BF16)
    return pl.pallas_call(
        _out_proj_kernel,
        grid=(rows // tm,),
        in_specs=[row512(0)] * 3 + [row128] * 3 + [
            row512(GATE_DSA_Z), row512(0), row512(GATE_GDN_Z), row1024(GATE_A), row1024(GATE_B), row1024(0),
            whole(w_up_a), whole(w_up_b), whole(w_out), whole(final_w), whole(expand)],
        out_specs=pl.BlockSpec((tm, D_MODEL), lambda i: (i, 0)),
        out_shape=jax.ShapeDtypeStruct((rows, D_MODEL), F32),
        compiler_params=pltpu.CompilerParams(
            dimension_semantics=("parallel",), vmem_limit_bytes=VMEM_LIMIT_BYTES),
        name="out_proj",
    )(*o_groups, *lse_groups, rest, o_b, rest, rest, rest, x2, w_up_a, w_up_b, w_out, final_w, expand)


QUAD = 4 * GDN_DIM
GDN_GROUP = 16


def _split_hi_lo(x):
    hi = x.astype(BF16)
    lo = (x - hi.astype(F32)).astype(BF16)
    return hi, lo


def _softplus(x):
    return jnp.maximum(x, 0.0) + jnp.log1p(jnp.exp(-jnp.abs(x)))


def _bmm(a, b):
    return lax.dot_general(a, b, (((2,), (1,)), ((0,), (0,))), preferred_element_type=F32)


def _bmm_nt(a, b):
    return lax.dot_general(a, b, (((2,), (2,)), ((0,), (0,))), preferred_element_type=F32)


def _block_diag(x, bd_mask):
    t = jnp.concatenate([x, x, x, x], axis=1)
    return jnp.where(bd_mask, t, jnp.zeros_like(t))


def _gdn_kernel(q_ref, k_ref, v_ref, lg_ref, alog_ref, dtb_ref, nw_ref, e512_ref, eb_ref, eg_ref, tri_ref, o_ref,
                beta_ref, gh_ref, gl_ref, lhs_ref, oloc_ref, snew_ref, dec_ref, oraw_ref, state_ref, *, tg):
    first = pl.program_id(1) == 0
    nc = tg // GDN_CHUNK
    nb = 2 * nc

    @pl.when(first)
    def _():
        state_ref[...] = jnp.zeros_like(state_ref)

    logits = lg_ref[...]
    beta_hi, beta_lo = _split_hi_lo(_sigmoid(logits))
    g_hi, g_lo = _split_hi_lo(-jnp.exp(alog_ref[...]) * _softplus(logits + dtb_ref[...]))
    beta_ref[...] = (jnp.dot(beta_hi, eb_ref[...], preferred_element_type=F32)
                     + jnp.dot(beta_lo, eb_ref[...], preferred_element_type=F32))
    gh_ref[...] = jnp.dot(g_hi, eg_ref[...], preferred_element_type=F32).astype(BF16)
    gl_ref[...] = jnp.dot(g_lo, eg_ref[...], preferred_element_type=F32).astype(BF16)

    row = lax.broadcasted_iota(jnp.int32, (GDN_CHUNK, QUAD), 0)
    col = lax.broadcasted_iota(jnp.int32, (GDN_CHUNK, QUAD), 1) % GDN_CHUNK
    incl = row >= col
    strict = row > col
    eye = (row == col).astype(F32)
    bd_mask = (lax.broadcasted_iota(jnp.int32, (QUAD, QUAD), 0) // GDN_DIM
               == lax.broadcasted_iota(jnp.int32, (QUAD, QUAD), 1) // GDN_DIM)
    tri = tri_ref[...]

    def chunk_quads(ref, c0, n, dtype):
        a = ref[pl.ds(pl.multiple_of(c0 * GDN_CHUNK, GDN_CHUNK), n * GDN_CHUNK), :]
        a = a.reshape(n, GDN_CHUNK, GDN_WIDTH).astype(dtype)
        return jnp.stack([a[:, :, :QUAD], a[:, :, QUAD:]], axis=1).reshape(2 * n, GDN_CHUNK, QUAD)

    n_per = GDN_GROUP // 2

    def local_pass(it, carry):
        c0 = it * n_per
        q, k, v = (chunk_quads(r, c0, n_per, F32) for r in (q_ref, k_ref, v_ref))
        beta = chunk_quads(beta_ref, c0, n_per, F32)
        gh, gl = chunk_quads(gh_ref, c0, n_per, BF16), chunk_quads(gl_ref, c0, n_per, BF16)

        zero = jnp.zeros_like(gh)
        rhs_hi = jnp.concatenate([gh, jnp.where(strict, gh, zero)], axis=-1)
        rhs_lo = jnp.concatenate([gl, jnp.where(strict, gl, zero)], axis=-1)
        tri_b = jnp.broadcast_to(tri, (GDN_GROUP, GDN_CHUNK, GDN_CHUNK))
        gd = _bmm(tri_b, rhs_hi) + _bmm(tri_b, rhs_lo)
        g_cum, d_pair = gd[:, :, :QUAD], gd[:, :, QUAD:]
        decay_incl = jnp.where(incl, jnp.exp(d_pair), 0.0)
        decay_strict = jnp.where(strict, decay_incl, 0.0)
        exp_g = jnp.exp(g_cum)
        g_last = g_cum[:, GDN_CHUNK - 1:GDN_CHUNK, :]
        k_dec = k * jnp.exp(g_last - g_cum)
        k_beta = k * beta

        k_bd = _block_diag(k.astype(BF16), bd_mask)
        aa = _bmm_nt(jnp.concatenate([k_beta, q], axis=1).astype(BF16), k_bd)
        a = aa[:, :GDN_CHUNK] * decay_strict
        attn = aa[:, GDN_CHUNK:] * decay_incl

        p = a
        t = eye - a
        p = _bmm(p.astype(BF16), _block_diag(p.astype(BF16), bd_mask))
        for _ in range(4):
            r = _bmm(jnp.concatenate([p, t], axis=1).astype(BF16), _block_diag(p.astype(BF16), bd_mask))
            p = r[:, :GDN_CHUNK]
            t = t + r[:, GDN_CHUNK:]
        t = t + _bmm(t.astype(BF16), _block_diag(p.astype(BF16), bd_mask))
        t16 = t.astype(BF16)
        u = _bmm(t16, _block_diag((v * beta).astype(BF16), bd_mask))
        w = _bmm(t16, _block_diag((k_beta * exp_g).astype(BF16), bd_mask))

        w16, u16 = w.astype(BF16), u.astype(BF16)
        kt = _bmm(jnp.swapaxes(k_dec, 1, 2).astype(BF16), jnp.concatenate([w16, u16], axis=-1))
        ao = _bmm(attn.astype(BF16),
                  jnp.concatenate([_block_diag(w16, bd_mask), _block_diag(u16, bd_mask)], axis=-1))
        sl = pl.ds(pl.multiple_of(it * GDN_GROUP, GDN_GROUP), GDN_GROUP)
        lhs_ref[sl] = jnp.concatenate(
            [q * exp_g - ao[:, :, :QUAD], jnp.where(bd_mask, -kt[:, :, :QUAD], 0.0)], axis=1).astype(BF16)
        oloc_ref[sl] = ao[:, :, QUAD:]
        snew_ref[sl] = jnp.where(bd_mask, kt[:, :, QUAD:], 0.0)
        dec_ref[sl] = jnp.broadcast_to(jnp.exp(g_last), (GDN_GROUP, 8, QUAD))
        return carry

    lax.fori_loop(0, nb // GDN_GROUP, local_pass, 0)

    def scan_step(c, carry):
        sl = pl.ds(pl.multiple_of(2 * c, 2), 2)
        state = state_ref[...]
        r = _bmm(lhs_ref[sl], state.astype(BF16))
        state_ref[...] = state * dec_ref[sl][:, 0:1, :] + r[:, GDN_CHUNK:] + snew_ref[sl]
        o = r[:, :GDN_CHUNK] + oloc_ref[sl]
        rows = pl.ds(pl.multiple_of(c * GDN_CHUNK, GDN_CHUNK), GDN_CHUNK)
        oraw_ref[rows, 0:QUAD] = o[0]
        oraw_ref[rows, QUAD:] = o[1]
        return carry

    lax.fori_loop(0, nc, scan_step, 0)

    o_all = oraw_ref[...]
    ms = jnp.dot((o_all * o_all).astype(BF16), e512_ref[...], preferred_element_type=F32) * (1.0 / GDN_DIM)
    o_ref[...] = (o_all * lax.rsqrt(ms + NORM_EPS) * nw_ref[...]).astype(o_ref.dtype)


def _gdn(gdn_qkv, logits, alog_row, dtb_row, norm_row, batch, seq):
    rows = gdn_qkv.shape[0]
    tg = min(512, seq)
    steps = seq // tg
    nb = 2 * (tg // GDN_CHUNK)
    head_of_lane = np.arange(GDN_WIDTH) // GDN_DIM
    e512 = jnp.asarray(head_of_lane[:, None] == head_of_lane[None, :], BF16)
    e_beta = jnp.asarray(np.arange(LOGIT_PAD)[:, None] == head_of_lane[None, :], BF16)
    e_g = jnp.asarray(np.arange(LOGIT_PAD)[:, None] == head_of_lane[None, :] + GDN_HEADS, BF16)
    tri = jnp.asarray(np.tril(np.ones((GDN_CHUNK, GDN_CHUNK))), BF16)

    cur = lambda c: pl.BlockSpec((tg, 512), lambda b, n: (b * steps + n, c))
    whole = lambda a: pl.BlockSpec(a.shape, lambda b, n: (0,) * a.ndim)
    consts = (alog_row, dtb_row, norm_row, e512, e_beta, e_g, tri)
    return pl.pallas_call(
        functools.partial(_gdn_kernel, tg=tg),
        grid=(batch, steps),
        in_specs=[cur(0), cur(1), cur(2),
                  pl.BlockSpec((tg, LOGIT_PAD), lambda b, n: (b * steps + n, 0))] + [whole(a) for a in consts],
        out_specs=pl.BlockSpec((tg, GDN_WIDTH), lambda b, n: (b * steps + n, 0)),
        out_shape=jax.ShapeDtypeStruct((rows, GDN_WIDTH), BF16),
        scratch_shapes=[
            pltpu.VMEM((tg, GDN_WIDTH), F32),
            pltpu.VMEM((tg, GDN_WIDTH), BF16),
            pltpu.VMEM((tg, GDN_WIDTH), BF16),
            pltpu.VMEM((nb, GDN_CHUNK + QUAD, QUAD), BF16),
            pltpu.VMEM((nb, GDN_CHUNK, QUAD), F32),
            pltpu.VMEM((nb, QUAD, QUAD), F32),
            pltpu.VMEM((nb, 8, QUAD), F32),
            pltpu.VMEM((tg, GDN_WIDTH), F32),
            pltpu.VMEM((2, QUAD, QUAD), F32),
        ],
        compiler_params=pltpu.CompilerParams(
            dimension_semantics=("parallel", "arbitrary"), vmem_limit_bytes=VMEM_LIMIT_BYTES),
        name="gdn",
    )(gdn_qkv, gdn_qkv, gdn_qkv, logits, *consts)


def _rope_tables(seq):
    inv_freq = ROPE_THETA ** (-jnp.arange(0, DSA_HEAD_DIM, 2, dtype=F32) / DSA_HEAD_DIM)
    ang = jnp.arange(seq, dtype=F32)[:, None] * inv_freq[None, :]
    cos, sin = jnp.cos(ang), jnp.sin(ang)
    return jnp.concatenate([cos, cos, cos, cos], axis=-1), jnp.concatenate([-sin, sin, -sin, sin], axis=-1)


def kernel(x, norm_w, w_in, conv_w, a_log, dt_bias, gdn_norm_w, w_up_a, w_up_b, w_out, final_norm_w):
    batch, seq, _ = x.shape
    assert norm_w.shape[0] == 1, "the final RMSNorm is fused into the (single) layer's output kernel"
    cos_t, sin_t = _rope_tables(seq)
    x2 = x.reshape(batch * seq, D_MODEL)
    w = w_in[0]
    gates = LOGIT_START + 2 * GDN_HEADS
    dsa_z, gdn_in, gdn_z = QKV_COLS, QKV_COLS + DSA_WIDTH, QKV_COLS + DSA_WIDTH + GROUP_COLS
    w_all = jnp.concatenate(
        [w[:, :QKV_COLS], w[:, gdn_in:gdn_z], w[:, gates:], w[:, dsa_z:gdn_in], w[:, gdn_z:LOGIT_START],
         w[:, LOGIT_START:gates], jnp.zeros((D_MODEL, LOGIT_PAD - 2 * GDN_HEADS), w.dtype)], axis=1).astype(BF16)
    qkv, gdn_qkv, rest, logits = _in_proj(x2, norm_w[0][None, :], w_all, conv_w[0], cos_t, sin_t, seq)

    o_groups, lse_groups = [], []
    for g, (_, dilation) in enumerate(DSA_PATTERNS):
        o_g, lse_g = _dsa_attention(qkv, g, dilation, batch, seq)
        o_groups.append(o_g)
        lse_groups.append(lse_g)

    pad8 = lambda p: jnp.pad(p.astype(F32), (GDN_HEADS, LOGIT_PAD - 2 * GDN_HEADS))[None, :]
    o_b = _gdn(gdn_qkv, logits, pad8(a_log[0]), pad8(dt_bias[0]),
               jnp.tile(gdn_norm_w[0].astype(F32), GDN_HEADS)[None, :], batch, seq)

    out = _out_proj(o_groups, lse_groups, rest, o_b, x2,
                    w_up_a[0].astype(BF16), w_up_b[0].astype(BF16), w_out[0].astype(BF16),
                    final_norm_w[None, :])
    return out.reshape(batch, seq, D_MODEL)
```

```python
import functools

import numpy as np
import jax
import jax.numpy as jnp
from jax import lax
from jax.experimental import pallas as pl
from jax.experimental.pallas import tpu as pltpu

F32 = jnp.float32
BF16 = jnp.bfloat16

D_MODEL = 1024
DSA_PATTERNS = ((128, 1), (512, 4), (2048, 16))
DSA_HEADS = 8
DSA_HEAD_DIM = 64
DSA_WIDTH = DSA_HEADS * DSA_HEAD_DIM
DSA_BLOCK = 128
ROPE_THETA = 10000.0
GDN_HEADS = 8
GDN_DIM = 64
GDN_WIDTH = GDN_HEADS * GDN_DIM
GDN_CONV = 4
GDN_CHUNK = 64
NORM_EPS = 1e-6

QKV_COLS = 3 * 3 * DSA_WIDTH
GROUP_COLS = 3 * DSA_WIDTH
GATE_COLS = 3072
GATE_A, GATE_B = 0, 1
GATE_DSA_Z, GATE_GDN_Z = 4, 5
LOGIT_START = 7168
LOGIT_PAD = 128
PROJ_ROWS = 512
LOG2_E = 1.4426950408889634
LN_2 = 0.6931471805599453
DSA_Q_SCALE = DSA_HEAD_DIM ** -0.5 * LOG2_E

VMEM_LIMIT_BYTES = 56 * 1024 * 1024
LANES = 128
NEG_BIG = -1e30


def _lane_iota(shape):
    return lax.broadcasted_iota(jnp.int32, shape, len(shape) - 1)


def _in_proj_qkv_kernel(x_ref, nw_ref, w_ref, cos_ref, sin_ref, qkv_ref, h_ref):
    x = x_ref[...]
    ms = jnp.mean(x * x, axis=-1, keepdims=True)
    h = (x * lax.rsqrt(ms + NORM_EPS) * nw_ref[...]).astype(BF16)
    h_ref[...] = h
    cos = cos_ref[...]
    sin = sin_ref[...]
    low_half = (_lane_iota(cos.shape) % DSA_HEAD_DIM) < (DSA_HEAD_DIM // 2)
    for g in range(len(DSA_PATTERNS)):
        c0 = g * GROUP_COLS
        acc = jnp.dot(h, w_ref[:, c0:c0 + GROUP_COLS], preferred_element_type=F32)
        for c in range(8):
            t = acc[:, c * LANES:(c + 1) * LANES]
            rot = jnp.where(low_half, pltpu.roll(t, LANES - 32, 1), pltpu.roll(t, 32, 1))
            r = t * cos + rot * sin
            if c < 4:
                r = r * DSA_Q_SCALE
            qkv_ref[:, c0 + c * LANES:c0 + (c + 1) * LANES] = r.astype(BF16)
        qkv_ref[:, c0 + 8 * LANES:c0 + GROUP_COLS] = acc[:, 8 * LANES:].astype(BF16)


def _in_proj_rest_kernel(h_ref, w_ref, wl_ref, cw_ref, e512_ref, gdn_ref, gates_ref, logit_ref, xs_ref, *,
                         tiles_per_seq):
    tm = h_ref.shape[0]
    first = pl.program_id(0) % tiles_per_seq == 0

    @pl.when(first)
    def _():
        xs_ref[0:8, :] = jnp.zeros((8, GROUP_COLS), F32)

    @pl.when(jnp.logical_not(first))
    def _():
        xs_ref[0:8, :] = xs_ref[tm:tm + 8, :]

    h = h_ref[...]
    logit_ref[...] = jnp.dot(h, wl_ref[...], preferred_element_type=F32)
    xs_ref[8:8 + tm, :] = jnp.dot(h, w_ref[:, 0:GROUP_COLS], preferred_element_type=F32)
    for g in range(GATE_COLS // GROUP_COLS):
        c0 = g * GROUP_COLS
        gates_ref[:, c0:c0 + GROUP_COLS] = jnp.dot(
            h, w_ref[:, GROUP_COLS + c0:2 * GROUP_COLS + c0], preferred_element_type=F32).astype(BF16)
    for part in range(3):
        cols = slice(part * GDN_WIDTH, (part + 1) * GDN_WIDTH)
        w = cw_ref[:, cols]
        y = w[0:1, :] * xs_ref[5:5 + tm, cols]
        for j in range(1, GDN_CONV):
            y = y + w[j:j + 1, :] * xs_ref[5 + j:5 + j + tm, cols]
        y = _silu(y)
        if part < 2:
            ss = jnp.dot((y * y).astype(BF16), e512_ref[...], preferred_element_type=F32)
            y = y * (lax.rsqrt(ss + NORM_EPS) * (GDN_DIM ** -0.5 if part == 0 else 1.0))
        gdn_ref[:, cols] = y.astype(BF16)


def _resident(shape, col_block=0):
    return pl.BlockSpec(shape, lambda i: (0,) * (len(shape) - 1) + (col_block,), pipeline_mode=pl.Buffered(1))


def _in_proj(x2, norm_w, w_bf16, w_rest, conv_w, cos_t, sin_t, seq):
    rest_cols = GROUP_COLS + GATE_COLS
    assert rest_cols % LOGIT_PAD == 0
    rows = x2.shape[0]
    tm = min(PROJ_ROWS, seq)
    n_seq_tiles = seq // tm
    row = lambda width: pl.BlockSpec((tm, width), lambda i: (i, 0))
    table = pl.BlockSpec((tm, LANES), lambda i: (i % n_seq_tiles, 0))
    params = pltpu.CompilerParams(dimension_semantics=("parallel",), vmem_limit_bytes=VMEM_LIMIT_BYTES)
    head_of_lane = np.arange(GDN_WIDTH) // GDN_DIM
    e512 = jnp.asarray(head_of_lane[:, None] == head_of_lane[None, :], BF16)
    qkv, h = pl.pallas_call(
        _in_proj_qkv_kernel,
        grid=(rows // tm,),
        in_specs=[row(D_MODEL), _resident((1, D_MODEL)), _resident((D_MODEL, QKV_COLS)), table, table],
        out_specs=[row(QKV_COLS), row(D_MODEL)],
        out_shape=[jax.ShapeDtypeStruct((rows, QKV_COLS), BF16), jax.ShapeDtypeStruct((rows, D_MODEL), BF16)],
        compiler_params=params,
        name="in_proj_qkv",
    )(x2, norm_w, w_bf16, cos_t, sin_t)
    gdn_qkv, gates, logits = pl.pallas_call(
        functools.partial(_in_proj_rest_kernel, tiles_per_seq=n_seq_tiles),
        grid=(rows // tm,),
        in_specs=[row(D_MODEL), _resident((D_MODEL, rest_cols)), _resident((D_MODEL, LOGIT_PAD), rest_cols // LOGIT_PAD),
                  _resident(conv_w.shape), _resident(e512.shape)],
        out_specs=[row(GROUP_COLS), row(GATE_COLS), row(LOGIT_PAD)],
        out_shape=[jax.ShapeDtypeStruct((rows, GROUP_COLS), BF16), jax.ShapeDtypeStruct((rows, GATE_COLS), BF16),
                   jax.ShapeDtypeStruct((rows, LOGIT_PAD), F32)],
        scratch_shapes=[pltpu.VMEM((tm + 8, GROUP_COLS), F32)],
        compiler_params=pltpu.CompilerParams(
            dimension_semantics=("arbitrary",), vmem_limit_bytes=VMEM_LIMIT_BYTES),
        name="in_proj_rest",
    )(h, w_rest, w_rest, conv_w, e512)
    return qkv, gdn_qkv, gates, logits


DSA_SPAN = 2048
LSE_REP = LANES // DSA_HEADS


def _dsa_kernel(q_ref, k_ref, v_ref, o_ref, lse_ref, stage_ref, qp_ref, kp_ref, vp_ref, lstage_ref, tmp_ref=None,
                *, d, span):
    first = pl.program_id(1) == 0
    npr = span // d
    n_sub = npr // DSA_BLOCK
    kstride = npr + DSA_BLOCK

    def regroup(src_ref, dst_ref, dst_stride, dst_off):
        if d == 1:
            dst_ref[dst_off:dst_off + npr, :] = src_ref[...]
            return
        for c in range(4):
            stage_ref[c] = src_ref[:, c * LANES:(c + 1) * LANES].astype(F32)
        if d > 4:
            quarter = span // 4
            for c in range(4):
                for r4 in range(4):
                    tmp_ref[c, r4 * quarter:(r4 + 1) * quarter, :] = stage_ref[c, pl.ds(r4, quarter, stride=4), :]
            slabs, stride, start = tmp_ref, d // 4, lambda r: (r % 4) * quarter + r // 4
        else:
            slabs, stride, start = stage_ref, d, lambda r: r
        for c in range(4):
            for r in range(d):
                lo = r * dst_stride + dst_off
                dst_ref[lo:lo + npr, c * LANES:(c + 1) * LANES] = (
                    slabs[c, pl.ds(start(r), npr, stride=stride), :].astype(BF16))

    @pl.when(first)
    def _():
        for r in range(d):
            kp_ref[r * kstride:r * kstride + DSA_BLOCK, :] = jnp.zeros((DSA_BLOCK, DSA_WIDTH), BF16)
            vp_ref[r * kstride:r * kstride + DSA_BLOCK, :] = jnp.zeros((DSA_BLOCK, DSA_WIDTH), BF16)

    regroup(q_ref, qp_ref, npr, 0)
    regroup(k_ref, kp_ref, kstride, DSA_BLOCK)
    regroup(v_ref, vp_ref, kstride, DSA_BLOCK)

    qi = lax.broadcasted_iota(jnp.int32, (DSA_BLOCK, 2 * DSA_BLOCK), 0)
    kj = lax.broadcasted_iota(jnp.int32, (DSA_BLOCK, 2 * DSA_BLOCK), 1)
    band = (kj >= qi) & (kj <= qi + DSA_BLOCK)
    head_a = _lane_iota((DSA_BLOCK, LANES)) < DSA_HEAD_DIM
    lse_head = _lane_iota((DSA_BLOCK, LANES)) // LSE_REP

    def block(blk, carry):
        r, i = blk // n_sub, blk % n_sub
        q0 = pl.multiple_of(r * npr + i * DSA_BLOCK, DSA_BLOCK)
        k0 = pl.multiple_of(r * kstride + i * DSA_BLOCK, DSA_BLOCK)
        nat0 = r + d * DSA_BLOCK * i
        mask = band & (kj >= jnp.where(first & (i == 0), DSA_BLOCK, 0))
        m_tile = jnp.zeros((DSA_BLOCK, LANES), F32)
        den_tile = jnp.ones((DSA_BLOCK, LANES), F32)
        for hp in range(DSA_HEADS // 2):
            cols = slice(hp * LANES, (hp + 1) * LANES)
            q2 = qp_ref[pl.ds(q0, DSA_BLOCK), cols]
            k2 = kp_ref[pl.ds(k0, 2 * DSA_BLOCK), cols]
            v2 = vp_ref[pl.ds(k0, 2 * DSA_BLOCK), cols]
            pvs, ms, dens = [], [], []
            for is_a in (True, False):
                sel = head_a if is_a else jnp.logical_not(head_a)
                qh = jnp.where(sel, q2, jnp.zeros_like(q2))
                s = lax.dot_general(qh, k2, (((1,), (1,)), ((), ())), preferred_element_type=F32)
                s = jnp.where(mask, s, NEG_BIG)
                m = jnp.max(s, axis=-1, keepdims=True)
                p = jnp.exp2(s - m)
                dens.append(jnp.sum(p, axis=-1, keepdims=True))
                pvs.append(jnp.dot(p.astype(BF16), v2, preferred_element_type=F32))
                ms.append(m)
            o_pair = jnp.where(head_a, pvs[0], pvs[1]) / jnp.where(head_a, dens[0], dens[1])
            den_tile = jnp.where(lse_head == 2 * hp, dens[0], jnp.where(lse_head == 2 * hp + 1, dens[1], den_tile))
            m_tile = jnp.where(lse_head == 2 * hp, ms[0], jnp.where(lse_head == 2 * hp + 1, ms[1], m_tile))
            if d == 1:
                o_ref[pl.ds(q0, DSA_BLOCK), cols] = o_pair.astype(o_ref.dtype)
            else:
                stage_ref[hp, pl.ds(nat0, DSA_BLOCK, stride=d), :] = o_pair
        lse_tile = m_tile * LN_2 + jnp.log(den_tile)
        if d == 1:
            lse_ref[pl.ds(q0, DSA_BLOCK), :] = lse_tile
        else:
            lstage_ref[pl.ds(nat0, DSA_BLOCK, stride=d), :] = lse_tile
        return carry

    lax.fori_loop(0, d * n_sub, block, 0, unroll=4)

    for r in range(d):
        lo = r * kstride
        kp_ref[lo:lo + DSA_BLOCK, :] = kp_ref[lo + npr:lo + npr + DSA_BLOCK, :]
        vp_ref[lo:lo + DSA_BLOCK, :] = vp_ref[lo + npr:lo + npr + DSA_BLOCK, :]
    if d > 1:
        for c in range(4):
            o_ref[:, c * LANES:(c + 1) * LANES] = stage_ref[c].astype(o_ref.dtype)
        lse_ref[...] = lstage_ref[...]


def _dsa_attention(main2, group, dilation, batch, seq):
    span = min(DSA_SPAN, seq)
    steps = seq // span
    kv_rows = span + dilation * DSA_BLOCK
    col = lambda t: pl.BlockSpec((span, 512), lambda b, n: (b * steps + n, group * 3 + t))
    row = lambda width: pl.BlockSpec((span, width), lambda b, n: (b * steps + n, 0))
    return pl.pallas_call(
        functools.partial(_dsa_kernel, d=dilation, span=span),
        grid=(batch, steps),
        in_specs=[col(0), col(1), col(2)],
        out_specs=[row(DSA_WIDTH), row(LANES)],
        out_shape=[
            jax.ShapeDtypeStruct((batch * seq, DSA_WIDTH), BF16),
            jax.ShapeDtypeStruct((batch * seq, LANES), F32),
        ],
        scratch_shapes=[
            pltpu.VMEM((4, span, LANES), F32),
            pltpu.VMEM((span, DSA_WIDTH), BF16),
            pltpu.VMEM((kv_rows, DSA_WIDTH), BF16),
            pltpu.VMEM((kv_rows, DSA_WIDTH), BF16),
            pltpu.VMEM((span, LANES), F32),
        ] + ([pltpu.VMEM((4, span, LANES), F32)] if dilation > 4 else []),
        compiler_params=pltpu.CompilerParams(
            dimension_semantics=("parallel", "arbitrary"), vmem_limit_bytes=VMEM_LIMIT_BYTES),
        name=f"dsa_attn_d{dilation}",
    )(main2, main2, main2)


def _silu(x):
    return x * (1.0 / (1.0 + jnp.exp(-x)))


def _sigmoid(x):
    return 1.0 / (1.0 + jnp.exp(-x))


def _out_proj_kernel(o0_ref, o1_ref, o2_ref, l0_ref, l1_ref, l2_ref, za_ref, ob_ref, zb_ref, ga_ref, gb_ref,
                     x_ref, wa_ref, wb_ref, wo_ref, fw_ref, ex_ref, out_ref):
    l0, l1, l2 = l0_ref[...], l1_ref[...], l2_ref[...]
    mx = jnp.maximum(jnp.maximum(l0, l1), l2)
    e0, e1, e2 = jnp.exp(l0 - mx), jnp.exp(l1 - mx), jnp.exp(l2 - mx)
    inv = 1.0 / (e0 + e1 + e2)

    def per_lane(w):
        return jnp.dot(w.astype(BF16), ex_ref[...], preferred_element_type=F32)

    o_a = (per_lane(e0 * inv) * o0_ref[...].astype(F32) + per_lane(e1 * inv) * o1_ref[...].astype(F32)
           + per_lane(e2 * inv) * o2_ref[...].astype(F32))
    a_in = (o_a * _silu(za_ref[...].astype(F32))).astype(BF16)
    b_in = (ob_ref[...].astype(F32) * _silu(zb_ref[...].astype(F32))).astype(BF16)
    y_a = jnp.dot(a_in, wa_ref[...], preferred_element_type=F32)
    y_b = jnp.dot(b_in, wb_ref[...], preferred_element_type=F32)
    merged = _sigmoid(ga_ref[...].astype(F32)) * y_a + _sigmoid(gb_ref[...].astype(F32)) * y_b
    y = x_ref[...] + jnp.dot(merged.astype(BF16), wo_ref[...], preferred_element_type=F32)
    ms = jnp.mean(y * y, axis=-1, keepdims=True)
    out_ref[...] = y * lax.rsqrt(ms + NORM_EPS) * fw_ref[...]


def _out_proj(o_groups, lse_groups, rest, o_b, x2, w_up_a, w_up_b, w_out, final_w):
    rows = x2.shape[0]
    tm = min(512, rows)
    row128 = pl.BlockSpec((tm, LANES), lambda i: (i, 0))
    row512 = lambda c: pl.BlockSpec((tm, 512), lambda i: (i, c))
    row1024 = lambda c: pl.BlockSpec((tm, 1024), lambda i: (i, c))
    whole = lambda a: pl.BlockSpec(a.shape, lambda i: (0,) * a.ndim)
    expand = jnp.asarray(np.arange(LANES)[:, None] == LSE_REP * (np.arange(DSA_WIDTH)[None, :] // DSA_HEAD_DIM), BF16)
    return pl.pallas_call(
        _out_proj_kernel,
        grid=(rows // tm,),
        in_specs=[row512(0)] * 3 + [row128] * 3 + [
            row512(GATE_DSA_Z), row512(0), row512(GATE_GDN_Z), row1024(GATE_A), row1024(GATE_B), row1024(0),
            whole(w_up_a), whole(w_up_b), whole(w_out), whole(final_w), whole(expand)],
        out_specs=pl.BlockSpec((tm, D_MODEL), lambda i: (i, 0)),
        out_shape=jax.ShapeDtypeStruct((rows, D_MODEL), F32),
        compiler_params=pltpu.CompilerParams(
            dimension_semantics=("parallel",), vmem_limit_bytes=VMEM_LIMIT_BYTES),
        name="out_proj",
    )(*o_groups, *lse_groups, rest, o_b, rest, rest, rest, x2, w_up_a, w_up_b, w_out, final_w, expand)


QUAD = 4 * GDN_DIM
GDN_GROUP = 16


def _split_hi_lo(x):
    hi = x.astype(BF16)
    lo = (x - hi.astype(F32)).astype(BF16)
    return hi, lo


def _softplus(x):
    return jnp.maximum(x, 0.0) + jnp.log1p(jnp.exp(-jnp.abs(x)))


def _bmm(a, b):
    return lax.dot_general(a, b, (((2,), (1,)), ((0,), (0,))), preferred_element_type=F32)


def _bmm_nt(a, b):
    return lax.dot_general(a, b, (((2,), (2,)), ((0,), (0,))), preferred_element_type=F32)


def _block_diag(x, bd_mask):
    t = jnp.concatenate([x, x, x, x], axis=1)
    return jnp.where(bd_mask, t, jnp.zeros_like(t))


def _gdn_kernel(q_ref, k_ref, v_ref, lg_ref, alog_ref, dtb_ref, nw_ref, e512_ref, eb_ref, eg_ref, tri_ref, o_ref,
                beta_ref, gh_ref, gl_ref, lhs_ref, oloc_ref, snew_ref, dec_ref, oraw_ref, state_ref, *, tg):
    first = pl.program_id(1) == 0
    nc = tg // GDN_CHUNK
    nb = 2 * nc

    @pl.when(first)
    def _():
        state_ref[...] = jnp.zeros_like(state_ref)

    logits = lg_ref[...]
    beta_hi, beta_lo = _split_hi_lo(_sigmoid(logits))
    g_hi, g_lo = _split_hi_lo(-jnp.exp(alog_ref[...]) * _softplus(logits + dtb_ref[...]))
    beta_ref[...] = (jnp.dot(beta_hi, eb_ref[...], preferred_element_type=F32)
                     + jnp.dot(beta_lo, eb_ref[...], preferred_element_type=F32))
    gh_ref[...] = jnp.dot(g_hi, eg_ref[...], preferred_element_type=F32).astype(BF16)
    gl_ref[...] = jnp.dot(g_lo, eg_ref[...], preferred_element_type=F32).astype(BF16)

    row = lax.broadcasted_iota(jnp.int32, (GDN_CHUNK, QUAD), 0)
    col = lax.broadcasted_iota(jnp.int32, (GDN_CHUNK, QUAD), 1) % GDN_CHUNK
    incl = row >= col
    strict = row > col
    eye = (row == col).astype(F32)
    bd_mask = (lax.broadcasted_iota(jnp.int32, (QUAD, QUAD), 0) // GDN_DIM
               == lax.broadcasted_iota(jnp.int32, (QUAD, QUAD), 1) // GDN_DIM)
    tri = tri_ref[...]

    def chunk_quads(ref, c0, n, dtype):
        a = ref[pl.ds(pl.multiple_of(c0 * GDN_CHUNK, GDN_CHUNK), n * GDN_CHUNK), :]
        a = a.reshape(n, GDN_CHUNK, GDN_WIDTH).astype(dtype)
        return jnp.stack([a[:, :, :QUAD], a[:, :, QUAD:]], axis=1).reshape(2 * n, GDN_CHUNK, QUAD)

    n_per = GDN_GROUP // 2

    def local_pass(it, carry):
        c0 = it * n_per
        q, k, v = (chunk_quads(r, c0, n_per, F32) for r in (q_ref, k_ref, v_ref))
        beta = chunk_quads(beta_ref, c0, n_per, F32)
        gh, gl = chunk_quads(gh_ref, c0, n_per, BF16), chunk_quads(gl_ref, c0, n_per, BF16)

        zero = jnp.zeros_like(gh)
        rhs_hi = jnp.concatenate([gh, jnp.where(strict, gh, zero)], axis=-1)
        rhs_lo = jnp.concatenate([gl, jnp.where(strict, gl, zero)], axis=-1)
        tri_b = jnp.broadcast_to(tri, (GDN_GROUP, GDN_CHUNK, GDN_CHUNK))
        gd = _bmm(tri_b, rhs_hi) + _bmm(tri_b, rhs_lo)
        g_cum, d_pair = gd[:, :, :QUAD], gd[:, :, QUAD:]
        decay_incl = jnp.where(incl, jnp.exp(d_pair), 0.0)
        decay_strict = jnp.where(strict, decay_incl, 0.0)
        exp_g = jnp.exp(g_cum)
        g_last = g_cum[:, GDN_CHUNK - 1:GDN_CHUNK, :]
        k_dec = k * jnp.exp(g_last - g_cum)
        k_beta = k * beta

        k_bd = _block_diag(k.astype(BF16), bd_mask)
        aa = _bmm_nt(jnp.concatenate([k_beta, q], axis=1).astype(BF16), k_bd)
        a = aa[:, :GDN_CHUNK] * decay_strict
        attn = aa[:, GDN_CHUNK:] * decay_incl

        p = a
        t = eye - a
        p = _bmm(p.astype(BF16), _block_diag(p.astype(BF16), bd_mask))
        for _ in range(4):
            r = _bmm(jnp.concatenate([p, t], axis=1).astype(BF16), _block_diag(p.astype(BF16), bd_mask))
            p = r[:, :GDN_CHUNK]
            t = t + r[:, GDN_CHUNK:]
        t = t + _bmm(t.astype(BF16), _block_diag(p.astype(BF16), bd_mask))
        t16 = t.astype(BF16)
        u = _bmm(t16, _block_diag((v * beta).astype(BF16), bd_mask))
        w = _bmm(t16, _block_diag((k_beta * exp_g).astype(BF16), bd_mask))

        w16, u16 = w.astype(BF16), u.astype(BF16)
        kt = _bmm(jnp.swapaxes(k_dec, 1, 2).astype(BF16), jnp.concatenate([w16, u16], axis=-1))
        ao = _bmm(attn.astype(BF16),
                  jnp.concatenate([_block_diag(w16, bd_mask), _block_diag(u16, bd_mask)], axis=-1))
        sl = pl.ds(pl.multiple_of(it * GDN_GROUP, GDN_GROUP), GDN_GROUP)
        lhs_ref[sl] = jnp.concatenate(
            [q * exp_g - ao[:, :, :QUAD], jnp.where(bd_mask, -kt[:, :, :QUAD], 0.0)], axis=1).astype(BF16)
        oloc_ref[sl] = ao[:, :, QUAD:]
        snew_ref[sl] = jnp.where(bd_mask, kt[:, :, QUAD:], 0.0)
        dec_ref[sl] = jnp.broadcast_to(jnp.exp(g_last), (GDN_GROUP, 8, QUAD))
        return carry

    lax.fori_loop(0, nb // GDN_GROUP, local_pass, 0)

    def scan_step(c, carry):
        sl = pl.ds(pl.multiple_of(2 * c, 2), 2)
        state = state_ref[...]
        r = _bmm(lhs_ref[sl], state.astype(BF16))
        state_ref[...] = state * dec_ref[sl][:, 0:1, :] + r[:, GDN_CHUNK:] + snew_ref[sl]
        o = r[:, :GDN_CHUNK] + oloc_ref[sl]
        rows = pl.ds(pl.multiple_of(c * GDN_CHUNK, GDN_CHUNK), GDN_CHUNK)
        oraw_ref[rows, 0:QUAD] = o[0]
        oraw_ref[rows, QUAD:] = o[1]
        return carry

    lax.fori_loop(0, nc, scan_step, 0)

    o_all = oraw_ref[...]
    ms = jnp.dot((o_all * o_all).astype(BF16), e512_ref[...], preferred_element_type=F32) * (1.0 / GDN_DIM)
    o_ref[...] = (o_all * lax.rsqrt(ms + NORM_EPS) * nw_ref[...]).astype(o_ref.dtype)


def _gdn(gdn_qkv, logits, alog_row, dtb_row, norm_row, batch, seq):
    rows = gdn_qkv.shape[0]
    tg = min(512, seq)
    steps = seq // tg
    nb = 2 * (tg // GDN_CHUNK)
    head_of_lane = np.arange(GDN_WIDTH) // GDN_DIM
    e512 = jnp.asarray(head_of_lane[:, None] == head_of_lane[None, :], BF16)
    e_beta = jnp.asarray(np.arange(LOGIT_PAD)[:, None] == head_of_lane[None, :], BF16)
    e_g = jnp.asarray(np.arange(LOGIT_PAD)[:, None] == head_of_lane[None, :] + GDN_HEADS, BF16)
    tri = jnp.asarray(np.tril(np.ones((GDN_CHUNK, GDN_CHUNK))), BF16)

    cur = lambda c: pl.BlockSpec((tg, 512), lambda b, n: (b * steps + n, c))
    whole = lambda a: pl.BlockSpec(a.shape, lambda b, n: (0,) * a.ndim)
    consts = (alog_row, dtb_row, norm_row, e512, e_beta, e_g, tri)
    return pl.pallas_call(
        functools.partial(_gdn_kernel, tg=tg),
        grid=(batch, steps),
        in_specs=[cur(0), cur(1), cur(2),
                  pl.BlockSpec((tg, LOGIT_PAD), lambda b, n: (b * steps + n, 0))] + [whole(a) for a in consts],
        out_specs=pl.BlockSpec((tg, GDN_WIDTH), lambda b, n: (b * steps + n, 0)),
        out_shape=jax.ShapeDtypeStruct((rows, GDN_WIDTH), BF16),
        scratch_shapes=[
            pltpu.VMEM((tg, GDN_WIDTH), F32),
            pltpu.VMEM((tg, GDN_WIDTH), BF16),
            pltpu.VMEM((tg, GDN_WIDTH), BF16),
            pltpu.VMEM((nb, GDN_CHUNK + QUAD, QUAD), BF16),
            pltpu.VMEM((nb, GDN_CHUNK, QUAD), F32),
            pltpu.VMEM((nb, QUAD, QUAD), F32),
            pltpu.VMEM((nb, 8, QUAD), F32),
            pltpu.VMEM((tg, GDN_WIDTH), F32),
            pltpu.VMEM((2, QUAD, QUAD), F32),
        ],
        compiler_params=pltpu.CompilerParams(
            dimension_semantics=("parallel", "arbitrary"), vmem_limit_bytes=VMEM_LIMIT_BYTES),
        name="gdn",
    )(gdn_qkv, gdn_qkv, gdn_qkv, logits, *consts)


def _rope_tables(seq):
    inv_freq = ROPE_THETA ** (-jnp.arange(0, DSA_HEAD_DIM, 2, dtype=F32) / DSA_HEAD_DIM)
    ang = jnp.arange(seq, dtype=F32)[:, None] * inv_freq[None, :]
    cos, sin = jnp.cos(ang), jnp.sin(ang)
    return jnp.concatenate([cos, cos, cos, cos], axis=-1), jnp.concatenate([-sin, sin, -sin, sin], axis=-1)


def kernel(x, norm_w, w_in, conv_w, a_log, dt_bias, gdn_norm_w, w_up_a, w_up_b, w_out, final_norm_w):
    batch, seq, _ = x.shape
    assert norm_w.shape[0] == 1, "the final RMSNorm is fused into the (single) layer's output kernel"
    cos_t, sin_t = _rope_tables(seq)
    x2 = x.reshape(batch * seq, D_MODEL)
    w = w_in[0]
    gates = LOGIT_START + 2 * GDN_HEADS
    dsa_z, gdn_in, gdn_z = QKV_COLS, QKV_COLS + DSA_WIDTH, QKV_COLS + DSA_WIDTH + GROUP_COLS
    w = w.astype(BF16)
    w_rest = jnp.concatenate(
        [w[:, gdn_in:gdn_z], w[:, gates:], w[:, dsa_z:gdn_in], w[:, gdn_z:LOGIT_START],
         w[:, LOGIT_START:gates], jnp.zeros((D_MODEL, LOGIT_PAD - 2 * GDN_HEADS), BF16)], axis=1)
    qkv, gdn_qkv, rest, logits = _in_proj(x2, norm_w[0][None, :], w, w_rest, conv_w[0], cos_t, sin_t, seq)

    o_groups, lse_groups = [], []
    for g, (_, dilation) in enumerate(DSA_PATTERNS):
        o_g, lse_g = _dsa_attention(qkv, g, dilation, batch, seq)
        o_groups.append(o_g)
        lse_groups.append(lse_g)

    pad8 = lambda p: jnp.pad(p.astype(F32), (GDN_HEADS, LOGIT_PAD - 2 * GDN_HEADS))[None, :]
    o_b = _gdn(gdn_qkv, logits, pad8(a_log[0]), pad8(dt_bias[0]),
               jnp.tile(gdn_norm_w[0].astype(F32), GDN_HEADS)[None, :], batch, seq)

    out = _out_proj(o_groups, lse_groups, rest, o_b, x2,
                    w_up_a[0].astype(BF16), w_up_b[0].astype(BF16), w_out[0].astype(BF16),
                    final_norm_w[None, :])
    return out.reshape(batch, seq, D_MODEL)
```

```python
import functools

import numpy as np
import jax
import jax.numpy as jnp
from jax import lax
from jax.experimental import pallas as pl
from jax.experimental.pallas import tpu as pltpu

F32 = jnp.float32
BF16 = jnp.bfloat16

D_MODEL = 1024
DSA_PATTERNS = ((128, 1), (512, 4), (2048, 16))
DSA_HEADS = 8
DSA_HEAD_DIM = 64
DSA_WIDTH = DSA_HEADS * DSA_HEAD_DIM
DSA_BLOCK = 128
ROPE_THETA = 10000.0
GDN_HEADS = 8
GDN_DIM = 64
GDN_WIDTH = GDN_HEADS * GDN_DIM
GDN_CONV = 4
GDN_CHUNK = 64
NORM_EPS = 1e-6

QKV_COLS = 3 * 3 * DSA_WIDTH
GROUP_COLS = 3 * DSA_WIDTH
GATE_COLS = 3072
GATE_A, GATE_B = 0, 1
GATE_DSA_Z, GATE_GDN_Z = 4, 5
GDN_IN_START = QKV_COLS + DSA_WIDTH
LOGIT_START = 7168
LOGIT_PAD = 128
PROJ_ROWS = 512
LOG2_E = 1.4426950408889634
LN_2 = 0.6931471805599453
DSA_Q_SCALE = DSA_HEAD_DIM ** -0.5 * LOG2_E

VMEM_LIMIT_BYTES = 56 * 1024 * 1024
LANES = 128
NEG_BIG = -1e30


def _lane_iota(shape):
    return lax.broadcasted_iota(jnp.int32, shape, len(shape) - 1)


def _in_proj_qkv_kernel(x_ref, nw_ref, w_ref, cos_ref, sin_ref, qkv_ref, h_ref):
    x = x_ref[...]
    ms = jnp.mean(x * x, axis=-1, keepdims=True)
    h = (x * lax.rsqrt(ms + NORM_EPS) * nw_ref[...]).astype(BF16)
    h_ref[...] = h
    cos = cos_ref[...]
    sin = sin_ref[...]
    for g in range(len(DSA_PATTERNS)):
        c0 = g * GROUP_COLS
        acc = jnp.dot(h, w_ref[:, c0:c0 + GROUP_COLS], preferred_element_type=F32)
        for c in range(8):
            t = acc[:, c * LANES:(c + 1) * LANES]
            r = t * cos + pltpu.roll(t, LANES // 2, 1) * sin
            if c < 4:
                r = r * DSA_Q_SCALE
            qkv_ref[:, c0 + c * LANES:c0 + (c + 1) * LANES] = r.astype(BF16)
        qkv_ref[:, c0 + 8 * LANES:c0 + GROUP_COLS] = acc[:, 8 * LANES:].astype(BF16)


def _in_proj_rest_kernel(h_ref, wgq_ref, wgk_ref, wgv_ref, w_ref, wl_ref, cw_ref, e512_ref,
                         gdn_ref, gates_ref, logit_ref, xs_ref, *, tiles_per_seq):
    tm = h_ref.shape[0]
    first = pl.program_id(0) % tiles_per_seq == 0

    @pl.when(first)
    def _():
        xs_ref[0:8, :] = jnp.zeros((8, GROUP_COLS), F32)

    @pl.when(jnp.logical_not(first))
    def _():
        xs_ref[0:8, :] = xs_ref[tm:tm + 8, :]

    h = h_ref[...]
    logit_ref[...] = jnp.dot(h, wl_ref[...], preferred_element_type=F32)
    for part, wg_ref in enumerate((wgq_ref, wgk_ref, wgv_ref)):
        xs_ref[8:8 + tm, part * GDN_WIDTH:(part + 1) * GDN_WIDTH] = jnp.dot(
            h, wg_ref[...], preferred_element_type=F32)
    for g in range(GATE_COLS // GROUP_COLS):
        c0 = g * GROUP_COLS
        gates_ref[:, c0:c0 + GROUP_COLS] = jnp.dot(
            h, w_ref[:, c0:c0 + GROUP_COLS], preferred_element_type=F32).astype(BF16)
    for part in range(3):
        cols = slice(part * GDN_WIDTH, (part + 1) * GDN_WIDTH)
        w = cw_ref[:, cols]
        y = w[0:1, :] * xs_ref[5:5 + tm, cols]
        for j in range(1, GDN_CONV):
            y = y + w[j:j + 1, :] * xs_ref[5 + j:5 + j + tm, cols]
        y = _silu(y)
        if part < 2:
            ss = jnp.dot((y * y).astype(BF16), e512_ref[...], preferred_element_type=F32)
            y = y * (lax.rsqrt(ss + NORM_EPS) * (GDN_DIM ** -0.5 if part == 0 else 1.0))
        gdn_ref[:, cols] = y.astype(BF16)


def _resident(shape, col_block=0):
    return pl.BlockSpec(shape, lambda i: (0,) * (len(shape) - 1) + (col_block,), pipeline_mode=pl.Buffered(1))


def _in_proj(x2, norm_w, w_qkv, w_bf16, w_gates, conv_w, cos_t, sin_t, seq):
    assert GDN_IN_START % GDN_WIDTH == 0 and GATE_COLS % LOGIT_PAD == 0
    rows = x2.shape[0]
    tm = min(PROJ_ROWS, seq)
    n_seq_tiles = seq // tm
    row = lambda width: pl.BlockSpec((tm, width), lambda i: (i, 0))
    table = pl.BlockSpec((tm, LANES), lambda i: (i % n_seq_tiles, 0))
    head_of_lane = np.arange(GDN_WIDTH) // GDN_DIM
    e512 = jnp.asarray(head_of_lane[:, None] == head_of_lane[None, :], BF16)
    gdn_block = GDN_IN_START // GDN_WIDTH
    qkv, h = pl.pallas_call(
        _in_proj_qkv_kernel,
        grid=(rows // tm,),
        in_specs=[row(D_MODEL), _resident((1, D_MODEL)), _resident((D_MODEL, QKV_COLS)), table, table],
        out_specs=[row(QKV_COLS), row(D_MODEL)],
        out_shape=[jax.ShapeDtypeStruct((rows, QKV_COLS), BF16), jax.ShapeDtypeStruct((rows, D_MODEL), BF16)],
        compiler_params=pltpu.CompilerParams(dimension_semantics=("parallel",), vmem_limit_bytes=VMEM_LIMIT_BYTES),
        name="in_proj_qkv",
    )(x2, norm_w, w_qkv, cos_t, sin_t)
    gdn_qkv, gates, logits = pl.pallas_call(
        functools.partial(_in_proj_rest_kernel, tiles_per_seq=n_seq_tiles),
        grid=(rows // tm,),
        in_specs=[row(D_MODEL), _resident((D_MODEL, GDN_WIDTH), gdn_block),
                  _resident((D_MODEL, GDN_WIDTH), gdn_block + 1), _resident((D_MODEL, GDN_WIDTH), gdn_block + 2),
                  _resident((D_MODEL, GATE_COLS)), _resident((D_MODEL, LOGIT_PAD), GATE_COLS // LOGIT_PAD),
                  _resident(conv_w.shape), _resident(e512.shape)],
        out_specs=[row(GROUP_COLS), row(GATE_COLS), row(LOGIT_PAD)],
        out_shape=[jax.ShapeDtypeStruct((rows, GROUP_COLS), BF16), jax.ShapeDtypeStruct((rows, GATE_COLS), BF16),
                   jax.ShapeDtypeStruct((rows, LOGIT_PAD), F32)],
        scratch_shapes=[pltpu.VMEM((tm + 8, GROUP_COLS), F32)],
        compiler_params=pltpu.CompilerParams(
            dimension_semantics=("arbitrary",), vmem_limit_bytes=VMEM_LIMIT_BYTES),
        name="in_proj_rest",
    )(h, w_bf16, w_bf16, w_bf16, w_gates, w_gates, conv_w, e512)
    return qkv, gdn_qkv, gates, logits


DSA_SPAN = 2048
LSE_REP = LANES // DSA_HEADS


def _dsa_kernel(q_ref, k_ref, v_ref, o_ref, lse_ref, stage_ref, qp_ref, kp_ref, vp_ref, lstage_ref, tmp_ref=None,
                *, d, span):
    first = pl.program_id(1) == 0
    npr = span // d
    n_sub = npr // DSA_BLOCK
    kstride = npr + DSA_BLOCK

    def regroup(src_ref, dst_ref, dst_stride, dst_off):
        if d == 1:
            dst_ref[dst_off:dst_off + npr, :] = src_ref[...]
            return
        for c in range(4):
            stage_ref[c] = src_ref[:, c * LANES:(c + 1) * LANES].astype(F32)
        if d > 4:
            quarter = span // 4
            for c in range(4):
                for r4 in range(4):
                    tmp_ref[c, r4 * quarter:(r4 + 1) * quarter, :] = stage_ref[c, pl.ds(r4, quarter, stride=4), :]
            slabs, stride, start = tmp_ref, d // 4, lambda r: (r % 4) * quarter + r // 4
        else:
            slabs, stride, start = stage_ref, d, lambda r: r
        for c in range(4):
            for r in range(d):
                lo = r * dst_stride + dst_off
                dst_ref[lo:lo + npr, c * LANES:(c + 1) * LANES] = (
                    slabs[c, pl.ds(start(r), npr, stride=stride), :].astype(BF16))

    @pl.when(first)
    def _():
        for r in range(d):
            kp_ref[r * kstride:r * kstride + DSA_BLOCK, :] = jnp.zeros((DSA_BLOCK, DSA_WIDTH), BF16)
            vp_ref[r * kstride:r * kstride + DSA_BLOCK, :] = jnp.zeros((DSA_BLOCK, DSA_WIDTH), BF16)

    regroup(q_ref, qp_ref, npr, 0)
    regroup(k_ref, kp_ref, kstride, DSA_BLOCK)
    regroup(v_ref, vp_ref, kstride, DSA_BLOCK)

    qi = lax.broadcasted_iota(jnp.int32, (DSA_BLOCK, 2 * DSA_BLOCK), 0)
    kj = lax.broadcasted_iota(jnp.int32, (DSA_BLOCK, 2 * DSA_BLOCK), 1)
    band = (kj >= qi) & (kj <= qi + DSA_BLOCK)
    head_a = _lane_iota((DSA_BLOCK, LANES)) < DSA_HEAD_DIM
    qk_head_a = (_lane_iota((DSA_BLOCK, LANES)) // (DSA_HEAD_DIM // 2)) % 2 == 0
    lse_head = _lane_iota((DSA_BLOCK, LANES)) // LSE_REP

    def block(blk, carry):
        r, i = blk // n_sub, blk % n_sub
        q0 = pl.multiple_of(r * npr + i * DSA_BLOCK, DSA_BLOCK)
        k0 = pl.multiple_of(r * kstride + i * DSA_BLOCK, DSA_BLOCK)
        nat0 = r + d * DSA_BLOCK * i
        mask = band & (kj >= jnp.where(first & (i == 0), DSA_BLOCK, 0))
        m_tile = jnp.zeros((DSA_BLOCK, LANES), F32)
        den_tile = jnp.ones((DSA_BLOCK, LANES), F32)
        for hp in range(DSA_HEADS // 2):
            cols = slice(hp * LANES, (hp + 1) * LANES)
            q2 = qp_ref[pl.ds(q0, DSA_BLOCK), cols]
            k2 = kp_ref[pl.ds(k0, 2 * DSA_BLOCK), cols]
            v2 = vp_ref[pl.ds(k0, 2 * DSA_BLOCK), cols]
            pvs, ms, dens = [], [], []
            for is_a in (True, False):
                sel = qk_head_a if is_a else jnp.logical_not(qk_head_a)
                qh = jnp.where(sel, q2, jnp.zeros_like(q2))
                s = lax.dot_general(qh, k2, (((1,), (1,)), ((), ())), preferred_element_type=F32)
                s = jnp.where(mask, s, NEG_BIG)
                m = jnp.max(s, axis=-1, keepdims=True)
                p = jnp.exp2(s - m)
                dens.append(jnp.sum(p, axis=-1, keepdims=True))
                pvs.append(jnp.dot(p.astype(BF16), v2, preferred_element_type=F32))
                ms.append(m)
            o_pair = jnp.where(head_a, pvs[0], pvs[1]) / jnp.where(head_a, dens[0], dens[1])
            den_tile = jnp.where(lse_head == 2 * hp, dens[0], jnp.where(lse_head == 2 * hp + 1, dens[1], den_tile))
            m_tile = jnp.where(lse_head == 2 * hp, ms[0], jnp.where(lse_head == 2 * hp + 1, ms[1], m_tile))
            if d == 1:
                o_ref[pl.ds(q0, DSA_BLOCK), cols] = o_pair.astype(o_ref.dtype)
            else:
                stage_ref[hp, pl.ds(nat0, DSA_BLOCK, stride=d), :] = o_pair
        lse_tile = m_tile * LN_2 + jnp.log(den_tile)
        if d == 1:
            lse_ref[pl.ds(q0, DSA_BLOCK), :] = lse_tile
        else:
            lstage_ref[pl.ds(nat0, DSA_BLOCK, stride=d), :] = lse_tile
        return carry

    lax.fori_loop(0, d * n_sub, block, 0, unroll=4)

    for r in range(d):
        lo = r * kstride
        kp_ref[lo:lo + DSA_BLOCK, :] = kp_ref[lo + npr:lo + npr + DSA_BLOCK, :]
        vp_ref[lo:lo + DSA_BLOCK, :] = vp_ref[lo + npr:lo + npr + DSA_BLOCK, :]
    if d > 1:
        for c in range(4):
            o_ref[:, c * LANES:(c + 1) * LANES] = stage_ref[c].astype(o_ref.dtype)
        lse_ref[...] = lstage_ref[...]


def _dsa_attention(qkv, group, dilation, batch, seq):
    span = min(DSA_SPAN, seq)
    steps = seq // span
    kv_rows = span + dilation * DSA_BLOCK
    col = lambda t: pl.BlockSpec((span, 512), lambda b, n: (b * steps + n, group * 3 + t))
    row = lambda width: pl.BlockSpec((span, width), lambda b, n: (b * steps + n, 0))
    return pl.pallas_call(
        functools.partial(_dsa_kernel, d=dilation, span=span),
        grid=(batch, steps),
        in_specs=[col(0), col(1), col(2)],
        out_specs=[row(DSA_WIDTH), row(LANES)],
        out_shape=[
            jax.ShapeDtypeStruct((batch * seq, DSA_WIDTH), BF16),
            jax.ShapeDtypeStruct((batch * seq, LANES), F32),
        ],
        scratch_shapes=[
            pltpu.VMEM((4, span, LANES), F32),
            pltpu.VMEM((span, DSA_WIDTH), BF16),
            pltpu.VMEM((kv_rows, DSA_WIDTH), BF16),
            pltpu.VMEM((kv_rows, DSA_WIDTH), BF16),
            pltpu.VMEM((span, LANES), F32),
        ] + ([pltpu.VMEM((4, span, LANES), F32)] if dilation > 4 else []),
        compiler_params=pltpu.CompilerParams(
            dimension_semantics=("parallel", "arbitrary"), vmem_limit_bytes=VMEM_LIMIT_BYTES),
        name=f"dsa_attn_d{dilation}",
    )(qkv, qkv, qkv)


def _silu(x):
    return x * (1.0 / (1.0 + jnp.exp(-x)))


def _sigmoid(x):
    return 1.0 / (1.0 + jnp.exp(-x))


def _out_proj_kernel(o0_ref, o1_ref, o2_ref, l0_ref, l1_ref, l2_ref, za_ref, ob_ref, zb_ref, ga_ref, gb_ref,
                     x_ref, wa_ref, wb_ref, wo_ref, fw_ref, ex_ref, out_ref):
    l0, l1, l2 = l0_ref[...], l1_ref[...], l2_ref[...]
    mx = jnp.maximum(jnp.maximum(l0, l1), l2)
    e0, e1, e2 = jnp.exp(l0 - mx), jnp.exp(l1 - mx), jnp.exp(l2 - mx)
    inv = 1.0 / (e0 + e1 + e2)

    def per_lane(w):
        return jnp.dot(w.astype(BF16), ex_ref[...], preferred_element_type=F32)

    o_a = (per_lane(e0 * inv) * o0_ref[...].astype(F32) + per_lane(e1 * inv) * o1_ref[...].astype(F32)
           + per_lane(e2 * inv) * o2_ref[...].astype(F32))
    a_in = (o_a * _silu(za_ref[...].astype(F32))).astype(BF16)
    b_in = (ob_ref[...].astype(F32) * _silu(zb_ref[...].astype(F32))).astype(BF16)
    y_a = jnp.dot(a_in, wa_ref[...], preferred_element_type=F32)
    y_b = jnp.dot(b_in, wb_ref[...], preferred_element_type=F32)
    merged = _sigmoid(ga_ref[...].astype(F32)) * y_a + _sigmoid(gb_ref[...].astype(F32)) * y_b
    y = x_ref[...] + jnp.dot(merged.astype(BF16), wo_ref[...], preferred_element_type=F32)
    ms = jnp.mean(y * y, axis=-1, keepdims=True)
    out_ref[...] = y * lax.rsqrt(ms + NORM_EPS) * fw_ref[...]


def _out_proj(o_groups, lse_groups, gates, o_b, x2, w_up_a, w_up_b, w_out, final_w):
    rows = x2.shape[0]
    tm = min(512, rows)
    row128 = pl.BlockSpec((tm, LANES), lambda i: (i, 0))
    row512 = lambda c: pl.BlockSpec((tm, 512), lambda i: (i, c))
    row1024 = lambda c: pl.BlockSpec((tm, 1024), lambda i: (i, c))
    whole = lambda a: pl.BlockSpec(a.shape, lambda i: (0,) * a.ndim)
    expand = jnp.asarray(np.arange(LANES)[:, None] == LSE_REP * (np.arange(DSA_WIDTH)[None, :] // DSA_HEAD_DIM), BF16)
    return pl.pallas_call(
        _out_proj_kernel,
        grid=(rows // tm,),
        in_specs=[row512(0)] * 3 + [row128] * 3 + [
            row512(GATE_DSA_Z), row512(0), row512(GATE_GDN_Z), row1024(GATE_A), row1024(GATE_B), row1024(0),
            whole(w_up_a), whole(w_up_b), whole(w_out), whole(final_w), whole(expand)],
        out_specs=pl.BlockSpec((tm, D_MODEL), lambda i: (i, 0)),
        out_shape=jax.ShapeDtypeStruct((rows, D_MODEL), F32),
        compiler_params=pltpu.CompilerParams(
            dimension_semantics=("parallel",), vmem_limit_bytes=VMEM_LIMIT_BYTES),
        name="out_proj",
    )(*o_groups, *lse_groups, gates, o_b, gates, gates, gates, x2, w_up_a, w_up_b, w_out, final_w, expand)


QUAD = 4 * GDN_DIM
GDN_GROUP = 16


def _split_hi_lo(x):
    hi = x.astype(BF16)
    lo = (x - hi.astype(F32)).astype(BF16)
    return hi, lo


def _softplus(x):
    return jnp.maximum(x, 0.0) + jnp.log1p(jnp.exp(-jnp.abs(x)))


def _bmm(a, b):
    return lax.dot_general(a, b, (((2,), (1,)), ((0,), (0,))), preferred_element_type=F32)


def _bmm_nt(a, b):
    return lax.dot_general(a, b, (((2,), (2,)), ((0,), (0,))), preferred_element_type=F32)


def _block_diag(x, bd_mask):
    t = jnp.concatenate([x, x, x, x], axis=1)
    return jnp.where(bd_mask, t, jnp.zeros_like(t))


def _gdn_kernel(q_ref, k_ref, v_ref, lg_ref, alog_ref, dtb_ref, nw_ref, e512_ref, eb_ref, eg_ref, tri_ref, o_ref,
                beta_ref, gh_ref, gl_ref, lhs_ref, oloc_ref, snew_ref, dec_ref, oraw_ref, state_ref, *, tg):
    first = pl.program_id(1) == 0
    nc = tg // GDN_CHUNK
    nb = 2 * nc

    @pl.when(first)
    def _():
        state_ref[...] = jnp.zeros_like(state_ref)

    logits = lg_ref[...]
    beta_hi, beta_lo = _split_hi_lo(_sigmoid(logits))
    g_hi, g_lo = _split_hi_lo(-jnp.exp(alog_ref[...]) * _softplus(logits + dtb_ref[...]))
    beta_ref[...] = (jnp.dot(beta_hi, eb_ref[...], preferred_element_type=F32)
                     + jnp.dot(beta_lo, eb_ref[...], preferred_element_type=F32))
    gh_ref[...] = jnp.dot(g_hi, eg_ref[...], preferred_element_type=F32).astype(BF16)
    gl_ref[...] = jnp.dot(g_lo, eg_ref[...], preferred_element_type=F32).astype(BF16)

    row = lax.broadcasted_iota(jnp.int32, (GDN_CHUNK, QUAD), 0)
    col = lax.broadcasted_iota(jnp.int32, (GDN_CHUNK, QUAD), 1) % GDN_CHUNK
    incl = row >= col
    strict = row > col
    eye = (row == col).astype(F32)
    bd_mask = (lax.broadcasted_iota(jnp.int32, (QUAD, QUAD), 0) // GDN_DIM
               == lax.broadcasted_iota(jnp.int32, (QUAD, QUAD), 1) // GDN_DIM)
    tri = tri_ref[...]

    def chunk_quads(ref, c0, n, dtype):
        a = ref[pl.ds(pl.multiple_of(c0 * GDN_CHUNK, GDN_CHUNK), n * GDN_CHUNK), :]
        a = a.reshape(n, GDN_CHUNK, GDN_WIDTH).astype(dtype)
        return jnp.stack([a[:, :, :QUAD], a[:, :, QUAD:]], axis=1).reshape(2 * n, GDN_CHUNK, QUAD)

    n_per = GDN_GROUP // 2

    def local_pass(it, carry):
        c0 = it * n_per
        q, k, v = (chunk_quads(r, c0, n_per, F32) for r in (q_ref, k_ref, v_ref))
        beta = chunk_quads(beta_ref, c0, n_per, F32)
        gh, gl = chunk_quads(gh_ref, c0, n_per, BF16), chunk_quads(gl_ref, c0, n_per, BF16)

        zero = jnp.zeros_like(gh)
        rhs_hi = jnp.concatenate([gh, jnp.where(strict, gh, zero)], axis=-1)
        rhs_lo = jnp.concatenate([gl, jnp.where(strict, gl, zero)], axis=-1)
        tri_b = jnp.broadcast_to(tri, (GDN_GROUP, GDN_CHUNK, GDN_CHUNK))
        gd = _bmm(tri_b, rhs_hi) + _bmm(tri_b, rhs_lo)
        g_cum, d_pair = gd[:, :, :QUAD], gd[:, :, QUAD:]
        decay_incl = jnp.where(incl, jnp.exp(d_pair), 0.0)
        decay_strict = jnp.where(strict, decay_incl, 0.0)
        exp_g = jnp.exp(g_cum)
        g_last = g_cum[:, GDN_CHUNK - 1:GDN_CHUNK, :]
        k_dec = k * jnp.exp(g_last - g_cum)
        k_beta = k * beta

        k_bd = _block_diag(k.astype(BF16), bd_mask)
        aa = _bmm_nt(jnp.concatenate([k_beta, q], axis=1).astype(BF16), k_bd)
        a = aa[:, :GDN_CHUNK] * decay_strict
        attn = aa[:, GDN_CHUNK:] * decay_incl

        p = a
        t = eye - a
        p = _bmm(p.astype(BF16), _block_diag(p.astype(BF16), bd_mask))
        for _ in range(4):
            r = _bmm(jnp.concatenate([p, t], axis=1).astype(BF16), _block_diag(p.astype(BF16), bd_mask))
            p = r[:, :GDN_CHUNK]
            t = t + r[:, GDN_CHUNK:]
        t = t + _bmm(t.astype(BF16), _block_diag(p.astype(BF16), bd_mask))
        t16 = t.astype(BF16)
        u = _bmm(t16, _block_diag((v * beta).astype(BF16), bd_mask))
        w = _bmm(t16, _block_diag((k_beta * exp_g).astype(BF16), bd_mask))

        w16, u16 = w.astype(BF16), u.astype(BF16)
        kt = _bmm(jnp.swapaxes(k_dec, 1, 2).astype(BF16), jnp.concatenate([w16, u16], axis=-1))
        ao = _bmm(attn.astype(BF16),
                  jnp.concatenate([_block_diag(w16, bd_mask), _block_diag(u16, bd_mask)], axis=-1))
        sl = pl.ds(pl.multiple_of(it * GDN_GROUP, GDN_GROUP), GDN_GROUP)
        lhs_ref[sl] = jnp.concatenate(
            [q * exp_g - ao[:, :, :QUAD], jnp.where(bd_mask, -kt[:, :, :QUAD], 0.0)], axis=1).astype(BF16)
        oloc_ref[sl] = ao[:, :, QUAD:]
        snew_ref[sl] = jnp.where(bd_mask, kt[:, :, QUAD:], 0.0)
        dec_ref[sl] = jnp.broadcast_to(jnp.exp(g_last), (GDN_GROUP, 8, QUAD))
        return carry

    lax.fori_loop(0, nb // GDN_GROUP, local_pass, 0)

    def scan_step(c, carry):
        sl = pl.ds(pl.multiple_of(2 * c, 2), 2)
        state = state_ref[...]
        r = _bmm(lhs_ref[sl], state.astype(BF16))
        state_ref[...] = state * dec_ref[sl][:, 0:1, :] + r[:, GDN_CHUNK:] + snew_ref[sl]
        o = r[:, :GDN_CHUNK] + oloc_ref[sl]
        rows = pl.ds(pl.multiple_of(c * GDN_CHUNK, GDN_CHUNK), GDN_CHUNK)
        oraw_ref[rows, 0:QUAD] = o[0]
        oraw_ref[rows, QUAD:] = o[1]
        return carry

    lax.fori_loop(0, nc, scan_step, 0)

    o_all = oraw_ref[...]
    ms = jnp.dot((o_all * o_all).astype(BF16), e512_ref[...], preferred_element_type=F32) * (1.0 / GDN_DIM)
    o_ref[...] = (o_all * lax.rsqrt(ms + NORM_EPS) * nw_ref[...]).astype(o_ref.dtype)


def _gdn(gdn_qkv, logits, alog_row, dtb_row, norm_row, batch, seq):
    rows = gdn_qkv.shape[0]
    tg = min(512, seq)
    steps = seq // tg
    nb = 2 * (tg // GDN_CHUNK)
    head_of_lane = np.arange(GDN_WIDTH) // GDN_DIM
    e512 = jnp.asarray(head_of_lane[:, None] == head_of_lane[None, :], BF16)
    e_beta = jnp.asarray(np.arange(LOGIT_PAD)[:, None] == head_of_lane[None, :], BF16)
    e_g = jnp.asarray(np.arange(LOGIT_PAD)[:, None] == head_of_lane[None, :] + GDN_HEADS, BF16)
    tri = jnp.asarray(np.tril(np.ones((GDN_CHUNK, GDN_CHUNK))), BF16)

    cur = lambda c: pl.BlockSpec((tg, 512), lambda b, n: (b * steps + n, c))
    whole = lambda a: pl.BlockSpec(a.shape, lambda b, n: (0,) * a.ndim)
    consts = (alog_row, dtb_row, norm_row, e512, e_beta, e_g, tri)
    return pl.pallas_call(
        functools.partial(_gdn_kernel, tg=tg),
        grid=(batch, steps),
        in_specs=[cur(0), cur(1), cur(2),
                  pl.BlockSpec((tg, LOGIT_PAD), lambda b, n: (b * steps + n, 0))] + [whole(a) for a in consts],
        out_specs=pl.BlockSpec((tg, GDN_WIDTH), lambda b, n: (b * steps + n, 0)),
        out_shape=jax.ShapeDtypeStruct((rows, GDN_WIDTH), BF16),
        scratch_shapes=[
            pltpu.VMEM((tg, GDN_WIDTH), F32),
            pltpu.VMEM((tg, GDN_WIDTH), BF16),
            pltpu.VMEM((tg, GDN_WIDTH), BF16),
            pltpu.VMEM((nb, GDN_CHUNK + QUAD, QUAD), BF16),
            pltpu.VMEM((nb, GDN_CHUNK, QUAD), F32),
            pltpu.VMEM((nb, QUAD, QUAD), F32),
            pltpu.VMEM((nb, 8, QUAD), F32),
            pltpu.VMEM((tg, GDN_WIDTH), F32),
            pltpu.VMEM((2, QUAD, QUAD), F32),
        ],
        compiler_params=pltpu.CompilerParams(
            dimension_semantics=("parallel", "arbitrary"), vmem_limit_bytes=VMEM_LIMIT_BYTES),
        name="gdn",
    )(gdn_qkv, gdn_qkv, gdn_qkv, logits, *consts)


def _rope_tables(seq):
    inv_freq = np.float32(ROPE_THETA) ** (-np.arange(0, DSA_HEAD_DIM, 2, dtype=np.float32) / np.float32(DSA_HEAD_DIM))
    ang = np.arange(seq, dtype=np.float32)[:, None] * inv_freq[None, :]
    cos, sin = np.cos(ang), np.sin(ang)
    return (jnp.asarray(np.concatenate([cos, cos, cos, cos], axis=-1), F32),
            jnp.asarray(np.concatenate([-sin, -sin, sin, sin], axis=-1), F32))


def _pair_halves_layout(w_qkv):
    half = DSA_HEAD_DIM // 2
    w6 = w_qkv.reshape(D_MODEL, len(DSA_PATTERNS), 3, DSA_HEADS // 2, 2, 2, half)
    qk = jnp.swapaxes(w6[:, :, :2], 4, 5)
    return jnp.concatenate([qk, w6[:, :, 2:]], axis=2).reshape(D_MODEL, QKV_COLS)


def kernel(x, norm_w, w_in, conv_w, a_log, dt_bias, gdn_norm_w, w_up_a, w_up_b, w_out, final_norm_w):
    batch, seq, _ = x.shape
    assert norm_w.shape[0] == 1, "the final RMSNorm is fused into the (single) layer's output kernel"
    cos_t, sin_t = _rope_tables(seq)
    x2 = x.reshape(batch * seq, D_MODEL)
    w = w_in[0].astype(BF16)
    gates = LOGIT_START + 2 * GDN_HEADS
    dsa_z, gdn_z = QKV_COLS, GDN_IN_START + GROUP_COLS
    w_gates = jnp.concatenate(
        [w[:, gates:], w[:, dsa_z:GDN_IN_START], w[:, gdn_z:LOGIT_START],
         w[:, LOGIT_START:gates], jnp.zeros((D_MODEL, LOGIT_PAD - 2 * GDN_HEADS), BF16)], axis=1)
    qkv, gdn_qkv, gate_cols, logits = _in_proj(
        x2, norm_w[0][None, :], _pair_halves_layout(w[:, :QKV_COLS]), w, w_gates, conv_w[0], cos_t, sin_t, seq)

    o_groups, lse_groups = [], []
    for g, (_, dilation) in enumerate(DSA_PATTERNS):
        o_g, lse_g = _dsa_attention(qkv, g, dilation, batch, seq)
        o_groups.append(o_g)
        lse_groups.append(lse_g)

    pad8 = lambda p: jnp.pad(p.astype(F32), (GDN_HEADS, LOGIT_PAD - 2 * GDN_HEADS))[None, :]
    o_b = _gdn(gdn_qkv, logits, pad8(a_log[0]), pad8(dt_bias[0]),
               jnp.tile(gdn_norm_w[0].astype(F32), GDN_HEADS)[None, :], batch, seq)

    out = _out_proj(o_groups, lse_groups, gate_cols, o_b, x2,
                    w_up_a[0].astype(BF16), w_up_b[0].astype(BF16), w_out[0].astype(BF16),
                    final_norm_w[None, :])
    return out.reshape(batch, seq, D_MODEL)
```

```python
import functools

import numpy as np
import jax
import jax.numpy as jnp
from jax import lax
from jax.experimental import pallas as pl
from jax.experimental.pallas import tpu as pltpu

F32 = jnp.float32
BF16 = jnp.bfloat16

D_MODEL = 1024
DSA_PATTERNS = ((128, 1), (512, 4), (2048, 16))
DSA_HEADS = 8
DSA_HEAD_DIM = 64
DSA_WIDTH = DSA_HEADS * DSA_HEAD_DIM
DSA_BLOCK = 128
ROPE_THETA = 10000.0
GDN_HEADS = 8
GDN_DIM = 64
GDN_WIDTH = GDN_HEADS * GDN_DIM
GDN_CONV = 4
GDN_CHUNK = 64
NORM_EPS = 1e-6

QKV_COLS = 3 * 3 * DSA_WIDTH
GROUP_COLS = 3 * DSA_WIDTH
GATE_COLS = 3072
GATE_A, GATE_B = 0, 1
GATE_DSA_Z, GATE_GDN_Z = 4, 5
GDN_IN_START = QKV_COLS + DSA_WIDTH
LOGIT_START = 7168
LOGIT_PAD = 128
PROJ_ROWS = 512
LOG2_E = 1.4426950408889634
LN_2 = 0.6931471805599453
DSA_Q_SCALE = DSA_HEAD_DIM ** -0.5 * LOG2_E

VMEM_LIMIT_BYTES = 56 * 1024 * 1024
LANES = 128
NEG_BIG = -1e30


def _lane_iota(shape):
    return lax.broadcasted_iota(jnp.int32, shape, len(shape) - 1)


def _in_proj_qkv_kernel(x_ref, nw_ref, w_ref, cos_ref, sin_ref, qkv_ref, h_ref):
    x = x_ref[...]
    ms = jnp.mean(x * x, axis=-1, keepdims=True)
    h = (x * lax.rsqrt(ms + NORM_EPS) * nw_ref[...]).astype(BF16)
    h_ref[...] = h
    cos = cos_ref[...]
    sin = sin_ref[...]
    low_half = (_lane_iota(cos.shape) % DSA_HEAD_DIM) < (DSA_HEAD_DIM // 2)
    for g in range(len(DSA_PATTERNS)):
        c0 = g * GROUP_COLS
        acc = jnp.dot(h, w_ref[:, c0:c0 + GROUP_COLS], preferred_element_type=F32)
        for c in range(8):
            t = acc[:, c * LANES:(c + 1) * LANES]
            rot = jnp.where(low_half, pltpu.roll(t, LANES - 32, 1), pltpu.roll(t, 32, 1))
            r = t * cos + rot * sin
            if c < 4:
                r = r * DSA_Q_SCALE
            qkv_ref[:, c0 + c * LANES:c0 + (c + 1) * LANES] = r.astype(BF16)
        qkv_ref[:, c0 + 8 * LANES:c0 + GROUP_COLS] = acc[:, 8 * LANES:].astype(BF16)


def _in_proj_rest_kernel(h_ref, wgq_ref, wgk_ref, wgv_ref, w_ref, wl_ref, cw_ref, e512_ref,
                         gdn_ref, gates_ref, logit_ref, xs_ref, *, tiles_per_seq):
    tm = h_ref.shape[0]
    first = pl.program_id(0) % tiles_per_seq == 0

    @pl.when(first)
    def _():
        xs_ref[0:8, :] = jnp.zeros((8, GROUP_COLS), F32)

    @pl.when(jnp.logical_not(first))
    def _():
        xs_ref[0:8, :] = xs_ref[tm:tm + 8, :]

    h = h_ref[...]
    logit_ref[...] = jnp.dot(h, wl_ref[...], preferred_element_type=F32)
    for part, wg_ref in enumerate((wgq_ref, wgk_ref, wgv_ref)):
        xs_ref[8:8 + tm, part * GDN_WIDTH:(part + 1) * GDN_WIDTH] = jnp.dot(
            h, wg_ref[...], preferred_element_type=F32)
    for g in range(GATE_COLS // GROUP_COLS):
        c0 = g * GROUP_COLS
        gates_ref[:, c0:c0 + GROUP_COLS] = jnp.dot(
            h, w_ref[:, c0:c0 + GROUP_COLS], preferred_element_type=F32).astype(BF16)
    for part in range(3):
        cols = slice(part * GDN_WIDTH, (part + 1) * GDN_WIDTH)
        w = cw_ref[:, cols]
        y = w[0:1, :] * xs_ref[5:5 + tm, cols]
        for j in range(1, GDN_CONV):
            y = y + w[j:j + 1, :] * xs_ref[5 + j:5 + j + tm, cols]
        y = _silu(y)
        if part < 2:
            ss = jnp.dot((y * y).astype(BF16), e512_ref[...], preferred_element_type=F32)
            y = y * (lax.rsqrt(ss + NORM_EPS) * (GDN_DIM ** -0.5 if part == 0 else 1.0))
        gdn_ref[:, cols] = y.astype(BF16)


def _resident(shape, col_block=0):
    return pl.BlockSpec(shape, lambda i: (0,) * (len(shape) - 1) + (col_block,), pipeline_mode=pl.Buffered(1))


def _in_proj(x2, norm_w, w_bf16, w_gates, conv_w, cos_t, sin_t, seq):
    assert GDN_IN_START % GDN_WIDTH == 0 and GATE_COLS % LOGIT_PAD == 0
    rows = x2.shape[0]
    tm = min(PROJ_ROWS, seq)
    n_seq_tiles = seq // tm
    row = lambda width: pl.BlockSpec((tm, width), lambda i: (i, 0))
    table = pl.BlockSpec((tm, LANES), lambda i: (i % n_seq_tiles, 0))
    head_of_lane = np.arange(GDN_WIDTH) // GDN_DIM
    e512 = jnp.asarray(head_of_lane[:, None] == head_of_lane[None, :], BF16)
    gdn_block = GDN_IN_START // GDN_WIDTH
    qkv, h = pl.pallas_call(
        _in_proj_qkv_kernel,
        grid=(rows // tm,),
        in_specs=[row(D_MODEL), _resident((1, D_MODEL)), _resident((D_MODEL, QKV_COLS)), table, table],
        out_specs=[row(QKV_COLS), row(D_MODEL)],
        out_shape=[jax.ShapeDtypeStruct((rows, QKV_COLS), BF16), jax.ShapeDtypeStruct((rows, D_MODEL), BF16)],
        compiler_params=pltpu.CompilerParams(dimension_semantics=("parallel",), vmem_limit_bytes=VMEM_LIMIT_BYTES),
        name="in_proj_qkv",
    )(x2, norm_w, w_bf16, cos_t, sin_t)
    gdn_qkv, gates, logits = pl.pallas_call(
        functools.partial(_in_proj_rest_kernel, tiles_per_seq=n_seq_tiles),
        grid=(rows // tm,),
        in_specs=[row(D_MODEL), _resident((D_MODEL, GDN_WIDTH), gdn_block),
                  _resident((D_MODEL, GDN_WIDTH), gdn_block + 1), _resident((D_MODEL, GDN_WIDTH), gdn_block + 2),
                  _resident((D_MODEL, GATE_COLS)), _resident((D_MODEL, LOGIT_PAD), GATE_COLS // LOGIT_PAD),
                  _resident(conv_w.shape), _resident(e512.shape)],
        out_specs=[row(GROUP_COLS), row(GATE_COLS), row(LOGIT_PAD)],
        out_shape=[jax.ShapeDtypeStruct((rows, GROUP_COLS), BF16), jax.ShapeDtypeStruct((rows, GATE_COLS), BF16),
                   jax.ShapeDtypeStruct((rows, LOGIT_PAD), F32)],
        scratch_shapes=[pltpu.VMEM((tm + 8, GROUP_COLS), F32)],
        compiler_params=pltpu.CompilerParams(
            dimension_semantics=("arbitrary",), vmem_limit_bytes=VMEM_LIMIT_BYTES),
        name="in_proj_rest",
    )(h, w_bf16, w_bf16, w_bf16, w_gates, w_gates, conv_w, e512)
    return qkv, gdn_qkv, gates, logits


DSA_SPAN = 2048
LSE_REP = LANES // DSA_HEADS


def _dsa_kernel(q_ref, k_ref, v_ref, o_ref, lse_ref, stage_ref, qp_ref, kp_ref, vp_ref, lstage_ref, tmp_ref=None,
                *, d, span):
    first = pl.program_id(1) == 0
    npr = span // d
    n_sub = npr // DSA_BLOCK
    kstride = npr + DSA_BLOCK

    def regroup(src_ref, dst_ref, dst_stride, dst_off):
        if d == 1:
            dst_ref[dst_off:dst_off + npr, :] = src_ref[...]
            return
        for c in range(4):
            stage_ref[c] = src_ref[:, c * LANES:(c + 1) * LANES].astype(F32)
        if d > 4:
            quarter = span // 4
            for c in range(4):
                for r4 in range(4):
                    tmp_ref[c, r4 * quarter:(r4 + 1) * quarter, :] = stage_ref[c, pl.ds(r4, quarter, stride=4), :]
            slabs, stride, start = tmp_ref, d // 4, lambda r: (r % 4) * quarter + r // 4
        else:
            slabs, stride, start = stage_ref, d, lambda r: r
        for c in range(4):
            for r in range(d):
                lo = r * dst_stride + dst_off
                dst_ref[lo:lo + npr, c * LANES:(c + 1) * LANES] = (
                    slabs[c, pl.ds(start(r), npr, stride=stride), :].astype(BF16))

    @pl.when(first)
    def _():
        for r in range(d):
            kp_ref[r * kstride:r * kstride + DSA_BLOCK, :] = jnp.zeros((DSA_BLOCK, DSA_WIDTH), BF16)
            vp_ref[r * kstride:r * kstride + DSA_BLOCK, :] = jnp.zeros((DSA_BLOCK, DSA_WIDTH), BF16)

    regroup(q_ref, qp_ref, npr, 0)
    regroup(k_ref, kp_ref, kstride, DSA_BLOCK)
    regroup(v_ref, vp_ref, kstride, DSA_BLOCK)

    qi = lax.broadcasted_iota(jnp.int32, (DSA_BLOCK, 2 * DSA_BLOCK), 0)
    kj = lax.broadcasted_iota(jnp.int32, (DSA_BLOCK, 2 * DSA_BLOCK), 1)
    band = (kj >= qi) & (kj <= qi + DSA_BLOCK)
    head_a = _lane_iota((DSA_BLOCK, LANES)) < DSA_HEAD_DIM
    lse_head = _lane_iota((DSA_BLOCK, LANES)) // LSE_REP

    def block(blk, carry):
        r, i = blk // n_sub, blk % n_sub
        q0 = pl.multiple_of(r * npr + i * DSA_BLOCK, DSA_BLOCK)
        k0 = pl.multiple_of(r * kstride + i * DSA_BLOCK, DSA_BLOCK)
        nat0 = r + d * DSA_BLOCK * i
        mask = band & (kj >= jnp.where(first & (i == 0), DSA_BLOCK, 0))
        m_tile = jnp.zeros((DSA_BLOCK, LANES), F32)
        den_tile = jnp.ones((DSA_BLOCK, LANES), F32)
        for hp in range(DSA_HEADS // 2):
            cols = slice(hp * LANES, (hp + 1) * LANES)
            q2 = qp_ref[pl.ds(q0, DSA_BLOCK), cols]
            k2 = kp_ref[pl.ds(k0, 2 * DSA_BLOCK), cols]
            v2 = vp_ref[pl.ds(k0, 2 * DSA_BLOCK), cols]
            pvs, ms, dens = [], [], []
            for is_a in (True, False):
                sel = head_a if is_a else jnp.logical_not(head_a)
                qh = jnp.where(sel, q2, jnp.zeros_like(q2))
                s = lax.dot_general(qh, k2, (((1,), (1,)), ((), ())), preferred_element_type=F32)
                s = jnp.where(mask, s, NEG_BIG)
                m = jnp.max(s, axis=-1, keepdims=True)
                p = jnp.exp2(s - m)
                dens.append(jnp.sum(p, axis=-1, keepdims=True))
                pvs.append(jnp.dot(p.astype(BF16), v2, preferred_element_type=F32))
                ms.append(m)
            o_pair = jnp.where(head_a, pvs[0], pvs[1]) / jnp.where(head_a, dens[0], dens[1])
            den_tile = jnp.where(lse_head == 2 * hp, dens[0], jnp.where(lse_head == 2 * hp + 1, dens[1], den_tile))
            m_tile = jnp.where(lse_head == 2 * hp, ms[0], jnp.where(lse_head == 2 * hp + 1, ms[1], m_tile))
            if d == 1:
                o_ref[pl.ds(q0, DSA_BLOCK), cols] = o_pair.astype(o_ref.dtype)
            else:
                stage_ref[hp, pl.ds(nat0, DSA_BLOCK, stride=d), :] = o_pair
        lse_tile = m_tile * LN_2 + jnp.log(den_tile)
        if d == 1:
            lse_ref[pl.ds(q0, DSA_BLOCK), :] = lse_tile
        else:
            lstage_ref[pl.ds(nat0, DSA_BLOCK, stride=d), :] = lse_tile
        return carry

    lax.fori_loop(0, d * n_sub, block, 0, unroll=16)

    for r in range(d):
        lo = r * kstride
        kp_ref[lo:lo + DSA_BLOCK, :] = kp_ref[lo + npr:lo + npr + DSA_BLOCK, :]
        vp_ref[lo:lo + DSA_BLOCK, :] = vp_ref[lo + npr:lo + npr + DSA_BLOCK, :]
    if d > 1:
        for c in range(4):
            o_ref[:, c * LANES:(c + 1) * LANES] = stage_ref[c].astype(o_ref.dtype)
        lse_ref[...] = lstage_ref[...]


def _dsa_attention(qkv, group, dilation, batch, seq):
    span = min(DSA_SPAN, seq)
    steps = seq // span
    kv_rows = span + dilation * DSA_BLOCK
    col = lambda t: pl.BlockSpec((span, 512), lambda b, n: (b * steps + n, group * 3 + t))
    row = lambda width: pl.BlockSpec((span, width), lambda b, n: (b * steps + n, 0))
    return pl.pallas_call(
        functools.partial(_dsa_kernel, d=dilation, span=span),
        grid=(batch, steps),
        in_specs=[col(0), col(1), col(2)],
        out_specs=[row(DSA_WIDTH), row(LANES)],
        out_shape=[
            jax.ShapeDtypeStruct((batch * seq, DSA_WIDTH), BF16),
            jax.ShapeDtypeStruct((batch * seq, LANES), F32),
        ],
        scratch_shapes=[
            pltpu.VMEM((4, span, LANES), F32),
            pltpu.VMEM((span, DSA_WIDTH), BF16),
            pltpu.VMEM((kv_rows, DSA_WIDTH), BF16),
            pltpu.VMEM((kv_rows, DSA_WIDTH), BF16),
            pltpu.VMEM((span, LANES), F32),
        ] + ([pltpu.VMEM((4, span, LANES), F32)] if dilation > 4 else []),
        compiler_params=pltpu.CompilerParams(
            dimension_semantics=("parallel", "arbitrary"), vmem_limit_bytes=VMEM_LIMIT_BYTES),
        name=f"dsa_attn_d{dilation}",
    )(qkv, qkv, qkv)


def _silu(x):
    return x * (1.0 / (1.0 + jnp.exp(-x)))


def _sigmoid(x):
    return 1.0 / (1.0 + jnp.exp(-x))


def _out_proj_kernel(o0_ref, o1_ref, o2_ref, l0_ref, l1_ref, l2_ref, za_ref, ob_ref, zb_ref, ga_ref, gb_ref,
                     x_ref, wa_ref, wb_ref, wo_ref, fw_ref, ex_ref, out_ref):
    l0, l1, l2 = l0_ref[...], l1_ref[...], l2_ref[...]
    mx = jnp.maximum(jnp.maximum(l0, l1), l2)
    e0, e1, e2 = jnp.exp(l0 - mx), jnp.exp(l1 - mx), jnp.exp(l2 - mx)
    inv = 1.0 / (e0 + e1 + e2)

    def per_lane(w):
        return jnp.dot(w.astype(BF16), ex_ref[...], preferred_element_type=F32)

    o_a = (per_lane(e0 * inv) * o0_ref[...].astype(F32) + per_lane(e1 * inv) * o1_ref[...].astype(F32)
           + per_lane(e2 * inv) * o2_ref[...].astype(F32))
    a_in = (o_a * _silu(za_ref[...].astype(F32))).astype(BF16)
    b_in = (ob_ref[...].astype(F32) * _silu(zb_ref[...].astype(F32))).astype(BF16)
    y_a = jnp.dot(a_in, wa_ref[...], preferred_element_type=F32)
    y_b = jnp.dot(b_in, wb_ref[...], preferred_element_type=F32)
    merged = _sigmoid(ga_ref[...].astype(F32)) * y_a + _sigmoid(gb_ref[...].astype(F32)) * y_b
    y = x_ref[...] + jnp.dot(merged.astype(BF16), wo_ref[...], preferred_element_type=F32)
    ms = jnp.mean(y * y, axis=-1, keepdims=True)
    out_ref[...] = y * lax.rsqrt(ms + NORM_EPS) * fw_ref[...]


def _out_proj(o_groups, lse_groups, gates, o_b, x2, w_up_a, w_up_b, w_out, final_w):
    rows = x2.shape[0]
    tm = min(512, rows)
    row128 = pl.BlockSpec((tm, LANES), lambda i: (i, 0))
    row512 = lambda c: pl.BlockSpec((tm, 512), lambda i: (i, c))
    row1024 = lambda c: pl.BlockSpec((tm, 1024), lambda i: (i, c))
    whole = lambda a: pl.BlockSpec(a.shape, lambda i: (0,) * a.ndim)
    expand = jnp.asarray(np.arange(LANES)[:, None] == LSE_REP * (np.arange(DSA_WIDTH)[None, :] // DSA_HEAD_DIM), BF16)
    return pl.pallas_call(
        _out_proj_kernel,
        grid=(rows // tm,),
        in_specs=[row512(0)] * 3 + [row128] * 3 + [
            row512(GATE_DSA_Z), row512(0), row512(GATE_GDN_Z), row1024(GATE_A), row1024(GATE_B), row1024(0),
            whole(w_up_a), whole(w_up_b), whole(w_out), whole(final_w), whole(expand)],
        out_specs=pl.BlockSpec((tm, D_MODEL), lambda i: (i, 0)),
        out_shape=jax.ShapeDtypeStruct((rows, D_MODEL), F32),
        compiler_params=pltpu.CompilerParams(
            dimension_semantics=("parallel",), vmem_limit_bytes=VMEM_LIMIT_BYTES),
        name="out_proj",
    )(*o_groups, *lse_groups, gates, o_b, gates, gates, gates, x2, w_up_a, w_up_b, w_out, final_w, expand)


QUAD = 4 * GDN_DIM
GDN_GROUP = 16


def _split_hi_lo(x):
    hi = x.astype(BF16)
    lo = (x - hi.astype(F32)).astype(BF16)
    return hi, lo


def _softplus(x):
    return jnp.maximum(x, 0.0) + jnp.log1p(jnp.exp(-jnp.abs(x)))


def _bmm(a, b):
    return lax.dot_general(a, b, (((2,), (1,)), ((0,), (0,))), preferred_element_type=F32)


def _bmm_nt(a, b):
    return lax.dot_general(a, b, (((2,), (2,)), ((0,), (0,))), preferred_element_type=F32)


def _block_diag(x, bd_mask):
    t = jnp.concatenate([x, x, x, x], axis=1)
    return jnp.where(bd_mask, t, jnp.zeros_like(t))


def _gdn_kernel(q_ref, k_ref, v_ref, lg_ref, alog_ref, dtb_ref, nw_ref, e512_ref, eb_ref, eg_ref, tri_ref, o_ref,
                beta_ref, gh_ref, gl_ref, lhs_ref, oloc_ref, snew_ref, dec_ref, oraw_ref, state_ref, *, tg):
    first = pl.program_id(1) == 0
    nc = tg // GDN_CHUNK
    nb = 2 * nc

    @pl.when(first)
    def _():
        state_ref[...] = jnp.zeros_like(state_ref)

    logits = lg_ref[...]
    beta_hi, beta_lo = _split_hi_lo(_sigmoid(logits))
    g_hi, g_lo = _split_hi_lo(-jnp.exp(alog_ref[...]) * _softplus(logits + dtb_ref[...]))
    beta_ref[...] = (jnp.dot(beta_hi, eb_ref[...], preferred_element_type=F32)
                     + jnp.dot(beta_lo, eb_ref[...], preferred_element_type=F32))
    gh_ref[...] = jnp.dot(g_hi, eg_ref[...], preferred_element_type=F32).astype(BF16)
    gl_ref[...] = jnp.dot(g_lo, eg_ref[...], preferred_element_type=F32).astype(BF16)

    row = lax.broadcasted_iota(jnp.int32, (GDN_CHUNK, QUAD), 0)
    col = lax.broadcasted_iota(jnp.int32, (GDN_CHUNK, QUAD), 1) % GDN_CHUNK
    incl = row >= col
    strict = row > col
    eye = (row == col).astype(F32)
    bd_mask = (lax.broadcasted_iota(jnp.int32, (QUAD, QUAD), 0) // GDN_DIM
               == lax.broadcasted_iota(jnp.int32, (QUAD, QUAD), 1) // GDN_DIM)
    tri = tri_ref[...]

    def chunk_quads(ref, c0, n, dtype):
        a = ref[pl.ds(pl.multiple_of(c0 * GDN_CHUNK, GDN_CHUNK), n * GDN_CHUNK), :]
        a = a.reshape(n, GDN_CHUNK, GDN_WIDTH).astype(dtype)
        return jnp.stack([a[:, :, :QUAD], a[:, :, QUAD:]], axis=1).reshape(2 * n, GDN_CHUNK, QUAD)

    n_per = GDN_GROUP // 2

    def local_pass(it, carry):
        c0 = it * n_per
        q, k, v = (chunk_quads(r, c0, n_per, F32) for r in (q_ref, k_ref, v_ref))
        beta = chunk_quads(beta_ref, c0, n_per, F32)
        gh, gl = chunk_quads(gh_ref, c0, n_per, BF16), chunk_quads(gl_ref, c0, n_per, BF16)

        zero = jnp.zeros_like(gh)
        rhs_hi = jnp.concatenate([gh, jnp.where(strict, gh, zero)], axis=-1)
        rhs_lo = jnp.concatenate([gl, jnp.where(strict, gl, zero)], axis=-1)
        tri_b = jnp.broadcast_to(tri, (GDN_GROUP, GDN_CHUNK, GDN_CHUNK))
        gd = _bmm(tri_b, rhs_hi) + _bmm(tri_b, rhs_lo)
        g_cum, d_pair = gd[:, :, :QUAD], gd[:, :, QUAD:]
        decay_incl = jnp.where(incl, jnp.exp(d_pair), 0.0)
        decay_strict = jnp.where(strict, decay_incl, 0.0)
        exp_g = jnp.exp(g_cum)
        g_last = g_cum[:, GDN_CHUNK - 1:GDN_CHUNK, :]
        k_dec = k * jnp.exp(g_last - g_cum)
        k_beta = k * beta

        k_bd = _block_diag(k.astype(BF16), bd_mask)
        aa = _bmm_nt(jnp.concatenate([k_beta, q], axis=1).astype(BF16), k_bd)
        a = aa[:, :GDN_CHUNK] * decay_strict
        attn = aa[:, GDN_CHUNK:] * decay_incl

        p = a
        t = eye - a
        p = _bmm(p.astype(BF16), _block_diag(p.astype(BF16), bd_mask))
        for _ in range(4):
            r = _bmm(jnp.concatenate([p, t], axis=1).astype(BF16), _block_diag(p.astype(BF16), bd_mask))
            p = r[:, :GDN_CHUNK]
            t = t + r[:, GDN_CHUNK:]
        t = t + _bmm(t.astype(BF16), _block_diag(p.astype(BF16), bd_mask))
        t16 = t.astype(BF16)
        u = _bmm(t16, _block_diag((v * beta).astype(BF16), bd_mask))
        w = _bmm(t16, _block_diag((k_beta * exp_g).astype(BF16), bd_mask))

        w16, u16 = w.astype(BF16), u.astype(BF16)
        kt = _bmm(jnp.swapaxes(k_dec, 1, 2).astype(BF16), jnp.concatenate([w16, u16], axis=-1))
        ao = _bmm(attn.astype(BF16),
                  jnp.concatenate([_block_diag(w16, bd_mask), _block_diag(u16, bd_mask)], axis=-1))
        sl = pl.ds(pl.multiple_of(it * GDN_GROUP, GDN_GROUP), GDN_GROUP)
        lhs_ref[sl] = jnp.concatenate(
            [q * exp_g - ao[:, :, :QUAD], jnp.where(bd_mask, -kt[:, :, :QUAD], 0.0)], axis=1).astype(BF16)
        oloc_ref[sl] = ao[:, :, QUAD:]
        snew_ref[sl] = jnp.where(bd_mask, kt[:, :, QUAD:], 0.0)
        dec_ref[sl] = jnp.broadcast_to(jnp.exp(g_last), (GDN_GROUP, 8, QUAD))
        return carry

    lax.fori_loop(0, nb // GDN_GROUP, local_pass, 0)

    def scan_step(c, carry):
        sl = pl.ds(pl.multiple_of(2 * c, 2), 2)
        state = state_ref[...]
        r = _bmm(lhs_ref[sl], state.astype(BF16))
        state_ref[...] = state * dec_ref[sl][:, 0:1, :] + r[:, GDN_CHUNK:] + snew_ref[sl]
        o = r[:, :GDN_CHUNK] + oloc_ref[sl]
        rows = pl.ds(pl.multiple_of(c * GDN_CHUNK, GDN_CHUNK), GDN_CHUNK)
        oraw_ref[rows, 0:QUAD] = o[0]
        oraw_ref[rows, QUAD:] = o[1]
        return carry

    lax.fori_loop(0, nc, scan_step, 0)

    o_all = oraw_ref[...]
    ms = jnp.dot((o_all * o_all).astype(BF16), e512_ref[...], preferred_element_type=F32) * (1.0 / GDN_DIM)
    o_ref[...] = (o_all * lax.rsqrt(ms + NORM_EPS) * nw_ref[...]).astype(o_ref.dtype)


def _gdn(gdn_qkv, logits, alog_row, dtb_row, norm_row, batch, seq):
    rows = gdn_qkv.shape[0]
    tg = min(512, seq)
    steps = seq // tg
    nb = 2 * (tg // GDN_CHUNK)
    head_of_lane = np.arange(GDN_WIDTH) // GDN_DIM
    e512 = jnp.asarray(head_of_lane[:, None] == head_of_lane[None, :], BF16)
    e_beta = jnp.asarray(np.arange(LOGIT_PAD)[:, None] == head_of_lane[None, :], BF16)
    e_g = jnp.asarray(np.arange(LOGIT_PAD)[:, None] == head_of_lane[None, :] + GDN_HEADS, BF16)
    tri = jnp.asarray(np.tril(np.ones((GDN_CHUNK, GDN_CHUNK))), BF16)

    cur = lambda c: pl.BlockSpec((tg, 512), lambda b, n: (b * steps + n, c))
    whole = lambda a: pl.BlockSpec(a.shape, lambda b, n: (0,) * a.ndim)
    consts = (alog_row, dtb_row, norm_row, e512, e_beta, e_g, tri)
    return pl.pallas_call(
        functools.partial(_gdn_kernel, tg=tg),
        grid=(batch, steps),
        in_specs=[cur(0), cur(1), cur(2),
                  pl.BlockSpec((tg, LOGIT_PAD), lambda b, n: (b * steps + n, 0))] + [whole(a) for a in consts],
        out_specs=pl.BlockSpec((tg, GDN_WIDTH), lambda b, n: (b * steps + n, 0)),
        out_shape=jax.ShapeDtypeStruct((rows, GDN_WIDTH), BF16),
        scratch_shapes=[
            pltpu.VMEM((tg, GDN_WIDTH), F32),
            pltpu.VMEM((tg, GDN_WIDTH), BF16),
            pltpu.VMEM((tg, GDN_WIDTH), BF16),
            pltpu.VMEM((nb, GDN_CHUNK + QUAD, QUAD), BF16),
            pltpu.VMEM((nb, GDN_CHUNK, QUAD), F32),
            pltpu.VMEM((nb, QUAD, QUAD), F32),
            pltpu.VMEM((nb, 8, QUAD), F32),
            pltpu.VMEM((tg, GDN_WIDTH), F32),
            pltpu.VMEM((2, QUAD, QUAD), F32),
        ],
        compiler_params=pltpu.CompilerParams(
            dimension_semantics=("parallel", "arbitrary"), vmem_limit_bytes=VMEM_LIMIT_BYTES),
        name="gdn",
    )(gdn_qkv, gdn_qkv, gdn_qkv, logits, *consts)


def _rope_tables(seq):
    inv_freq = np.float32(ROPE_THETA) ** (-np.arange(0, DSA_HEAD_DIM, 2, dtype=np.float32) / np.float32(DSA_HEAD_DIM))
    ang = np.arange(seq, dtype=np.float32)[:, None] * inv_freq[None, :]
    cos, sin = np.cos(ang), np.sin(ang)
    return (jnp.asarray(np.concatenate([cos, cos, cos, cos], axis=-1), F32),
            jnp.asarray(np.concatenate([-sin, sin, -sin, sin], axis=-1), F32))


def kernel(x, norm_w, w_in, conv_w, a_log, dt_bias, gdn_norm_w, w_up_a, w_up_b, w_out, final_norm_w):
    batch, seq, _ = x.shape
    assert norm_w.shape[0] == 1, "the final RMSNorm is fused into the (single) layer's output kernel"
    cos_t, sin_t = _rope_tables(seq)
    x2 = x.reshape(batch * seq, D_MODEL)
    w = w_in[0].astype(BF16)
    gates = LOGIT_START + 2 * GDN_HEADS
    dsa_z, gdn_z = QKV_COLS, GDN_IN_START + GROUP_COLS
    w_gates = jnp.concatenate(
        [w[:, gates:], w[:, dsa_z:GDN_IN_START], w[:, gdn_z:LOGIT_START],
         w[:, LOGIT_START:gates], jnp.zeros((D_MODEL, LOGIT_PAD - 2 * GDN_HEADS), BF16)], axis=1)
    qkv, gdn_qkv, gate_cols, logits = _in_proj(x2, norm_w[0][None, :], w, w_gates, conv_w[0], cos_t, sin_t, seq)

    o_groups, lse_groups = [], []
    for g, (_, dilation) in enumerate(DSA_PATTERNS):
        o_g, lse_g = _dsa_attention(qkv, g, dilation, batch, seq)
        o_groups.append(o_g)
        lse_groups.append(lse_g)

    pad8 = lambda p: jnp.pad(p.astype(F32), (GDN_HEADS, LOGIT_PAD - 2 * GDN_HEADS))[None, :]
    o_b = _gdn(gdn_qkv, logits, pad8(a_log[0]), pad8(dt_bias[0]),
               jnp.tile(gdn_norm_w[0].astype(F32), GDN_HEADS)[None, :], batch, seq)

    out = _out_proj(o_groups, lse_groups, gate_cols, o_b, x2,
                    w_up_a[0].astype(BF16), w_up_b[0].astype(BF16), w_out[0].astype(BF16),
                    final_norm_w[None, :])
    return out.reshape(batch, seq, D_MODEL)
```

```python
import functools

import numpy as np
import jax
import jax.numpy as jnp
from jax import lax
from jax.experimental import pallas as pl
from jax.experimental.pallas import tpu as pltpu

F32 = jnp.float32
BF16 = jnp.bfloat16

D_MODEL = 1024
DSA_PATTERNS = ((128, 1), (512, 4), (2048, 16))
DSA_HEADS = 8
DSA_HEAD_DIM = 64
DSA_WIDTH = DSA_HEADS * DSA_HEAD_DIM
DSA_BLOCK = 128
ROPE_THETA = 10000.0
GDN_HEADS = 8
GDN_DIM = 64
GDN_WIDTH = GDN_HEADS * GDN_DIM
GDN_CONV = 4
GDN_CHUNK = 64
NORM_EPS = 1e-6

QKV_COLS = 3 * 3 * DSA_WIDTH
GROUP_COLS = 3 * DSA_WIDTH
GATE_COLS = 3072
GATE_A, GATE_B = 0, 1
GATE_DSA_Z, GATE_GDN_Z = 4, 5
GDN_IN_START = QKV_COLS + DSA_WIDTH
LOGIT_START = 7168
LOGIT_PAD = 128
PROJ_ROWS = 512
LOG2_E = 1.4426950408889634
LN_2 = 0.6931471805599453
DSA_Q_SCALE = DSA_HEAD_DIM ** -0.5 * LOG2_E

VMEM_LIMIT_BYTES = 56 * 1024 * 1024
LANES = 128
NEG_BIG = -1e30


def _lane_iota(shape):
    return lax.broadcasted_iota(jnp.int32, shape, len(shape) - 1)


def _in_proj_qkv_kernel(x_ref, nw_ref, w_ref, cos_ref, sin_ref, qkv_ref, h_ref):
    x = x_ref[...]
    ms = jnp.mean(x * x, axis=-1, keepdims=True)
    h = (x * lax.rsqrt(ms + NORM_EPS) * nw_ref[...]).astype(BF16)
    h_ref[...] = h
    cos = cos_ref[...]
    sin = sin_ref[...]
    low_half = (_lane_iota(cos.shape) % DSA_HEAD_DIM) < (DSA_HEAD_DIM // 2)
    for g in range(len(DSA_PATTERNS)):
        c0 = g * GROUP_COLS
        acc = jnp.dot(h, w_ref[:, c0:c0 + GROUP_COLS], preferred_element_type=F32)
        for c in range(8):
            t = acc[:, c * LANES:(c + 1) * LANES]
            rot = jnp.where(low_half, pltpu.roll(t, LANES - 32, 1), pltpu.roll(t, 32, 1))
            r = t * cos + rot * sin
            if c < 4:
                r = r * DSA_Q_SCALE
            qkv_ref[:, c0 + c * LANES:c0 + (c + 1) * LANES] = r.astype(BF16)
        qkv_ref[:, c0 + 8 * LANES:c0 + GROUP_COLS] = acc[:, 8 * LANES:].astype(BF16)


def _in_proj_rest_kernel(h_ref, wgq_ref, wgk_ref, wgv_ref, w_ref, wl_ref, cw_ref, e512_ref,
                         gdn_ref, gates_ref, logit_ref, xs_ref, *, tiles_per_seq):
    tm = h_ref.shape[0]
    first = pl.program_id(0) % tiles_per_seq == 0

    @pl.when(first)
    def _():
        xs_ref[0:8, :] = jnp.zeros((8, GROUP_COLS), F32)

    @pl.when(jnp.logical_not(first))
    def _():
        xs_ref[0:8, :] = xs_ref[tm:tm + 8, :]

    h = h_ref[...]
    logit_ref[...] = jnp.dot(h, wl_ref[...], preferred_element_type=F32)
    for part, wg_ref in enumerate((wgq_ref, wgk_ref, wgv_ref)):
        xs_ref[8:8 + tm, part * GDN_WIDTH:(part + 1) * GDN_WIDTH] = jnp.dot(
            h, wg_ref[...], preferred_element_type=F32)
    for g in range(GATE_COLS // GROUP_COLS):
        c0 = g * GROUP_COLS
        gates_ref[:, c0:c0 + GROUP_COLS] = jnp.dot(
            h, w_ref[:, c0:c0 + GROUP_COLS], preferred_element_type=F32).astype(BF16)
    for part in range(3):
        cols = slice(part * GDN_WIDTH, (part + 1) * GDN_WIDTH)
        w = cw_ref[:, cols]
        y = w[0:1, :] * xs_ref[5:5 + tm, cols]
        for j in range(1, GDN_CONV):
            y = y + w[j:j + 1, :] * xs_ref[5 + j:5 + j + tm, cols]
        y = _silu(y)
        if part < 2:
            ss = jnp.dot((y * y).astype(BF16), e512_ref[...], preferred_element_type=F32)
            y = y * (lax.rsqrt(ss + NORM_EPS) * (GDN_DIM ** -0.5 if part == 0 else 1.0))
        gdn_ref[:, cols] = y.astype(BF16)


def _resident(shape, col_block=0):
    return pl.BlockSpec(shape, lambda i: (0,) * (len(shape) - 1) + (col_block,), pipeline_mode=pl.Buffered(1))


def _in_proj(x2, norm_w, w_bf16, w_gates, conv_w, cos_t, sin_t, seq):
    assert GDN_IN_START % GDN_WIDTH == 0 and GATE_COLS % LOGIT_PAD == 0
    rows = x2.shape[0]
    tm = min(PROJ_ROWS, seq)
    n_seq_tiles = seq // tm
    row = lambda width: pl.BlockSpec((tm, width), lambda i: (i, 0))
    table = pl.BlockSpec((tm, LANES), lambda i: (i % n_seq_tiles, 0))
    head_of_lane = np.arange(GDN_WIDTH) // GDN_DIM
    e512 = jnp.asarray(head_of_lane[:, None] == head_of_lane[None, :], BF16)
    gdn_block = GDN_IN_START // GDN_WIDTH
    qkv, h = pl.pallas_call(
        _in_proj_qkv_kernel,
        grid=(rows // tm,),
        in_specs=[row(D_MODEL), _resident((1, D_MODEL)), _resident((D_MODEL, QKV_COLS)), table, table],
        out_specs=[row(QKV_COLS), row(D_MODEL)],
        out_shape=[jax.ShapeDtypeStruct((rows, QKV_COLS), BF16), jax.ShapeDtypeStruct((rows, D_MODEL), BF16)],
        compiler_params=pltpu.CompilerParams(dimension_semantics=("parallel",), vmem_limit_bytes=VMEM_LIMIT_BYTES),
        name="in_proj_qkv",
    )(x2, norm_w, w_bf16, cos_t, sin_t)
    gdn_qkv, gates, logits = pl.pallas_call(
        functools.partial(_in_proj_rest_kernel, tiles_per_seq=n_seq_tiles),
        grid=(rows // tm,),
        in_specs=[row(D_MODEL), _resident((D_MODEL, GDN_WIDTH), gdn_block),
                  _resident((D_MODEL, GDN_WIDTH), gdn_block + 1), _resident((D_MODEL, GDN_WIDTH), gdn_block + 2),
                  _resident((D_MODEL, GATE_COLS)), _resident((D_MODEL, LOGIT_PAD), GATE_COLS // LOGIT_PAD),
                  _resident(conv_w.shape), _resident(e512.shape)],
        out_specs=[row(GROUP_COLS), row(GATE_COLS), row(LOGIT_PAD)],
        out_shape=[jax.ShapeDtypeStruct((rows, GROUP_COLS), BF16), jax.ShapeDtypeStruct((rows, GATE_COLS), BF16),
                   jax.ShapeDtypeStruct((rows, LOGIT_PAD), F32)],
        scratch_shapes=[pltpu.VMEM((tm + 8, GROUP_COLS), F32)],
        compiler_params=pltpu.CompilerParams(
            dimension_semantics=("arbitrary",), vmem_limit_bytes=VMEM_LIMIT_BYTES),
        name="in_proj_rest",
    )(h, w_bf16, w_bf16, w_bf16, w_gates, w_gates, conv_w, e512)
    return qkv, gdn_qkv, gates, logits


DSA_SPAN = 2048
LSE_REP = LANES // DSA_HEADS


def _dsa_kernel(q_ref, k_ref, v_ref, o_ref, lse_ref, stage_ref, qp_ref, kp_ref, vp_ref, lstage_ref, tmp_ref=None,
                *, d, span):
    first = pl.program_id(1) == 0
    npr = span // d
    n_sub = npr // DSA_BLOCK
    kstride = npr + DSA_BLOCK

    def regroup(src_ref, dst_ref, dst_stride, dst_off):
        if d == 1:
            dst_ref[dst_off:dst_off + npr, :] = src_ref[...]
            return
        for c in range(4):
            stage_ref[c] = src_ref[:, c * LANES:(c + 1) * LANES].astype(F32)
        if d > 4:
            quarter = span // 4
            for c in range(4):
                for r4 in range(4):
                    tmp_ref[c, r4 * quarter:(r4 + 1) * quarter, :] = stage_ref[c, pl.ds(r4, quarter, stride=4), :]
            slabs, stride, start = tmp_ref, d // 4, lambda r: (r % 4) * quarter + r // 4
        else:
            slabs, stride, start = stage_ref, d, lambda r: r
        for c in range(4):
            for r in range(d):
                lo = r * dst_stride + dst_off
                dst_ref[lo:lo + npr, c * LANES:(c + 1) * LANES] = (
                    slabs[c, pl.ds(start(r), npr, stride=stride), :].astype(BF16))

    @pl.when(first)
    def _():
        for r in range(d):
            kp_ref[r * kstride:r * kstride + DSA_BLOCK, :] = jnp.zeros((DSA_BLOCK, DSA_WIDTH), BF16)
            vp_ref[r * kstride:r * kstride + DSA_BLOCK, :] = jnp.zeros((DSA_BLOCK, DSA_WIDTH), BF16)

    regroup(q_ref, qp_ref, npr, 0)
    regroup(k_ref, kp_ref, kstride, DSA_BLOCK)
    regroup(v_ref, vp_ref, kstride, DSA_BLOCK)

    qi = lax.broadcasted_iota(jnp.int32, (DSA_BLOCK, 2 * DSA_BLOCK), 0)
    kj = lax.broadcasted_iota(jnp.int32, (DSA_BLOCK, 2 * DSA_BLOCK), 1)
    band = (kj >= qi) & (kj <= qi + DSA_BLOCK)
    head_a = _lane_iota((DSA_BLOCK, LANES)) < DSA_HEAD_DIM
    lse_head = _lane_iota((DSA_BLOCK, LANES)) // LSE_REP

    def block(blk, carry):
        r, i = blk // n_sub, blk % n_sub
        q0 = pl.multiple_of(r * npr + i * DSA_BLOCK, DSA_BLOCK)
        k0 = pl.multiple_of(r * kstride + i * DSA_BLOCK, DSA_BLOCK)
        nat0 = r + d * DSA_BLOCK * i
        mask = band & (kj >= jnp.where(first & (i == 0), DSA_BLOCK, 0))
        m_tile = jnp.zeros((DSA_BLOCK, LANES), F32)
        den_tile = jnp.ones((DSA_BLOCK, LANES), F32)
        for hp in range(DSA_HEADS // 2):
            cols = slice(hp * LANES, (hp + 1) * LANES)
            q2 = qp_ref[pl.ds(q0, DSA_BLOCK), cols]
            k2 = kp_ref[pl.ds(k0, 2 * DSA_BLOCK), cols]
            v2 = vp_ref[pl.ds(k0, 2 * DSA_BLOCK), cols]
            pvs, ms, dens = [], [], []
            for is_a in (True, False):
                sel = head_a if is_a else jnp.logical_not(head_a)
                qh = jnp.where(sel, q2, jnp.zeros_like(q2))
                s = lax.dot_general(qh, k2, (((1,), (1,)), ((), ())), preferred_element_type=F32)
                s = jnp.where(mask, s, NEG_BIG)
                m = jnp.max(s, axis=-1, keepdims=True)
                p = jnp.exp2(s - m)
                dens.append(jnp.sum(p, axis=-1, keepdims=True))
                pvs.append(jnp.dot(p.astype(BF16), v2, preferred_element_type=F32))
                ms.append(m)
            o_pair = jnp.where(head_a, pvs[0], pvs[1]) / jnp.where(head_a, dens[0], dens[1])
            den_tile = jnp.where(lse_head == 2 * hp, dens[0], jnp.where(lse_head == 2 * hp + 1, dens[1], den_tile))
            m_tile = jnp.where(lse_head == 2 * hp, ms[0], jnp.where(lse_head == 2 * hp + 1, ms[1], m_tile))
            if d == 1:
                o_ref[pl.ds(q0, DSA_BLOCK), cols] = o_pair.astype(o_ref.dtype)
            else:
                stage_ref[hp, pl.ds(nat0, DSA_BLOCK, stride=d), :] = o_pair
        lse_tile = m_tile * LN_2 + jnp.log(den_tile)
        if d == 1:
            lse_ref[pl.ds(q0, DSA_BLOCK), :] = lse_tile
        else:
            lstage_ref[pl.ds(nat0, DSA_BLOCK, stride=d), :] = lse_tile
        return carry

    lax.fori_loop(0, d * n_sub, block, 0, unroll=16)

    for r in range(d):
        lo = r * kstride
        kp_ref[lo:lo + DSA_BLOCK, :] = kp_ref[lo + npr:lo + npr + DSA_BLOCK, :]
        vp_ref[lo:lo + DSA_BLOCK, :] = vp_ref[lo + npr:lo + npr + DSA_BLOCK, :]
    if d > 1:
        for c in range(4):
            o_ref[:, c * LANES:(c + 1) * LANES] = stage_ref[c].astype(o_ref.dtype)
        lse_ref[...] = lstage_ref[...]


def _dsa_attention(qkv, group, dilation, batch, seq):
    span = min(DSA_SPAN, seq)
    steps = seq // span
    kv_rows = span + dilation * DSA_BLOCK
    col = lambda t: pl.BlockSpec((span, 512), lambda b, n: (b * steps + n, group * 3 + t))
    row = lambda width: pl.BlockSpec((span, width), lambda b, n: (b * steps + n, 0))
    return pl.pallas_call(
        functools.partial(_dsa_kernel, d=dilation, span=span),
        grid=(batch, steps),
        in_specs=[col(0), col(1), col(2)],
        out_specs=[row(DSA_WIDTH), row(LANES)],
        out_shape=[
            jax.ShapeDtypeStruct((batch * seq, DSA_WIDTH), BF16),
            jax.ShapeDtypeStruct((batch * seq, LANES), F32),
        ],
        scratch_shapes=[
            pltpu.VMEM((4, span, LANES), F32),
            pltpu.VMEM((span, DSA_WIDTH), BF16),
            pltpu.VMEM((kv_rows, DSA_WIDTH), BF16),
            pltpu.VMEM((kv_rows, DSA_WIDTH), BF16),
            pltpu.VMEM((span, LANES), F32),
        ] + ([pltpu.VMEM((4, span, LANES), F32)] if dilation > 4 else []),
        compiler_params=pltpu.CompilerParams(
            dimension_semantics=("parallel", "arbitrary"), vmem_limit_bytes=VMEM_LIMIT_BYTES),
        name=f"dsa_attn_d{dilation}",
    )(qkv, qkv, qkv)


def _silu(x):
    return x * (1.0 / (1.0 + jnp.exp(-x)))


def _sigmoid(x):
    return 1.0 / (1.0 + jnp.exp(-x))


def _out_proj_kernel(o0_ref, o1_ref, o2_ref, l0_ref, l1_ref, l2_ref, za_ref, ob_ref, zb_ref, ga_ref, gb_ref,
                     x_ref, wa_ref, wb_ref, wo_ref, fw_ref, ex_ref, out_ref):
    l0, l1, l2 = l0_ref[...], l1_ref[...], l2_ref[...]
    mx = jnp.maximum(jnp.maximum(l0, l1), l2)
    e0, e1, e2 = jnp.exp(l0 - mx), jnp.exp(l1 - mx), jnp.exp(l2 - mx)
    inv = 1.0 / (e0 + e1 + e2)

    def per_lane(w):
        return jnp.dot(w.astype(BF16), ex_ref[...], preferred_element_type=F32)

    o_a = (per_lane(e0 * inv) * o0_ref[...].astype(F32) + per_lane(e1 * inv) * o1_ref[...].astype(F32)
           + per_lane(e2 * inv) * o2_ref[...].astype(F32))
    a_in = (o_a * _silu(za_ref[...].astype(F32))).astype(BF16)
    b_in = (ob_ref[...].astype(F32) * _silu(zb_ref[...].astype(F32))).astype(BF16)
    y_a = jnp.dot(a_in, wa_ref[...], preferred_element_type=F32)
    y_b = jnp.dot(b_in, wb_ref[...], preferred_element_type=F32)
    merged = _sigmoid(ga_ref[...].astype(F32)) * y_a + _sigmoid(gb_ref[...].astype(F32)) * y_b
    y = x_ref[...] + jnp.dot(merged.astype(BF16), wo_ref[...], preferred_element_type=F32)
    ms = jnp.mean(y * y, axis=-1, keepdims=True)
    out_ref[...] = y * lax.rsqrt(ms + NORM_EPS) * fw_ref[...]


def _out_proj(o_groups, lse_groups, gates, o_b, x2, w_up_a, w_up_b, w_out, final_w):
    rows = x2.shape[0]
    tm = min(512, rows)
    row128 = pl.BlockSpec((tm, LANES), lambda i: (i, 0))
    row512 = lambda c: pl.BlockSpec((tm, 512), lambda i: (i, c))
    row1024 = lambda c: pl.BlockSpec((tm, 1024), lambda i: (i, c))
    whole = lambda a: pl.BlockSpec(a.shape, lambda i: (0,) * a.ndim)
    expand = jnp.asarray(np.arange(LANES)[:, None] == LSE_REP * (np.arange(DSA_WIDTH)[None, :] // DSA_HEAD_DIM), BF16)
    return pl.pallas_call(
        _out_proj_kernel,
        grid=(rows // tm,),
        in_specs=[row512(0)] * 3 + [row128] * 3 + [
            row512(GATE_DSA_Z), row512(0), row512(GATE_GDN_Z), row1024(GATE_A), row1024(GATE_B), row1024(0),
            whole(w_up_a), whole(w_up_b), whole(w_out), whole(final_w), whole(expand)],
        out_specs=pl.BlockSpec((tm, D_MODEL), lambda i: (i, 0)),
        out_shape=jax.ShapeDtypeStruct((rows, D_MODEL), F32),
        compiler_params=pltpu.CompilerParams(
            dimension_semantics=("parallel",), vmem_limit_bytes=VMEM_LIMIT_BYTES),
        name="out_proj",
    )(*o_groups, *lse_groups, gates, o_b, gates, gates, gates, x2, w_up_a, w_up_b, w_out, final_w, expand)


QUAD = 4 * GDN_DIM
GDN_TILE_ROWS = 512


def _split_hi_lo(x):
    hi = x.astype(BF16)
    lo = (x - hi.astype(F32)).astype(BF16)
    return hi, lo


def _softplus(x):
    return jnp.maximum(x, 0.0) + jnp.log1p(jnp.exp(-jnp.abs(x)))


def _bmm(a, b):
    return lax.dot_general(a, b, (((2,), (1,)), ((0,), (0,))), preferred_element_type=F32)


def _bmm_nt(a, b):
    return lax.dot_general(a, b, (((2,), (2,)), ((0,), (0,))), preferred_element_type=F32)


def _block_diag(x, bd_mask):
    t = jnp.concatenate([x, x, x, x], axis=1)
    return jnp.where(bd_mask, t, jnp.zeros_like(t))


def _gdn_kernel(q_ref, k_ref, v_ref, lg_ref, alog_ref, dtb_ref, nw_ref, e512_ref, eb_ref, eg_ref, tri_ref, o_ref,
                beta_ref, gh_ref, gl_ref, lhs_ref, oloc_ref, snew_ref, dec_ref, oraw_ref, state_ref):
    first = pl.program_id(0) == 0
    batch, rows, _ = q_ref.shape
    tg = batch * rows
    nc = rows // GDN_CHUNK
    per_step = 2 * batch
    nb = per_step * nc

    @pl.when(first)
    def _():
        state_ref[...] = jnp.zeros_like(state_ref)

    logits = lg_ref[...].reshape(tg, LOGIT_PAD)
    beta_hi, beta_lo = _split_hi_lo(_sigmoid(logits))
    g_hi, g_lo = _split_hi_lo(-jnp.exp(alog_ref[...]) * _softplus(logits + dtb_ref[...]))
    beta_ref[...] = (jnp.dot(beta_hi, eb_ref[...], preferred_element_type=F32)
                     + jnp.dot(beta_lo, eb_ref[...], preferred_element_type=F32))
    gh_ref[...] = jnp.dot(g_hi, eg_ref[...], preferred_element_type=F32).astype(BF16)
    gl_ref[...] = jnp.dot(g_lo, eg_ref[...], preferred_element_type=F32).astype(BF16)

    row = lax.broadcasted_iota(jnp.int32, (GDN_CHUNK, QUAD), 0)
    col = lax.broadcasted_iota(jnp.int32, (GDN_CHUNK, QUAD), 1) % GDN_CHUNK
    incl = row >= col
    strict = row > col
    eye = (row == col).astype(F32)
    bd_mask = (lax.broadcasted_iota(jnp.int32, (QUAD, QUAD), 0) // GDN_DIM
               == lax.broadcasted_iota(jnp.int32, (QUAD, QUAD), 1) // GDN_DIM)
    tri = tri_ref[...]

    def chunk_quads(ref, dtype):
        a = ref[...].reshape(batch, nc, GDN_CHUNK, GDN_WIDTH).astype(dtype)
        a = jnp.stack([a[b, j] for j in range(nc) for b in range(batch)])
        return jnp.stack([a[:, :, :QUAD], a[:, :, QUAD:]], axis=1).reshape(nb, GDN_CHUNK, QUAD)

    q, k, v = (chunk_quads(r, F32) for r in (q_ref, k_ref, v_ref))
    beta = chunk_quads(beta_ref, F32)
    gh, gl = chunk_quads(gh_ref, BF16), chunk_quads(gl_ref, BF16)

    zero = jnp.zeros_like(gh)
    rhs_hi = jnp.concatenate([gh, jnp.where(strict, gh, zero)], axis=-1)
    rhs_lo = jnp.concatenate([gl, jnp.where(strict, gl, zero)], axis=-1)
    tri_b = jnp.broadcast_to(tri, (nb, GDN_CHUNK, GDN_CHUNK))
    gd = _bmm(tri_b, rhs_hi) + _bmm(tri_b, rhs_lo)
    g_cum, d_pair = gd[:, :, :QUAD], gd[:, :, QUAD:]
    decay_incl = jnp.where(incl, jnp.exp(d_pair), 0.0)
    decay_strict = jnp.where(strict, decay_incl, 0.0)
    exp_g = jnp.exp(g_cum)
    g_last = g_cum[:, GDN_CHUNK - 1:GDN_CHUNK, :]
    k_dec = k * jnp.exp(g_last - g_cum)
    k_beta = k * beta

    k_bd = _block_diag(k.astype(BF16), bd_mask)
    aa = _bmm_nt(jnp.concatenate([k_beta, q], axis=1).astype(BF16), k_bd)
    a = aa[:, :GDN_CHUNK] * decay_strict
    attn = aa[:, GDN_CHUNK:] * decay_incl

    p = a
    t = eye - a
    p = _bmm(p.astype(BF16), _block_diag(p.astype(BF16), bd_mask))
    for _ in range(4):
        r = _bmm(jnp.concatenate([p, t], axis=1).astype(BF16), _block_diag(p.astype(BF16), bd_mask))
        p = r[:, :GDN_CHUNK]
        t = t + r[:, GDN_CHUNK:]
    t = t + _bmm(t.astype(BF16), _block_diag(p.astype(BF16), bd_mask))
    t16 = t.astype(BF16)
    u = _bmm(t16, _block_diag((v * beta).astype(BF16), bd_mask))
    w = _bmm(t16, _block_diag((k_beta * exp_g).astype(BF16), bd_mask))

    w16, u16 = w.astype(BF16), u.astype(BF16)
    kt = _bmm(jnp.swapaxes(k_dec, 1, 2).astype(BF16), jnp.concatenate([w16, u16], axis=-1))
    ao = _bmm(attn.astype(BF16),
              jnp.concatenate([_block_diag(w16, bd_mask), _block_diag(u16, bd_mask)], axis=-1))
    lhs_ref[...] = jnp.concatenate(
        [q * exp_g - ao[:, :, :QUAD], jnp.where(bd_mask, -kt[:, :, :QUAD], 0.0)], axis=1).astype(BF16)
    oloc_ref[...] = ao[:, :, QUAD:]
    snew_ref[...] = jnp.where(bd_mask, kt[:, :, QUAD:], 0.0)
    dec_ref[...] = jnp.broadcast_to(jnp.exp(g_last), (nb, 8, QUAD))

    for j in range(nc):
        sl = slice(j * per_step, (j + 1) * per_step)
        state = state_ref[...]
        r = _bmm(lhs_ref[sl], state.astype(BF16))
        state_ref[...] = state * dec_ref[sl][:, 0:1, :] + r[:, GDN_CHUNK:] + snew_ref[sl]
        o = r[:, :GDN_CHUNK] + oloc_ref[sl]
        for b in range(batch):
            lo = b * rows + j * GDN_CHUNK
            oraw_ref[lo:lo + GDN_CHUNK, 0:QUAD] = o[2 * b]
            oraw_ref[lo:lo + GDN_CHUNK, QUAD:] = o[2 * b + 1]

    o_all = oraw_ref[...]
    ms = jnp.dot((o_all * o_all).astype(BF16), e512_ref[...], preferred_element_type=F32) * (1.0 / GDN_DIM)
    o_ref[...] = (o_all * lax.rsqrt(ms + NORM_EPS) * nw_ref[...]).astype(o_ref.dtype).reshape(batch, rows, GDN_WIDTH)


def _gdn(gdn_qkv, logits, alog_row, dtb_row, norm_row, batch, seq):
    tg = min(GDN_TILE_ROWS, batch * seq)
    rows = tg // batch
    steps = seq // rows
    nb = 2 * (tg // GDN_CHUNK)
    head_of_lane = np.arange(GDN_WIDTH) // GDN_DIM
    e512 = jnp.asarray(head_of_lane[:, None] == head_of_lane[None, :], BF16)
    e_beta = jnp.asarray(np.arange(LOGIT_PAD)[:, None] == head_of_lane[None, :], BF16)
    e_g = jnp.asarray(np.arange(LOGIT_PAD)[:, None] == head_of_lane[None, :] + GDN_HEADS, BF16)
    tri = jnp.asarray(np.tril(np.ones((GDN_CHUNK, GDN_CHUNK))), BF16)

    cur = lambda width, c: pl.BlockSpec((batch, rows, width), lambda n: (0, n, c))
    whole = lambda a: pl.BlockSpec(a.shape, lambda n: (0,) * a.ndim)
    consts = (alog_row, dtb_row, norm_row, e512, e_beta, e_g, tri)
    qkv3 = gdn_qkv.reshape(batch, seq, GROUP_COLS)
    out = pl.pallas_call(
        _gdn_kernel,
        grid=(steps,),
        in_specs=[cur(GDN_WIDTH, 0), cur(GDN_WIDTH, 1), cur(GDN_WIDTH, 2), cur(LOGIT_PAD, 0)]
        + [whole(a) for a in consts],
        out_specs=cur(GDN_WIDTH, 0),
        out_shape=jax.ShapeDtypeStruct((batch, seq, GDN_WIDTH), BF16),
        scratch_shapes=[
            pltpu.VMEM((tg, GDN_WIDTH), F32),
            pltpu.VMEM((tg, GDN_WIDTH), BF16),
            pltpu.VMEM((tg, GDN_WIDTH), BF16),
            pltpu.VMEM((nb, GDN_CHUNK + QUAD, QUAD), BF16),
            pltpu.VMEM((nb, GDN_CHUNK, QUAD), F32),
            pltpu.VMEM((nb, QUAD, QUAD), F32),
            pltpu.VMEM((nb, 8, QUAD), F32),
            pltpu.VMEM((tg, GDN_WIDTH), F32),
            pltpu.VMEM((2 * batch, QUAD, QUAD), F32),
        ],
        compiler_params=pltpu.CompilerParams(
            dimension_semantics=("arbitrary",), vmem_limit_bytes=VMEM_LIMIT_BYTES),
        name="gdn",
    )(qkv3, qkv3, qkv3, logits.reshape(batch, seq, LOGIT_PAD), *consts)
    return out.reshape(batch * seq, GDN_WIDTH)


def _rope_tables(seq):
    inv_freq = np.float32(ROPE_THETA) ** (-np.arange(0, DSA_HEAD_DIM, 2, dtype=np.float32) / np.float32(DSA_HEAD_DIM))
    ang = np.arange(seq, dtype=np.float32)[:, None] * inv_freq[None, :]
    cos, sin = np.cos(ang), np.sin(ang)
    return (jnp.asarray(np.concatenate([cos, cos, cos, cos], axis=-1), F32),
            jnp.asarray(np.concatenate([-sin, sin, -sin, sin], axis=-1), F32))


def kernel(x, norm_w, w_in, conv_w, a_log, dt_bias, gdn_norm_w, w_up_a, w_up_b, w_out, final_norm_w):
    batch, seq, _ = x.shape
    assert norm_w.shape[0] == 1, "the final RMSNorm is fused into the (single) layer's output kernel"
    cos_t, sin_t = _rope_tables(seq)
    x2 = x.reshape(batch * seq, D_MODEL)
    w = w_in[0].astype(BF16)
    gates = LOGIT_START + 2 * GDN_HEADS
    dsa_z, gdn_z = QKV_COLS, GDN_IN_START + GROUP_COLS
    w_gates = jnp.concatenate(
        [w[:, gates:], w[:, dsa_z:GDN_IN_START], w[:, gdn_z:LOGIT_START],
         w[:, LOGIT_START:gates], jnp.zeros((D_MODEL, LOGIT_PAD - 2 * GDN_HEADS), BF16)], axis=1)
    qkv, gdn_qkv, gate_cols, logits = _in_proj(x2, norm_w[0][None, :], w, w_gates, conv_w[0], cos_t, sin_t, seq)

    o_groups, lse_groups = [], []
    for g, (_, dilation) in enumerate(DSA_PATTERNS):
        o_g, lse_g = _dsa_attention(qkv, g, dilation, batch, seq)
        o_groups.append(o_g)
        lse_groups.append(lse_g)

    pad8 = lambda p: jnp.pad(p.astype(F32), (GDN_HEADS, LOGIT_PAD - 2 * GDN_HEADS))[None, :]
    o_b = _gdn(gdn_qkv, logits, pad8(a_log[0]), pad8(dt_bias[0]),
               jnp.tile(gdn_norm_w[0].astype(F32), GDN_HEADS)[None, :], batch, seq)

    out = _out_proj(o_groups, lse_groups, gate_cols, o_b, x2,
                    w_up_a[0].astype(BF16), w_up_b[0].astype(BF16), w_out[0].astype(BF16),
                    final_norm_w[None, :])
    return out.reshape(batch, seq, D_MODEL)
```

```python
import functools

import numpy as np
import jax
import jax.numpy as jnp
from jax import lax
from jax.experimental import pallas as pl
from jax.experimental.pallas import tpu as pltpu

F32 = jnp.float32
BF16 = jnp.bfloat16

D_MODEL = 1024
DSA_PATTERNS = ((128, 1), (512, 4), (2048, 16))
DSA_HEADS = 8
DSA_HEAD_DIM = 64
DSA_WIDTH = DSA_HEADS * DSA_HEAD_DIM
DSA_BLOCK = 128
ROPE_THETA = 10000.0
GDN_HEADS = 8
GDN_DIM = 64
GDN_WIDTH = GDN_HEADS * GDN_DIM
GDN_CONV = 4
GDN_CHUNK = 64
NORM_EPS = 1e-6

QKV_COLS = 3 * 3 * DSA_WIDTH
GROUP_COLS = 3 * DSA_WIDTH
GATE_COLS = 3072
GATE_A, GATE_B = 0, 1
GATE_DSA_Z, GATE_GDN_Z = 4, 5
GDN_IN_START = QKV_COLS + DSA_WIDTH
LOGIT_START = 7168
LOGIT_PAD = 128
PROJ_ROWS = 512
LOG2_E = 1.4426950408889634
LN_2 = 0.6931471805599453
DSA_Q_SCALE = DSA_HEAD_DIM ** -0.5 * LOG2_E

VMEM_LIMIT_BYTES = 56 * 1024 * 1024
LANES = 128
NEG_BIG = -1e30


def _lane_iota(shape):
    return lax.broadcasted_iota(jnp.int32, shape, len(shape) - 1)


def _in_proj_qkv_kernel(x_ref, nw_ref, w_ref, cos_ref, sin_ref, qkv_ref, h_ref):
    x = x_ref[...]
    ms = jnp.mean(x * x, axis=-1, keepdims=True)
    h = (x * lax.rsqrt(ms + NORM_EPS) * nw_ref[...]).astype(BF16)
    h_ref[...] = h
    cos = cos_ref[...]
    sin = sin_ref[...]
    low_half = (_lane_iota(cos.shape) % DSA_HEAD_DIM) < (DSA_HEAD_DIM // 2)
    for g in range(len(DSA_PATTERNS)):
        c0 = g * GROUP_COLS
        acc = jnp.dot(h, w_ref[:, c0:c0 + GROUP_COLS].astype(BF16), preferred_element_type=F32)
        for c in range(8):
            t = acc[:, c * LANES:(c + 1) * LANES]
            rot = jnp.where(low_half, pltpu.roll(t, LANES - 32, 1), pltpu.roll(t, 32, 1))
            r = t * cos + rot * sin
            if c < 4:
                r = r * DSA_Q_SCALE
            qkv_ref[:, c0 + c * LANES:c0 + (c + 1) * LANES] = r.astype(BF16)
        qkv_ref[:, c0 + 8 * LANES:c0 + GROUP_COLS] = acc[:, 8 * LANES:].astype(BF16)


def _in_proj_rest_kernel(h_ref, wgq_ref, wgk_ref, wgv_ref, w_ref, wl_ref, cw_ref, e512_ref,
                         gdn_ref, gates_ref, logit_ref, xs_ref, *, tiles_per_seq):
    tm = h_ref.shape[0]
    first = pl.program_id(0) % tiles_per_seq == 0

    @pl.when(first)
    def _():
        xs_ref[0:8, :] = jnp.zeros((8, GROUP_COLS), F32)

    @pl.when(jnp.logical_not(first))
    def _():
        xs_ref[0:8, :] = xs_ref[tm:tm + 8, :]

    h = h_ref[...]
    logit_ref[...] = jnp.dot(h, wl_ref[...], preferred_element_type=F32)
    for part, wg_ref in enumerate((wgq_ref, wgk_ref, wgv_ref)):
        xs_ref[8:8 + tm, part * GDN_WIDTH:(part + 1) * GDN_WIDTH] = jnp.dot(
            h, wg_ref[...].astype(BF16), preferred_element_type=F32)
    for g in range(GATE_COLS // GROUP_COLS):
        c0 = g * GROUP_COLS
        gates_ref[:, c0:c0 + GROUP_COLS] = jnp.dot(
            h, w_ref[:, c0:c0 + GROUP_COLS], preferred_element_type=F32).astype(BF16)
    for part in range(3):
        cols = slice(part * GDN_WIDTH, (part + 1) * GDN_WIDTH)
        w = cw_ref[:, cols]
        y = w[0:1, :] * xs_ref[5:5 + tm, cols]
        for j in range(1, GDN_CONV):
            y = y + w[j:j + 1, :] * xs_ref[5 + j:5 + j + tm, cols]
        y = _silu(y)
        if part < 2:
            ss = jnp.dot((y * y).astype(BF16), e512_ref[...], preferred_element_type=F32)
            y = y * (lax.rsqrt(ss + NORM_EPS) * (GDN_DIM ** -0.5 if part == 0 else 1.0))
        gdn_ref[:, cols] = y.astype(BF16)


def _resident(shape, col_block=0):
    return pl.BlockSpec(shape, lambda i: (0,) * (len(shape) - 1) + (col_block,), pipeline_mode=pl.Buffered(1))


def _in_proj(x2, norm_w, w_f32, w_gates, conv_w, cos_t, sin_t, seq):
    assert GDN_IN_START % GDN_WIDTH == 0 and GATE_COLS % LOGIT_PAD == 0
    rows = x2.shape[0]
    tm = min(PROJ_ROWS, seq)
    n_seq_tiles = seq // tm
    row = lambda width: pl.BlockSpec((tm, width), lambda i: (i, 0))
    table = pl.BlockSpec((tm, LANES), lambda i: (i % n_seq_tiles, 0))
    head_of_lane = np.arange(GDN_WIDTH) // GDN_DIM
    e512 = jnp.asarray(head_of_lane[:, None] == head_of_lane[None, :], BF16)
    gdn_block = GDN_IN_START // GDN_WIDTH
    qkv, h = pl.pallas_call(
        _in_proj_qkv_kernel,
        grid=(rows // tm,),
        in_specs=[row(D_MODEL), _resident((1, D_MODEL)), _resident((D_MODEL, QKV_COLS)), table, table],
        out_specs=[row(QKV_COLS), row(D_MODEL)],
        out_shape=[jax.ShapeDtypeStruct((rows, QKV_COLS), BF16), jax.ShapeDtypeStruct((rows, D_MODEL), BF16)],
        compiler_params=pltpu.CompilerParams(dimension_semantics=("parallel",), vmem_limit_bytes=VMEM_LIMIT_BYTES),
        name="in_proj_qkv",
    )(x2, norm_w, w_f32, cos_t, sin_t)
    gdn_qkv, gates, logits = pl.pallas_call(
        functools.partial(_in_proj_rest_kernel, tiles_per_seq=n_seq_tiles),
        grid=(rows // tm,),
        in_specs=[row(D_MODEL), _resident((D_MODEL, GDN_WIDTH), gdn_block),
                  _resident((D_MODEL, GDN_WIDTH), gdn_block + 1), _resident((D_MODEL, GDN_WIDTH), gdn_block + 2),
                  _resident((D_MODEL, GATE_COLS)), _resident((D_MODEL, LOGIT_PAD), GATE_COLS // LOGIT_PAD),
                  _resident(conv_w.shape), _resident(e512.shape)],
        out_specs=[row(GROUP_COLS), row(GATE_COLS), row(LOGIT_PAD)],
        out_shape=[jax.ShapeDtypeStruct((rows, GROUP_COLS), BF16), jax.ShapeDtypeStruct((rows, GATE_COLS), BF16),
                   jax.ShapeDtypeStruct((rows, LOGIT_PAD), F32)],
        scratch_shapes=[pltpu.VMEM((tm + 8, GROUP_COLS), F32)],
        compiler_params=pltpu.CompilerParams(
            dimension_semantics=("arbitrary",), vmem_limit_bytes=VMEM_LIMIT_BYTES),
        name="in_proj_rest",
    )(h, w_f32, w_f32, w_f32, w_gates, w_gates, conv_w, e512)
    return qkv, gdn_qkv, gates, logits


DSA_SPAN = 2048
LSE_REP = LANES // DSA_HEADS


def _dsa_kernel(q_ref, k_ref, v_ref, o_ref, lse_ref, stage_ref, qp_ref, kp_ref, vp_ref, lstage_ref, tmp_ref=None,
                *, d, span):
    first = pl.program_id(1) == 0
    npr = span // d
    n_sub = npr // DSA_BLOCK
    kstride = npr + DSA_BLOCK

    def regroup(src_ref, dst_ref, dst_stride, dst_off):
        if d == 1:
            dst_ref[dst_off:dst_off + npr, :] = src_ref[...]
            return
        for c in range(4):
            stage_ref[c] = src_ref[:, c * LANES:(c + 1) * LANES].astype(F32)
        if d > 4:
            quarter = span // 4
            for c in range(4):
                for r4 in range(4):
                    tmp_ref[c, r4 * quarter:(r4 + 1) * quarter, :] = stage_ref[c, pl.ds(r4, quarter, stride=4), :]
            slabs, stride, start = tmp_ref, d // 4, lambda r: (r % 4) * quarter + r // 4
        else:
            slabs, stride, start = stage_ref, d, lambda r: r
        for c in range(4):
            for r in range(d):
                lo = r * dst_stride + dst_off
                dst_ref[lo:lo + npr, c * LANES:(c + 1) * LANES] = (
                    slabs[c, pl.ds(start(r), npr, stride=stride), :].astype(BF16))

    @pl.when(first)
    def _():
        for r in range(d):
            kp_ref[r * kstride:r * kstride + DSA_BLOCK, :] = jnp.zeros((DSA_BLOCK, DSA_WIDTH), BF16)
            vp_ref[r * kstride:r * kstride + DSA_BLOCK, :] = jnp.zeros((DSA_BLOCK, DSA_WIDTH), BF16)

    regroup(q_ref, qp_ref, npr, 0)
    regroup(k_ref, kp_ref, kstride, DSA_BLOCK)
    regroup(v_ref, vp_ref, kstride, DSA_BLOCK)

    qi = lax.broadcasted_iota(jnp.int32, (DSA_BLOCK, 2 * DSA_BLOCK), 0)
    kj = lax.broadcasted_iota(jnp.int32, (DSA_BLOCK, 2 * DSA_BLOCK), 1)
    band = (kj >= qi) & (kj <= qi + DSA_BLOCK)
    head_a = _lane_iota((DSA_BLOCK, LANES)) < DSA_HEAD_DIM
    lse_head = _lane_iota((DSA_BLOCK, LANES)) // LSE_REP

    def block(blk, carry):
        r, i = blk // n_sub, blk % n_sub
        q0 = pl.multiple_of(r * npr + i * DSA_BLOCK, DSA_BLOCK)
        k0 = pl.multiple_of(r * kstride + i * DSA_BLOCK, DSA_BLOCK)
        nat0 = r + d * DSA_BLOCK * i
        mask = band & (kj >= jnp.where(first & (i == 0), DSA_BLOCK, 0))
        m_tile = jnp.zeros((DSA_BLOCK, LANES), F32)
        den_tile = jnp.ones((DSA_BLOCK, LANES), F32)
        for hp in range(DSA_HEADS // 2):
            cols = slice(hp * LANES, (hp + 1) * LANES)
            q2 = qp_ref[pl.ds(q0, DSA_BLOCK), cols]
            k2 = kp_ref[pl.ds(k0, 2 * DSA_BLOCK), cols]
            v2 = vp_ref[pl.ds(k0, 2 * DSA_BLOCK), cols]
            pvs, ms, dens = [], [], []
            for is_a in (True, False):
                sel = head_a if is_a else jnp.logical_not(head_a)
                qh = jnp.where(sel, q2, jnp.zeros_like(q2))
                s = lax.dot_general(qh, k2, (((1,), (1,)), ((), ())), preferred_element_type=F32)
                s = jnp.where(mask, s, NEG_BIG)
                m = jnp.max(s, axis=-1, keepdims=True)
                p = jnp.exp2(s - m)
                dens.append(jnp.sum(p, axis=-1, keepdims=True))
                pvs.append(jnp.dot(p.astype(BF16), v2, preferred_element_type=F32))
                ms.append(m)
            o_pair = jnp.where(head_a, pvs[0], pvs[1]) / jnp.where(head_a, dens[0], dens[1])
            den_tile = jnp.where(lse_head == 2 * hp, dens[0], jnp.where(lse_head == 2 * hp + 1, dens[1], den_tile))
            m_tile = jnp.where(lse_head == 2 * hp, ms[0], jnp.where(lse_head == 2 * hp + 1, ms[1], m_tile))
            if d == 1:
                o_ref[pl.ds(q0, DSA_BLOCK), cols] = o_pair.astype(o_ref.dtype)
            else:
                stage_ref[hp, pl.ds(nat0, DSA_BLOCK, stride=d), :] = o_pair
        lse_tile = m_tile * LN_2 + jnp.log(den_tile)
        if d == 1:
            lse_ref[pl.ds(q0, DSA_BLOCK), :] = lse_tile
        else:
            lstage_ref[pl.ds(nat0, DSA_BLOCK, stride=d), :] = lse_tile
        return carry

    lax.fori_loop(0, d * n_sub, block, 0, unroll=16)

    for r in range(d):
        lo = r * kstride
        kp_ref[lo:lo + DSA_BLOCK, :] = kp_ref[lo + npr:lo + npr + DSA_BLOCK, :]
        vp_ref[lo:lo + DSA_BLOCK, :] = vp_ref[lo + npr:lo + npr + DSA_BLOCK, :]
    if d > 1:
        for c in range(4):
            o_ref[:, c * LANES:(c + 1) * LANES] = stage_ref[c].astype(o_ref.dtype)
        lse_ref[...] = lstage_ref[...]


def _dsa_attention(qkv, group, dilation, batch, seq):
    span = min(DSA_SPAN, seq)
    steps = seq // span
    kv_rows = span + dilation * DSA_BLOCK
    col = lambda t: pl.BlockSpec((span, 512), lambda b, n: (b * steps + n, group * 3 + t))
    row = lambda width: pl.BlockSpec((span, width), lambda b, n: (b * steps + n, 0))
    return pl.pallas_call(
        functools.partial(_dsa_kernel, d=dilation, span=span),
        grid=(batch, steps),
        in_specs=[col(0), col(1), col(2)],
        out_specs=[row(DSA_WIDTH), row(LANES)],
        out_shape=[
            jax.ShapeDtypeStruct((batch * seq, DSA_WIDTH), BF16),
            jax.ShapeDtypeStruct((batch * seq, LANES), F32),
        ],
        scratch_shapes=[
            pltpu.VMEM((4, span, LANES), F32),
            pltpu.VMEM((span, DSA_WIDTH), BF16),
            pltpu.VMEM((kv_rows, DSA_WIDTH), BF16),
            pltpu.VMEM((kv_rows, DSA_WIDTH), BF16),
            pltpu.VMEM((span, LANES), F32),
        ] + ([pltpu.VMEM((4, span, LANES), F32)] if dilation > 4 else []),
        compiler_params=pltpu.CompilerParams(
            dimension_semantics=("parallel", "arbitrary"), vmem_limit_bytes=VMEM_LIMIT_BYTES),
        name=f"dsa_attn_d{dilation}",
    )(qkv, qkv, qkv)


def _silu(x):
    return x * (1.0 / (1.0 + jnp.exp(-x)))


def _sigmoid(x):
    return 1.0 / (1.0 + jnp.exp(-x))


def _out_proj_kernel(o0_ref, o1_ref, o2_ref, l0_ref, l1_ref, l2_ref, za_ref, ob_ref, zb_ref, ga_ref, gb_ref,
                     x_ref, wa_ref, wb_ref, wo_ref, fw_ref, ex_ref, out_ref):
    l0, l1, l2 = l0_ref[...], l1_ref[...], l2_ref[...]
    mx = jnp.maximum(jnp.maximum(l0, l1), l2)
    e0, e1, e2 = jnp.exp(l0 - mx), jnp.exp(l1 - mx), jnp.exp(l2 - mx)
    inv = 1.0 / (e0 + e1 + e2)

    def per_lane(w):
        return jnp.dot(w.astype(BF16), ex_ref[...], preferred_element_type=F32)

    o_a = (per_lane(e0 * inv) * o0_ref[...].astype(F32) + per_lane(e1 * inv) * o1_ref[...].astype(F32)
           + per_lane(e2 * inv) * o2_ref[...].astype(F32))
    a_in = (o_a * _silu(za_ref[...].astype(F32))).astype(BF16)
    b_in = (ob_ref[...].astype(F32) * _silu(zb_ref[...].astype(F32))).astype(BF16)
    y_a = jnp.dot(a_in, wa_ref[...], preferred_element_type=F32)
    y_b = jnp.dot(b_in, wb_ref[...], preferred_element_type=F32)
    merged = _sigmoid(ga_ref[...].astype(F32)) * y_a + _sigmoid(gb_ref[...].astype(F32)) * y_b
    y = x_ref[...] + jnp.dot(merged.astype(BF16), wo_ref[...], preferred_element_type=F32)
    ms = jnp.mean(y * y, axis=-1, keepdims=True)
    out_ref[...] = y * lax.rsqrt(ms + NORM_EPS) * fw_ref[...]


def _out_proj(o_groups, lse_groups, gates, o_b, x2, w_up_a, w_up_b, w_out, final_w):
    rows = x2.shape[0]
    tm = min(512, rows)
    row128 = pl.BlockSpec((tm, LANES), lambda i: (i, 0))
    row512 = lambda c: pl.BlockSpec((tm, 512), lambda i: (i, c))
    row1024 = lambda c: pl.BlockSpec((tm, 1024), lambda i: (i, c))
    whole = lambda a: pl.BlockSpec(a.shape, lambda i: (0,) * a.ndim)
    expand = jnp.asarray(np.arange(LANES)[:, None] == LSE_REP * (np.arange(DSA_WIDTH)[None, :] // DSA_HEAD_DIM), BF16)
    return pl.pallas_call(
        _out_proj_kernel,
        grid=(rows // tm,),
        in_specs=[row512(0)] * 3 + [row128] * 3 + [
            row512(GATE_DSA_Z), row512(0), row512(GATE_GDN_Z), row1024(GATE_A), row1024(GATE_B), row1024(0),
            whole(w_up_a), whole(w_up_b), whole(w_out), whole(final_w), whole(expand)],
        out_specs=pl.BlockSpec((tm, D_MODEL), lambda i: (i, 0)),
        out_shape=jax.ShapeDtypeStruct((rows, D_MODEL), F32),
        compiler_params=pltpu.CompilerParams(
            dimension_semantics=("parallel",), vmem_limit_bytes=VMEM_LIMIT_BYTES),
        name="out_proj",
    )(*o_groups, *lse_groups, gates, o_b, gates, gates, gates, x2, w_up_a, w_up_b, w_out, final_w, expand)


QUAD = 4 * GDN_DIM
GDN_TILE_ROWS = 512


def _split_hi_lo(x):
    hi = x.astype(BF16)
    lo = (x - hi.astype(F32)).astype(BF16)
    return hi, lo


def _softplus(x):
    return jnp.maximum(x, 0.0) + jnp.log1p(jnp.exp(-jnp.abs(x)))


def _bmm(a, b):
    return lax.dot_general(a, b, (((2,), (1,)), ((0,), (0,))), preferred_element_type=F32)


def _bmm_nt(a, b):
    return lax.dot_general(a, b, (((2,), (2,)), ((0,), (0,))), preferred_element_type=F32)


def _block_diag(x, bd_mask):
    t = jnp.concatenate([x, x, x, x], axis=1)
    return jnp.where(bd_mask, t, jnp.zeros_like(t))


def _gdn_kernel(q_ref, k_ref, v_ref, lg_ref, alog_ref, dtb_ref, nw_ref, e512_ref, eb_ref, eg_ref, tri_ref, o_ref,
                beta_ref, gh_ref, gl_ref, lhs_ref, oloc_ref, snew_ref, dec_ref, oraw_ref, state_ref):
    first = pl.program_id(0) == 0
    batch, rows, _ = q_ref.shape
    tg = batch * rows
    nc = rows // GDN_CHUNK
    per_step = 2 * batch
    nb = per_step * nc

    @pl.when(first)
    def _():
        state_ref[...] = jnp.zeros_like(state_ref)

    logits = lg_ref[...].reshape(tg, LOGIT_PAD)
    beta_hi, beta_lo = _split_hi_lo(_sigmoid(logits))
    g_hi, g_lo = _split_hi_lo(-jnp.exp(alog_ref[...]) * _softplus(logits + dtb_ref[...]))
    beta_ref[...] = (jnp.dot(beta_hi, eb_ref[...], preferred_element_type=F32)
                     + jnp.dot(beta_lo, eb_ref[...], preferred_element_type=F32))
    gh_ref[...] = jnp.dot(g_hi, eg_ref[...], preferred_element_type=F32).astype(BF16)
    gl_ref[...] = jnp.dot(g_lo, eg_ref[...], preferred_element_type=F32).astype(BF16)

    row = lax.broadcasted_iota(jnp.int32, (GDN_CHUNK, QUAD), 0)
    col = lax.broadcasted_iota(jnp.int32, (GDN_CHUNK, QUAD), 1) % GDN_CHUNK
    incl = row >= col
    strict = row > col
    eye = (row == col).astype(F32)
    bd_mask = (lax.broadcasted_iota(jnp.int32, (QUAD, QUAD), 0) // GDN_DIM
               == lax.broadcasted_iota(jnp.int32, (QUAD, QUAD), 1) // GDN_DIM)
    tri = tri_ref[...]

    def chunk_quads(ref, dtype):
        a = ref[...].reshape(batch, nc, GDN_CHUNK, GDN_WIDTH).astype(dtype)
        a = jnp.stack([a[b, j] for j in range(nc) for b in range(batch)])
        return jnp.stack([a[:, :, :QUAD], a[:, :, QUAD:]], axis=1).reshape(nb, GDN_CHUNK, QUAD)

    q, k, v = (chunk_quads(r, F32) for r in (q_ref, k_ref, v_ref))
    beta = chunk_quads(beta_ref, F32)
    gh, gl = chunk_quads(gh_ref, BF16), chunk_quads(gl_ref, BF16)

    zero = jnp.zeros_like(gh)
    rhs_hi = jnp.concatenate([gh, jnp.where(strict, gh, zero)], axis=-1)
    rhs_lo = jnp.concatenate([gl, jnp.where(strict, gl, zero)], axis=-1)
    tri_b = jnp.broadcast_to(tri, (nb, GDN_CHUNK, GDN_CHUNK))
    gd = _bmm(tri_b, rhs_hi) + _bmm(tri_b, rhs_lo)
    g_cum, d_pair = gd[:, :, :QUAD], gd[:, :, QUAD:]
    decay_incl = jnp.where(incl, jnp.exp(d_pair), 0.0)
    decay_strict = jnp.where(strict, decay_incl, 0.0)
    exp_g = jnp.exp(g_cum)
    g_last = g_cum[:, GDN_CHUNK - 1:GDN_CHUNK, :]
    k_dec = k * jnp.exp(g_last - g_cum)
    k_beta = k * beta

    k_bd = _block_diag(k.astype(BF16), bd_mask)
    aa = _bmm_nt(jnp.concatenate([k_beta, q], axis=1).astype(BF16), k_bd)
    a = aa[:, :GDN_CHUNK] * decay_strict
    attn = aa[:, GDN_CHUNK:] * decay_incl

    p = a
    t = eye - a
    p = _bmm(p.astype(BF16), _block_diag(p.astype(BF16), bd_mask))
    for _ in range(4):
        r = _bmm(jnp.concatenate([p, t], axis=1).astype(BF16), _block_diag(p.astype(BF16), bd_mask))
        p = r[:, :GDN_CHUNK]
        t = t + r[:, GDN_CHUNK:]
    t = t + _bmm(t.astype(BF16), _block_diag(p.astype(BF16), bd_mask))
    t16 = t.astype(BF16)
    u = _bmm(t16, _block_diag((v * beta).astype(BF16), bd_mask))
    w = _bmm(t16, _block_diag((k_beta * exp_g).astype(BF16), bd_mask))

    w16, u16 = w.astype(BF16), u.astype(BF16)
    kt = _bmm(jnp.swapaxes(k_dec, 1, 2).astype(BF16), jnp.concatenate([w16, u16], axis=-1))
    ao = _bmm(attn.astype(BF16),
              jnp.concatenate([_block_diag(w16, bd_mask), _block_diag(u16, bd_mask)], axis=-1))
    lhs_ref[...] = jnp.concatenate(
        [q * exp_g - ao[:, :, :QUAD], jnp.where(bd_mask, -kt[:, :, :QUAD], 0.0)], axis=1).astype(BF16)
    oloc_ref[...] = ao[:, :, QUAD:]
    snew_ref[...] = jnp.where(bd_mask, kt[:, :, QUAD:], 0.0)
    dec_ref[...] = jnp.broadcast_to(jnp.exp(g_last), (nb, 8, QUAD))

    for j in range(nc):
        sl = slice(j * per_step, (j + 1) * per_step)
        state = state_ref[...]
        r = _bmm(lhs_ref[sl], state.astype(BF16))
        state_ref[...] = state * dec_ref[sl][:, 0:1, :] + r[:, GDN_CHUNK:] + snew_ref[sl]
        o = r[:, :GDN_CHUNK] + oloc_ref[sl]
        for b in range(batch):
            lo = b * rows + j * GDN_CHUNK
            oraw_ref[lo:lo + GDN_CHUNK, 0:QUAD] = o[2 * b]
            oraw_ref[lo:lo + GDN_CHUNK, QUAD:] = o[2 * b + 1]

    o_all = oraw_ref[...]
    ms = jnp.dot((o_all * o_all).astype(BF16), e512_ref[...], preferred_element_type=F32) * (1.0 / GDN_DIM)
    o_ref[...] = (o_all * lax.rsqrt(ms + NORM_EPS) * nw_ref[...]).astype(o_ref.dtype).reshape(batch, rows, GDN_WIDTH)


def _gdn(gdn_qkv, logits, alog_row, dtb_row, norm_row, batch, seq):
    tg = min(GDN_TILE_ROWS, batch * seq)
    rows = tg // batch
    steps = seq // rows
    nb = 2 * (tg // GDN_CHUNK)
    head_of_lane = np.arange(GDN_WIDTH) // GDN_DIM
    e512 = jnp.asarray(head_of_lane[:, None] == head_of_lane[None, :], BF16)
    e_beta = jnp.asarray(np.arange(LOGIT_PAD)[:, None] == head_of_lane[None, :], BF16)
    e_g = jnp.asarray(np.arange(LOGIT_PAD)[:, None] == head_of_lane[None, :] + GDN_HEADS, BF16)
    tri = jnp.asarray(np.tril(np.ones((GDN_CHUNK, GDN_CHUNK))), BF16)

    cur = lambda width, c: pl.BlockSpec((batch, rows, width), lambda n: (0, n, c))
    whole = lambda a: pl.BlockSpec(a.shape, lambda n: (0,) * a.ndim)
    consts = (alog_row, dtb_row, norm_row, e512, e_beta, e_g, tri)
    qkv3 = gdn_qkv.reshape(batch, seq, GROUP_COLS)
    out = pl.pallas_call(
        _gdn_kernel,
        grid=(steps,),
        in_specs=[cur(GDN_WIDTH, 0), cur(GDN_WIDTH, 1), cur(GDN_WIDTH, 2), cur(LOGIT_PAD, 0)]
        + [whole(a) for a in consts],
        out_specs=cur(GDN_WIDTH, 0),
        out_shape=jax.ShapeDtypeStruct((batch, seq, GDN_WIDTH), BF16),
        scratch_shapes=[
            pltpu.VMEM((tg, GDN_WIDTH), F32),
            pltpu.VMEM((tg, GDN_WIDTH), BF16),
            pltpu.VMEM((tg, GDN_WIDTH), BF16),
            pltpu.VMEM((nb, GDN_CHUNK + QUAD, QUAD), BF16),
            pltpu.VMEM((nb, GDN_CHUNK, QUAD), F32),
            pltpu.VMEM((nb, QUAD, QUAD), F32),
            pltpu.VMEM((nb, 8, QUAD), F32),
            pltpu.VMEM((tg, GDN_WIDTH), F32),
            pltpu.VMEM((2 * batch, QUAD, QUAD), F32),
        ],
        compiler_params=pltpu.CompilerParams(
            dimension_semantics=("arbitrary",), vmem_limit_bytes=VMEM_LIMIT_BYTES),
        name="gdn",
    )(qkv3, qkv3, qkv3, logits.reshape(batch, seq, LOGIT_PAD), *consts)
    return out.reshape(batch * seq, GDN_WIDTH)


def _rope_tables(seq):
    inv_freq = np.float32(ROPE_THETA) ** (-np.arange(0, DSA_HEAD_DIM, 2, dtype=np.float32) / np.float32(DSA_HEAD_DIM))
    ang = np.arange(seq, dtype=np.float32)[:, None] * inv_freq[None, :]
    cos, sin = np.cos(ang), np.sin(ang)
    return (jnp.asarray(np.concatenate([cos, cos, cos, cos], axis=-1), F32),
            jnp.asarray(np.concatenate([-sin, sin, -sin, sin], axis=-1), F32))


def kernel(x, norm_w, w_in, conv_w, a_log, dt_bias, gdn_norm_w, w_up_a, w_up_b, w_out, final_norm_w):
    batch, seq, _ = x.shape
    assert norm_w.shape[0] == 1, "the final RMSNorm is fused into the (single) layer's output kernel"
    cos_t, sin_t = _rope_tables(seq)
    x2 = x.reshape(batch * seq, D_MODEL)
    w = w_in[0]
    gates = LOGIT_START + 2 * GDN_HEADS
    dsa_z, gdn_z = QKV_COLS, GDN_IN_START + GROUP_COLS
    w_gates = jnp.concatenate(
        [w[:, gates:], w[:, dsa_z:GDN_IN_START], w[:, gdn_z:LOGIT_START],
         w[:, LOGIT_START:gates], jnp.zeros((D_MODEL, LOGIT_PAD - 2 * GDN_HEADS), w.dtype)], axis=1).astype(BF16)
    qkv, gdn_qkv, gate_cols, logits = _in_proj(x2, norm_w[0][None, :], w, w_gates, conv_w[0], cos_t, sin_t, seq)

    o_groups, lse_groups = [], []
    for g, (_, dilation) in enumerate(DSA_PATTERNS):
        o_g, lse_g = _dsa_attention(qkv, g, dilation, batch, seq)
        o_groups.append(o_g)
        lse_groups.append(lse_g)

    pad8 = lambda p: jnp.pad(p.astype(F32), (GDN_HEADS, LOGIT_PAD - 2 * GDN_HEADS))[None, :]
    o_b = _gdn(gdn_qkv, logits, pad8(a_log[0]), pad8(dt_bias[0]),
               jnp.tile(gdn_norm_w[0].astype(F32), GDN_HEADS)[None, :], batch, seq)

    out = _out_proj(o_groups, lse_groups, gate_cols, o_b, x2,
                    w_up_a[0].astype(BF16), w_up_b[0].astype(BF16), w_out[0].astype(BF16),
                    final_norm_w[None, :])
    return out.reshape(batch, seq, D_MODEL)
```

```python
import functools

import numpy as np
import jax
import jax.numpy as jnp
from jax import lax
from jax.experimental import pallas as pl
from jax.experimental.pallas import tpu as pltpu

F32 = jnp.float32
BF16 = jnp.bfloat16

D_MODEL = 1024
DSA_PATTERNS = ((128, 1), (512, 4), (2048, 16))
DSA_HEADS = 8
DSA_HEAD_DIM = 64
DSA_WIDTH = DSA_HEADS * DSA_HEAD_DIM
DSA_BLOCK = 128
ROPE_THETA = 10000.0
GDN_HEADS = 8
GDN_DIM = 64
GDN_WIDTH = GDN_HEADS * GDN_DIM
GDN_CONV = 4
GDN_CHUNK = 64
NORM_EPS = 1e-6

QKV_COLS = 3 * 3 * DSA_WIDTH
GROUP_COLS = 3 * DSA_WIDTH
GATE_COLS = 3072
GATE_A, GATE_B = 0, 1
GATE_DSA_Z, GATE_GDN_Z = 4, 5
GDN_IN_START = QKV_COLS + DSA_WIDTH
LOGIT_START = 7168
LOGIT_PAD = 128
PROJ_ROWS = 1024
OUT_ROWS = 1024
LOG2_E = 1.4426950408889634
LN_2 = 0.6931471805599453
DSA_Q_SCALE = DSA_HEAD_DIM ** -0.5 * LOG2_E

VMEM_LIMIT_BYTES = 56 * 1024 * 1024
LANES = 128
NEG_BIG = -1e30


def _lane_iota(shape):
    return lax.broadcasted_iota(jnp.int32, shape, len(shape) - 1)


def _in_proj_qkv_kernel(x_ref, nw_ref, w_ref, cos_ref, sin_ref, qkv_ref, h_ref):
    x = x_ref[...]
    ms = jnp.mean(x * x, axis=-1, keepdims=True)
    h = (x * lax.rsqrt(ms + NORM_EPS) * nw_ref[...]).astype(BF16)
    h_ref[...] = h
    cos = cos_ref[...]
    sin = sin_ref[...]
    low_half = (_lane_iota(cos.shape) % DSA_HEAD_DIM) < (DSA_HEAD_DIM // 2)
    for g in range(len(DSA_PATTERNS)):
        c0 = g * GROUP_COLS
        acc = jnp.dot(h, w_ref[:, c0:c0 + GROUP_COLS], preferred_element_type=F32)
        for c in range(8):
            t = acc[:, c * LANES:(c + 1) * LANES]
            rot = jnp.where(low_half, pltpu.roll(t, LANES - 32, 1), pltpu.roll(t, 32, 1))
            r = t * cos + rot * sin
            if c < 4:
                r = r * DSA_Q_SCALE
            qkv_ref[:, c0 + c * LANES:c0 + (c + 1) * LANES] = r.astype(BF16)
        qkv_ref[:, c0 + 8 * LANES:c0 + GROUP_COLS] = acc[:, 8 * LANES:].astype(BF16)


def _in_proj_rest_kernel(h_ref, wgq_ref, wgk_ref, wgv_ref, w_ref, wl_ref, cw_ref, e512_ref,
                         gdn_ref, gates_ref, logit_ref, xs_ref, *, tiles_per_seq):
    tm = h_ref.shape[0]
    first = pl.program_id(0) % tiles_per_seq == 0

    @pl.when(first)
    def _():
        xs_ref[0:8, :] = jnp.zeros((8, GROUP_COLS), F32)

    @pl.when(jnp.logical_not(first))
    def _():
        xs_ref[0:8, :] = xs_ref[tm:tm + 8, :]

    h = h_ref[...]
    logit_ref[...] = jnp.dot(h, wl_ref[...], preferred_element_type=F32)
    for part, wg_ref in enumerate((wgq_ref, wgk_ref, wgv_ref)):
        xs_ref[8:8 + tm, part * GDN_WIDTH:(part + 1) * GDN_WIDTH] = jnp.dot(
            h, wg_ref[...], preferred_element_type=F32)
    for g in range(GATE_COLS // GROUP_COLS):
        c0 = g * GROUP_COLS
        gates_ref[:, c0:c0 + GROUP_COLS] = jnp.dot(
            h, w_ref[:, c0:c0 + GROUP_COLS], preferred_element_type=F32).astype(BF16)
    for part in range(3):
        cols = slice(part * GDN_WIDTH, (part + 1) * GDN_WIDTH)
        w = cw_ref[:, cols]
        y = w[0:1, :] * xs_ref[5:5 + tm, cols]
        for j in range(1, GDN_CONV):
            y = y + w[j:j + 1, :] * xs_ref[5 + j:5 + j + tm, cols]
        y = _silu(y)
        if part < 2:
            ss = jnp.dot((y * y).astype(BF16), e512_ref[...], preferred_element_type=F32)
            y = y * (lax.rsqrt(ss + NORM_EPS) * (GDN_DIM ** -0.5 if part == 0 else 1.0))
        gdn_ref[:, cols] = y.astype(BF16)


def _resident(shape, col_block=0):
    return pl.BlockSpec(shape, lambda i: (0,) * (len(shape) - 1) + (col_block,), pipeline_mode=pl.Buffered(1))


def _in_proj(x2, norm_w, w_bf16, w_gates, conv_w, cos_t, sin_t, seq):
    assert GDN_IN_START % GDN_WIDTH == 0 and GATE_COLS % LOGIT_PAD == 0
    rows = x2.shape[0]
    tm = min(PROJ_ROWS, seq)
    n_seq_tiles = seq // tm
    row = lambda width: pl.BlockSpec((tm, width), lambda i: (i, 0))
    table = pl.BlockSpec((tm, LANES), lambda i: (i % n_seq_tiles, 0))
    head_of_lane = np.arange(GDN_WIDTH) // GDN_DIM
    e512 = jnp.asarray(head_of_lane[:, None] == head_of_lane[None, :], BF16)
    gdn_block = GDN_IN_START // GDN_WIDTH
    qkv, h = pl.pallas_call(
        _in_proj_qkv_kernel,
        grid=(rows // tm,),
        in_specs=[row(D_MODEL), _resident((1, D_MODEL)), _resident((D_MODEL, QKV_COLS)), table, table],
        out_specs=[row(QKV_COLS), row(D_MODEL)],
        out_shape=[jax.ShapeDtypeStruct((rows, QKV_COLS), BF16), jax.ShapeDtypeStruct((rows, D_MODEL), BF16)],
        compiler_params=pltpu.CompilerParams(dimension_semantics=("parallel",), vmem_limit_bytes=VMEM_LIMIT_BYTES),
        name="in_proj_qkv",
    )(x2, norm_w, w_bf16, cos_t, sin_t)
    gdn_qkv, gates, logits = pl.pallas_call(
        functools.partial(_in_proj_rest_kernel, tiles_per_seq=n_seq_tiles),
        grid=(rows // tm,),
        in_specs=[row(D_MODEL), _resident((D_MODEL, GDN_WIDTH), gdn_block),
                  _resident((D_MODEL, GDN_WIDTH), gdn_block + 1), _resident((D_MODEL, GDN_WIDTH), gdn_block + 2),
                  _resident((D_MODEL, GATE_COLS)), _resident((D_MODEL, LOGIT_PAD), GATE_COLS // LOGIT_PAD),
                  _resident(conv_w.shape), _resident(e512.shape)],
        out_specs=[row(GROUP_COLS), row(GATE_COLS), row(LOGIT_PAD)],
        out_shape=[jax.ShapeDtypeStruct((rows, GROUP_COLS), BF16), jax.ShapeDtypeStruct((rows, GATE_COLS), BF16),
                   jax.ShapeDtypeStruct((rows, LOGIT_PAD), F32)],
        scratch_shapes=[pltpu.VMEM((tm + 8, GROUP_COLS), F32)],
        compiler_params=pltpu.CompilerParams(
            dimension_semantics=("arbitrary",), vmem_limit_bytes=VMEM_LIMIT_BYTES),
        name="in_proj_rest",
    )(h, w_bf16, w_bf16, w_bf16, w_gates, w_gates, conv_w, e512)
    return qkv, gdn_qkv, gates, logits


DSA_SPAN = 2048
LSE_REP = LANES // DSA_HEADS


def _dsa_kernel(q_ref, k_ref, v_ref, o_ref, lse_ref, stage_ref, qp_ref, kp_ref, vp_ref, lstage_ref, tmp_ref=None,
                *, d, span):
    first = pl.program_id(1) == 0
    npr = span // d
    n_sub = npr // DSA_BLOCK
    kstride = npr + DSA_BLOCK

    def regroup(src_ref, dst_ref, dst_stride, dst_off):
        if d == 1:
            dst_ref[dst_off:dst_off + npr, :] = src_ref[...]
            return
        for c in range(4):
            stage_ref[c] = src_ref[:, c * LANES:(c + 1) * LANES].astype(F32)
        if d > 4:
            quarter = span // 4
            for c in range(4):
                for r4 in range(4):
                    tmp_ref[c, r4 * quarter:(r4 + 1) * quarter, :] = stage_ref[c, pl.ds(r4, quarter, stride=4), :]
            slabs, stride, start = tmp_ref, d // 4, lambda r: (r % 4) * quarter + r // 4
        else:
            slabs, stride, start = stage_ref, d, lambda r: r
        for c in range(4):
            for r in range(d):
                lo = r * dst_stride + dst_off
                dst_ref[lo:lo + npr, c * LANES:(c + 1) * LANES] = (
                    slabs[c, pl.ds(start(r), npr, stride=stride), :].astype(BF16))

    @pl.when(first)
    def _():
        for r in range(d):
            kp_ref[r * kstride:r * kstride + DSA_BLOCK, :] = jnp.zeros((DSA_BLOCK, DSA_WIDTH), BF16)
            vp_ref[r * kstride:r * kstride + DSA_BLOCK, :] = jnp.zeros((DSA_BLOCK, DSA_WIDTH), BF16)

    regroup(q_ref, qp_ref, npr, 0)
    regroup(k_ref, kp_ref, kstride, DSA_BLOCK)
    regroup(v_ref, vp_ref, kstride, DSA_BLOCK)

    qi = lax.broadcasted_iota(jnp.int32, (DSA_BLOCK, 2 * DSA_BLOCK), 0)
    kj = lax.broadcasted_iota(jnp.int32, (DSA_BLOCK, 2 * DSA_BLOCK), 1)
    band = (kj >= qi) & (kj <= qi + DSA_BLOCK)
    head_a = _lane_iota((DSA_BLOCK, LANES)) < DSA_HEAD_DIM
    lse_head = _lane_iota((DSA_BLOCK, LANES)) // LSE_REP

    def block(blk, carry):
        r, i = blk // n_sub, blk % n_sub
        q0 = pl.multiple_of(r * npr + i * DSA_BLOCK, DSA_BLOCK)
        k0 = pl.multiple_of(r * kstride + i * DSA_BLOCK, DSA_BLOCK)
        nat0 = r + d * DSA_BLOCK * i
        mask = band & (kj >= jnp.where(first & (i == 0), DSA_BLOCK, 0))
        m_tile = jnp.zeros((DSA_BLOCK, LANES), F32)
        den_tile = jnp.ones((DSA_BLOCK, LANES), F32)
        for hp in range(DSA_HEADS // 2):
            cols = slice(hp * LANES, (hp + 1) * LANES)
            q2 = qp_ref[pl.ds(q0, DSA_BLOCK), cols]
            k2 = kp_ref[pl.ds(k0, 2 * DSA_BLOCK), cols]
            v2 = vp_ref[pl.ds(k0, 2 * DSA_BLOCK), cols]
            pvs, ms, dens = [], [], []
            for is_a in (True, False):
                sel = head_a if is_a else jnp.logical_not(head_a)
                qh = jnp.where(sel, q2, jnp.zeros_like(q2))
                s = lax.dot_general(qh, k2, (((1,), (1,)), ((), ())), preferred_element_type=F32)
                s = jnp.where(mask, s, NEG_BIG)
                m = jnp.max(s, axis=-1, keepdims=True)
                p = jnp.exp2(s - m)
                dens.append(jnp.sum(p, axis=-1, keepdims=True))
                pvs.append(jnp.dot(p.astype(BF16), v2, preferred_element_type=F32))
                ms.append(m)
            o_pair = jnp.where(head_a, pvs[0], pvs[1]) / jnp.where(head_a, dens[0], dens[1])
            den_tile = jnp.where(lse_head == 2 * hp, dens[0], jnp.where(lse_head == 2 * hp + 1, dens[1], den_tile))
            m_tile = jnp.where(lse_head == 2 * hp, ms[0], jnp.where(lse_head == 2 * hp + 1, ms[1], m_tile))
            if d == 1:
                o_ref[pl.ds(q0, DSA_BLOCK), cols] = o_pair.astype(o_ref.dtype)
            else:
                stage_ref[hp, pl.ds(nat0, DSA_BLOCK, stride=d), :] = o_pair
        lse_tile = m_tile * LN_2 + jnp.log(den_tile)
        if d == 1:
            lse_ref[pl.ds(q0, DSA_BLOCK), :] = lse_tile
        else:
            lstage_ref[pl.ds(nat0, DSA_BLOCK, stride=d), :] = lse_tile
        return carry

    lax.fori_loop(0, d * n_sub, block, 0, unroll=16)

    for r in range(d):
        lo = r * kstride
        kp_ref[lo:lo + DSA_BLOCK, :] = kp_ref[lo + npr:lo + npr + DSA_BLOCK, :]
        vp_ref[lo:lo + DSA_BLOCK, :] = vp_ref[lo + npr:lo + npr + DSA_BLOCK, :]
    if d > 1:
        for c in range(4):
            o_ref[:, c * LANES:(c + 1) * LANES] = stage_ref[c].astype(o_ref.dtype)
        lse_ref[...] = lstage_ref[...]


def _dsa_attention(qkv, group, dilation, batch, seq):
    span = min(DSA_SPAN, seq)
    steps = seq // span
    kv_rows = span + dilation * DSA_BLOCK
    col = lambda t: pl.BlockSpec((span, 512), lambda b, n: (b * steps + n, group * 3 + t))
    row = lambda width: pl.BlockSpec((span, width), lambda b, n: (b * steps + n, 0))
    return pl.pallas_call(
        functools.partial(_dsa_kernel, d=dilation, span=span),
        grid=(batch, steps),
        in_specs=[col(0), col(1), col(2)],
        out_specs=[row(DSA_WIDTH), row(LANES)],
        out_shape=[
            jax.ShapeDtypeStruct((batch * seq, DSA_WIDTH), BF16),
            jax.ShapeDtypeStruct((batch * seq, LANES), F32),
        ],
        scratch_shapes=[
            pltpu.VMEM((4, span, LANES), F32),
            pltpu.VMEM((span, DSA_WIDTH), BF16),
            pltpu.VMEM((kv_rows, DSA_WIDTH), BF16),
            pltpu.VMEM((kv_rows, DSA_WIDTH), BF16),
            pltpu.VMEM((span, LANES), F32),
        ] + ([pltpu.VMEM((4, span, LANES), F32)] if dilation > 4 else []),
        compiler_params=pltpu.CompilerParams(
            dimension_semantics=("parallel", "arbitrary"), vmem_limit_bytes=VMEM_LIMIT_BYTES),
        name=f"dsa_attn_d{dilation}",
    )(qkv, qkv, qkv)


def _silu(x):
    return x * (1.0 / (1.0 + jnp.exp(-x)))


def _sigmoid(x):
    return 1.0 / (1.0 + jnp.exp(-x))


def _out_proj_kernel(o0_ref, o1_ref, o2_ref, l0_ref, l1_ref, l2_ref, za_ref, ob_ref, zb_ref, ga_ref, gb_ref,
                     x_ref, wa_ref, wb_ref, wo_ref, fw_ref, ex_ref, out_ref):
    l0, l1, l2 = l0_ref[...], l1_ref[...], l2_ref[...]
    mx = jnp.maximum(jnp.maximum(l0, l1), l2)
    e0, e1, e2 = jnp.exp(l0 - mx), jnp.exp(l1 - mx), jnp.exp(l2 - mx)
    inv = 1.0 / (e0 + e1 + e2)

    def per_lane(w):
        return jnp.dot(w.astype(BF16), ex_ref[...], preferred_element_type=F32)

    o_a = (per_lane(e0 * inv) * o0_ref[...].astype(F32) + per_lane(e1 * inv) * o1_ref[...].astype(F32)
           + per_lane(e2 * inv) * o2_ref[...].astype(F32))
    a_in = (o_a * _silu(za_ref[...].astype(F32))).astype(BF16)
    b_in = (ob_ref[...].astype(F32) * _silu(zb_ref[...].astype(F32))).astype(BF16)
    y_a = jnp.dot(a_in, wa_ref[...], preferred_element_type=F32)
    y_b = jnp.dot(b_in, wb_ref[...], preferred_element_type=F32)
    merged = _sigmoid(ga_ref[...].astype(F32)) * y_a + _sigmoid(gb_ref[...].astype(F32)) * y_b
    y = x_ref[...] + jnp.dot(merged.astype(BF16), wo_ref[...], preferred_element_type=F32)
    ms = jnp.mean(y * y, axis=-1, keepdims=True)
    out_ref[...] = y * lax.rsqrt(ms + NORM_EPS) * fw_ref[...]


def _out_proj(o_groups, lse_groups, gates, o_b, x2, w_up_a, w_up_b, w_out, final_w):
    rows = x2.shape[0]
    tm = min(OUT_ROWS, rows)
    row128 = pl.BlockSpec((tm, LANES), lambda i: (i, 0))
    row512 = lambda c: pl.BlockSpec((tm, 512), lambda i: (i, c))
    row1024 = lambda c: pl.BlockSpec((tm, 1024), lambda i: (i, c))
    whole = lambda a: pl.BlockSpec(a.shape, lambda i: (0,) * a.ndim)
    expand = jnp.asarray(np.arange(LANES)[:, None] == LSE_REP * (np.arange(DSA_WIDTH)[None, :] // DSA_HEAD_DIM), BF16)
    return pl.pallas_call(
        _out_proj_kernel,
        grid=(rows // tm,),
        in_specs=[row512(0)] * 3 + [row128] * 3 + [
            row512(GATE_DSA_Z), row512(0), row512(GATE_GDN_Z), row1024(GATE_A), row1024(GATE_B), row1024(0),
            whole(w_up_a), whole(w_up_b), whole(w_out), whole(final_w), whole(expand)],
        out_specs=pl.BlockSpec((tm, D_MODEL), lambda i: (i, 0)),
        out_shape=jax.ShapeDtypeStruct((rows, D_MODEL), F32),
        compiler_params=pltpu.CompilerParams(
            dimension_semantics=("parallel",), vmem_limit_bytes=VMEM_LIMIT_BYTES),
        name="out_proj",
    )(*o_groups, *lse_groups, gates, o_b, gates, gates, gates, x2, w_up_a, w_up_b, w_out, final_w, expand)


QUAD = 4 * GDN_DIM
GDN_TILE_ROWS = 1024


def _split_hi_lo(x):
    hi = x.astype(BF16)
    lo = (x - hi.astype(F32)).astype(BF16)
    return hi, lo


def _softplus(x):
    return jnp.maximum(x, 0.0) + jnp.log1p(jnp.exp(-jnp.abs(x)))


def _bmm(a, b):
    return lax.dot_general(a, b, (((2,), (1,)), ((0,), (0,))), preferred_element_type=F32)


def _bmm_nt(a, b):
    return lax.dot_general(a, b, (((2,), (2,)), ((0,), (0,))), preferred_element_type=F32)


def _block_diag(x, bd_mask):
    t = jnp.concatenate([x, x, x, x], axis=1)
    return jnp.where(bd_mask, t, jnp.zeros_like(t))


def _gdn_kernel(q_ref, k_ref, v_ref, lg_ref, alog_ref, dtb_ref, nw_ref, e512_ref, eb_ref, eg_ref, tri_ref, o_ref,
                beta_ref, gh_ref, gl_ref, lhs_ref, oloc_ref, snew_ref, dec_ref, oraw_ref, state_ref):
    first = pl.program_id(0) == 0
    batch, rows, _ = q_ref.shape
    tg = batch * rows
    nc = rows // GDN_CHUNK
    per_step = 2 * batch
    nb = per_step * nc

    @pl.when(first)
    def _():
        state_ref[...] = jnp.zeros_like(state_ref)

    logits = lg_ref[...].reshape(tg, LOGIT_PAD)
    beta_hi, beta_lo = _split_hi_lo(_sigmoid(logits))
    g_hi, g_lo = _split_hi_lo(-jnp.exp(alog_ref[...]) * _softplus(logits + dtb_ref[...]))
    beta_ref[...] = (jnp.dot(beta_hi, eb_ref[...], preferred_element_type=F32)
                     + jnp.dot(beta_lo, eb_ref[...], preferred_element_type=F32))
    gh_ref[...] = jnp.dot(g_hi, eg_ref[...], preferred_element_type=F32).astype(BF16)
    gl_ref[...] = jnp.dot(g_lo, eg_ref[...], preferred_element_type=F32).astype(BF16)

    row = lax.broadcasted_iota(jnp.int32, (GDN_CHUNK, QUAD), 0)
    col = lax.broadcasted_iota(jnp.int32, (GDN_CHUNK, QUAD), 1) % GDN_CHUNK
    incl = row >= col
    strict = row > col
    eye = (row == col).astype(F32)
    bd_mask = (lax.broadcasted_iota(jnp.int32, (QUAD, QUAD), 0) // GDN_DIM
               == lax.broadcasted_iota(jnp.int32, (QUAD, QUAD), 1) // GDN_DIM)
    tri = tri_ref[...]

    def chunk_quads(ref, dtype):
        a = ref[...].reshape(batch, nc, GDN_CHUNK, GDN_WIDTH).astype(dtype)
        a = jnp.stack([a[b, j] for j in range(nc) for b in range(batch)])
        return jnp.stack([a[:, :, :QUAD], a[:, :, QUAD:]], axis=1).reshape(nb, GDN_CHUNK, QUAD)

    q, k, v = (chunk_quads(r, F32) for r in (q_ref, k_ref, v_ref))
    beta = chunk_quads(beta_ref, F32)
    gh, gl = chunk_quads(gh_ref, BF16), chunk_quads(gl_ref, BF16)

    zero = jnp.zeros_like(gh)
    rhs_hi = jnp.concatenate([gh, jnp.where(strict, gh, zero)], axis=-1)
    rhs_lo = jnp.concatenate([gl, jnp.where(strict, gl, zero)], axis=-1)
    tri_b = jnp.broadcast_to(tri, (nb, GDN_CHUNK, GDN_CHUNK))
    gd = _bmm(tri_b, rhs_hi) + _bmm(tri_b, rhs_lo)
    g_cum, d_pair = gd[:, :, :QUAD], gd[:, :, QUAD:]
    decay_incl = jnp.where(incl, jnp.exp(d_pair), 0.0)
    decay_strict = jnp.where(strict, decay_incl, 0.0)
    exp_g = jnp.exp(g_cum)
    g_last = g_cum[:, GDN_CHUNK - 1:GDN_CHUNK, :]
    k_dec = k * jnp.exp(g_last - g_cum)
    k_beta = k * beta

    k_bd = _block_diag(k.astype(BF16), bd_mask)
    aa = _bmm_nt(jnp.concatenate([k_beta, q], axis=1).astype(BF16), k_bd)
    a = aa[:, :GDN_CHUNK] * decay_strict
    attn = aa[:, GDN_CHUNK:] * decay_incl

    p = a
    t = eye - a
    p = _bmm(p.astype(BF16), _block_diag(p.astype(BF16), bd_mask))
    for _ in range(4):
        r = _bmm(jnp.concatenate([p, t], axis=1).astype(BF16), _block_diag(p.astype(BF16), bd_mask))
        p = r[:, :GDN_CHUNK]
        t = t + r[:, GDN_CHUNK:]
    t = t + _bmm(t.astype(BF16), _block_diag(p.astype(BF16), bd_mask))
    t16 = t.astype(BF16)
    u = _bmm(t16, _block_diag((v * beta).astype(BF16), bd_mask))
    w = _bmm(t16, _block_diag((k_beta * exp_g).astype(BF16), bd_mask))

    w16, u16 = w.astype(BF16), u.astype(BF16)
    kt = _bmm(jnp.swapaxes(k_dec, 1, 2).astype(BF16), jnp.concatenate([w16, u16], axis=-1))
    ao = _bmm(attn.astype(BF16),
              jnp.concatenate([_block_diag(w16, bd_mask), _block_diag(u16, bd_mask)], axis=-1))
    lhs_ref[...] = jnp.concatenate(
        [q * exp_g - ao[:, :, :QUAD], jnp.where(bd_mask, -kt[:, :, :QUAD], 0.0)], axis=1).astype(BF16)
    oloc_ref[...] = ao[:, :, QUAD:]
    snew_ref[...] = jnp.where(bd_mask, kt[:, :, QUAD:], 0.0)
    dec_ref[...] = jnp.broadcast_to(jnp.exp(g_last), (nb, 8, QUAD))

    for j in range(nc):
        sl = slice(j * per_step, (j + 1) * per_step)
        state = state_ref[...]
        r = _bmm(lhs_ref[sl], state.astype(BF16))
        state_ref[...] = state * dec_ref[sl][:, 0:1, :] + r[:, GDN_CHUNK:] + snew_ref[sl]
        o = r[:, :GDN_CHUNK] + oloc_ref[sl]
        for b in range(batch):
            lo = b * rows + j * GDN_CHUNK
            oraw_ref[lo:lo + GDN_CHUNK, 0:QUAD] = o[2 * b]
            oraw_ref[lo:lo + GDN_CHUNK, QUAD:] = o[2 * b + 1]

    o_all = oraw_ref[...]
    ms = jnp.dot((o_all * o_all).astype(BF16), e512_ref[...], preferred_element_type=F32) * (1.0 / GDN_DIM)
    o_ref[...] = (o_all * lax.rsqrt(ms + NORM_EPS) * nw_ref[...]).astype(o_ref.dtype).reshape(batch, rows, GDN_WIDTH)


def _gdn(gdn_qkv, logits, alog_row, dtb_row, norm_row, batch, seq):
    tg = min(GDN_TILE_ROWS, batch * seq)
    rows = tg // batch
    steps = seq // rows
    nb = 2 * (tg // GDN_CHUNK)
    head_of_lane = np.arange(GDN_WIDTH) // GDN_DIM
    e512 = jnp.asarray(head_of_lane[:, None] == head_of_lane[None, :], BF16)
    e_beta = jnp.asarray(np.arange(LOGIT_PAD)[:, None] == head_of_lane[None, :], BF16)
    e_g = jnp.asarray(np.arange(LOGIT_PAD)[:, None] == head_of_lane[None, :] + GDN_HEADS, BF16)
    tri = jnp.asarray(np.tril(np.ones((GDN_CHUNK, GDN_CHUNK))), BF16)

    cur = lambda width, c: pl.BlockSpec((batch, rows, width), lambda n: (0, n, c))
    whole = lambda a: pl.BlockSpec(a.shape, lambda n: (0,) * a.ndim)
    consts = (alog_row, dtb_row, norm_row, e512, e_beta, e_g, tri)
    qkv3 = gdn_qkv.reshape(batch, seq, GROUP_COLS)
    out = pl.pallas_call(
        _gdn_kernel,
        grid=(steps,),
        in_specs=[cur(GDN_WIDTH, 0), cur(GDN_WIDTH, 1), cur(GDN_WIDTH, 2), cur(LOGIT_PAD, 0)]
        + [whole(a) for a in consts],
        out_specs=cur(GDN_WIDTH, 0),
        out_shape=jax.ShapeDtypeStruct((batch, seq, GDN_WIDTH), BF16),
        scratch_shapes=[
            pltpu.VMEM((tg, GDN_WIDTH), F32),
            pltpu.VMEM((tg, GDN_WIDTH), BF16),
            pltpu.VMEM((tg, GDN_WIDTH), BF16),
            pltpu.VMEM((nb, GDN_CHUNK + QUAD, QUAD), BF16),
            pltpu.VMEM((nb, GDN_CHUNK, QUAD), F32),
            pltpu.VMEM((nb, QUAD, QUAD), F32),
            pltpu.VMEM((nb, 8, QUAD), F32),
            pltpu.VMEM((tg, GDN_WIDTH), F32),
            pltpu.VMEM((2 * batch, QUAD, QUAD), F32),
        ],
        compiler_params=pltpu.CompilerParams(
            dimension_semantics=("arbitrary",), vmem_limit_bytes=VMEM_LIMIT_BYTES),
        name="gdn",
    )(qkv3, qkv3, qkv3, logits.reshape(batch, seq, LOGIT_PAD), *consts)
    return out.reshape(batch * seq, GDN_WIDTH)


def _rope_tables(seq):
    inv_freq = np.float32(ROPE_THETA) ** (-np.arange(0, DSA_HEAD_DIM, 2, dtype=np.float32) / np.float32(DSA_HEAD_DIM))
    ang = np.arange(seq, dtype=np.float32)[:, None] * inv_freq[None, :]
    cos, sin = np.cos(ang), np.sin(ang)
    return (jnp.asarray(np.concatenate([cos, cos, cos, cos], axis=-1), F32),
            jnp.asarray(np.concatenate([-sin, sin, -sin, sin], axis=-1), F32))


def kernel(x, norm_w, w_in, conv_w, a_log, dt_bias, gdn_norm_w, w_up_a, w_up_b, w_out, final_norm_w):
    batch, seq, _ = x.shape
    assert norm_w.shape[0] == 1, "the final RMSNorm is fused into the (single) layer's output kernel"
    cos_t, sin_t = _rope_tables(seq)
    x2 = x.reshape(batch * seq, D_MODEL)
    w = w_in[0].astype(BF16)
    gates = LOGIT_START + 2 * GDN_HEADS
    dsa_z, gdn_z = QKV_COLS, GDN_IN_START + GROUP_COLS
    w_gates = jnp.concatenate(
        [w[:, gates:], w[:, dsa_z:GDN_IN_START], w[:, gdn_z:LOGIT_START],
         w[:, LOGIT_START:gates], jnp.zeros((D_MODEL, LOGIT_PAD - 2 * GDN_HEADS), BF16)], axis=1)
    qkv, gdn_qkv, gate_cols, logits = _in_proj(x2, norm_w[0][None, :], w, w_gates, conv_w[0], cos_t, sin_t, seq)

    o_groups, lse_groups = [], []
    for g, (_, dilation) in enumerate(DSA_PATTERNS):
        o_g, lse_g = _dsa_attention(qkv, g, dilation, batch, seq)
        o_groups.append(o_g)
        lse_groups.append(lse_g)

    pad8 = lambda p: jnp.pad(p.astype(F32), (GDN_HEADS, LOGIT_PAD - 2 * GDN_HEADS))[None, :]
    o_b = _gdn(gdn_qkv, logits, pad8(a_log[0]), pad8(dt_bias[0]),
               jnp.tile(gdn_norm_w[0].astype(F32), GDN_HEADS)[None, :], batch, seq)

    out = _out_proj(o_groups, lse_groups, gate_cols, o_b, x2,
                    w_up_a[0].astype(BF16), w_up_b[0].astype(BF16), w_out[0].astype(BF16),
                    final_norm_w[None, :])
    return out.reshape(batch, seq, D_MODEL)
```

```python
import functools

import numpy as np
import jax
import jax.numpy as jnp
from jax import lax
from jax.experimental import pallas as pl
from jax.experimental.pallas import tpu as pltpu

F32 = jnp.float32
BF16 = jnp.bfloat16

D_MODEL = 1024
DSA_PATTERNS = ((128, 1), (512, 4), (2048, 16))
DSA_HEADS = 8
DSA_HEAD_DIM = 64
DSA_WIDTH = DSA_HEADS * DSA_HEAD_DIM
DSA_BLOCK = 128
ROPE_THETA = 10000.0
GDN_HEADS = 8
GDN_DIM = 64
GDN_WIDTH = GDN_HEADS * GDN_DIM
GDN_CONV = 4
GDN_CHUNK = 64
NORM_EPS = 1e-6

QKV_COLS = 3 * 3 * DSA_WIDTH
GROUP_COLS = 3 * DSA_WIDTH
GATE_COLS = 3072
GATE_A, GATE_B = 0, 1
GATE_DSA_Z, GATE_GDN_Z = 4, 5
GDN_IN_START = QKV_COLS + DSA_WIDTH
LOGIT_START = 7168
LOGIT_PAD = 128
PROJ_ROWS = 1024
OUT_ROWS = 1024
LOG2_E = 1.4426950408889634
LN_2 = 0.6931471805599453
DSA_Q_SCALE = DSA_HEAD_DIM ** -0.5 * LOG2_E

VMEM_LIMIT_BYTES = 56 * 1024 * 1024
LANES = 128
NEG_BIG = -1e30


def _lane_iota(shape):
    return lax.broadcasted_iota(jnp.int32, shape, len(shape) - 1)


def _in_proj_qkv_kernel(x_ref, nw_ref, w_ref, cos_ref, sin_ref, qkv0_ref, qkv1_ref, qkv2_ref, h_ref):
    x = x_ref[...]
    ms = jnp.mean(x * x, axis=-1, keepdims=True)
    h = (x * lax.rsqrt(ms + NORM_EPS) * nw_ref[...]).astype(BF16)
    h_ref[...] = h
    cos = cos_ref[...]
    sin = sin_ref[...]
    low_half = (_lane_iota(cos.shape) % DSA_HEAD_DIM) < (DSA_HEAD_DIM // 2)
    for g, qkv_ref in enumerate((qkv0_ref, qkv1_ref, qkv2_ref)):
        c0 = g * GROUP_COLS
        acc = jnp.dot(h, w_ref[:, c0:c0 + GROUP_COLS], preferred_element_type=F32)
        for c in range(8):
            t = acc[:, c * LANES:(c + 1) * LANES]
            rot = jnp.where(low_half, pltpu.roll(t, LANES - 32, 1), pltpu.roll(t, 32, 1))
            r = t * cos + rot * sin
            if c < 4:
                r = r * DSA_Q_SCALE
            qkv_ref[:, c * LANES:(c + 1) * LANES] = r.astype(BF16)
        qkv_ref[:, 8 * LANES:] = acc[:, 8 * LANES:].astype(BF16)


def _in_proj_rest_kernel(h_ref, wgq_ref, wgk_ref, wgv_ref, w_ref, wl_ref, cw_ref, e512_ref,
                         gdn_ref, gates_ref, logit_ref, xs_ref, *, tiles_per_seq):
    tm = h_ref.shape[0]
    first = pl.program_id(0) % tiles_per_seq == 0

    @pl.when(first)
    def _():
        xs_ref[0:8, :] = jnp.zeros((8, GROUP_COLS), F32)

    @pl.when(jnp.logical_not(first))
    def _():
        xs_ref[0:8, :] = xs_ref[tm:tm + 8, :]

    h = h_ref[...]
    logit_ref[...] = jnp.dot(h, wl_ref[...], preferred_element_type=F32)
    for part, wg_ref in enumerate((wgq_ref, wgk_ref, wgv_ref)):
        xs_ref[8:8 + tm, part * GDN_WIDTH:(part + 1) * GDN_WIDTH] = jnp.dot(
            h, wg_ref[...], preferred_element_type=F32)
    for g in range(GATE_COLS // GROUP_COLS):
        c0 = g * GROUP_COLS
        gates_ref[:, c0:c0 + GROUP_COLS] = jnp.dot(
            h, w_ref[:, c0:c0 + GROUP_COLS], preferred_element_type=F32).astype(BF16)
    for part in range(3):
        cols = slice(part * GDN_WIDTH, (part + 1) * GDN_WIDTH)
        w = cw_ref[:, cols]
        y = w[0:1, :] * xs_ref[5:5 + tm, cols]
        for j in range(1, GDN_CONV):
            y = y + w[j:j + 1, :] * xs_ref[5 + j:5 + j + tm, cols]
        y = _silu(y)
        if part < 2:
            ss = jnp.dot((y * y).astype(BF16), e512_ref[...], preferred_element_type=F32)
            y = y * (lax.rsqrt(ss + NORM_EPS) * (GDN_DIM ** -0.5 if part == 0 else 1.0))
        gdn_ref[:, cols] = y.astype(BF16)


def _resident(shape, col_block=0):
    return pl.BlockSpec(shape, lambda i: (0,) * (len(shape) - 1) + (col_block,), pipeline_mode=pl.Buffered(1))


def _in_proj(x2, norm_w, w_bf16, w_gates, conv_w, cos_t, sin_t, seq):
    assert GDN_IN_START % GDN_WIDTH == 0 and GATE_COLS % LOGIT_PAD == 0
    rows = x2.shape[0]
    tm = min(PROJ_ROWS, seq)
    n_seq_tiles = seq // tm
    row = lambda width: pl.BlockSpec((tm, width), lambda i: (i, 0))
    table = pl.BlockSpec((tm, LANES), lambda i: (i % n_seq_tiles, 0))
    head_of_lane = np.arange(GDN_WIDTH) // GDN_DIM
    e512 = jnp.asarray(head_of_lane[:, None] == head_of_lane[None, :], BF16)
    gdn_block = GDN_IN_START // GDN_WIDTH
    *qkv, h = pl.pallas_call(
        _in_proj_qkv_kernel,
        grid=(rows // tm,),
        in_specs=[row(D_MODEL), _resident((1, D_MODEL)), _resident((D_MODEL, QKV_COLS)), table, table],
        out_specs=[row(GROUP_COLS)] * len(DSA_PATTERNS) + [row(D_MODEL)],
        out_shape=[jax.ShapeDtypeStruct((rows, GROUP_COLS), BF16)] * len(DSA_PATTERNS)
        + [jax.ShapeDtypeStruct((rows, D_MODEL), BF16)],
        compiler_params=pltpu.CompilerParams(dimension_semantics=("parallel",), vmem_limit_bytes=VMEM_LIMIT_BYTES),
        name="in_proj_qkv",
    )(x2, norm_w, w_bf16, cos_t, sin_t)
    gdn_qkv, gates, logits = pl.pallas_call(
        functools.partial(_in_proj_rest_kernel, tiles_per_seq=n_seq_tiles),
        grid=(rows // tm,),
        in_specs=[row(D_MODEL), _resident((D_MODEL, GDN_WIDTH), gdn_block),
                  _resident((D_MODEL, GDN_WIDTH), gdn_block + 1), _resident((D_MODEL, GDN_WIDTH), gdn_block + 2),
                  _resident((D_MODEL, GATE_COLS)), _resident((D_MODEL, LOGIT_PAD), GATE_COLS // LOGIT_PAD),
                  _resident(conv_w.shape), _resident(e512.shape)],
        out_specs=[row(GROUP_COLS), row(GATE_COLS), row(LOGIT_PAD)],
        out_shape=[jax.ShapeDtypeStruct((rows, GROUP_COLS), BF16), jax.ShapeDtypeStruct((rows, GATE_COLS), BF16),
                   jax.ShapeDtypeStruct((rows, LOGIT_PAD), F32)],
        scratch_shapes=[pltpu.VMEM((tm + 8, GROUP_COLS), F32)],
        compiler_params=pltpu.CompilerParams(
            dimension_semantics=("arbitrary",), vmem_limit_bytes=VMEM_LIMIT_BYTES),
        name="in_proj_rest",
    )(h, w_bf16, w_bf16, w_bf16, w_gates, w_gates, conv_w, e512)
    return qkv, gdn_qkv, gates, logits


DSA_SPAN = 2048
LSE_REP = LANES // DSA_HEADS


def _dsa_kernel(qkv_ref, o_ref, lse_ref, stage_ref, qp_ref, kp_ref, vp_ref, lstage_ref, tmp_ref=None,
                *, d, span):
    first = pl.program_id(1) == 0
    npr = span // d
    n_sub = npr // DSA_BLOCK
    kstride = npr + DSA_BLOCK

    def regroup(part, dst_ref, dst_stride, dst_off):
        c_in = part * DSA_WIDTH
        if d == 1:
            dst_ref[dst_off:dst_off + npr, :] = qkv_ref[:, c_in:c_in + DSA_WIDTH]
            return
        for c in range(4):
            stage_ref[c] = qkv_ref[:, c_in + c * LANES:c_in + (c + 1) * LANES].astype(F32)
        if d > 4:
            quarter = span // 4
            for c in range(4):
                for r4 in range(4):
                    tmp_ref[c, r4 * quarter:(r4 + 1) * quarter, :] = stage_ref[c, pl.ds(r4, quarter, stride=4), :]
            slabs, stride, start = tmp_ref, d // 4, lambda r: (r % 4) * quarter + r // 4
        else:
            slabs, stride, start = stage_ref, d, lambda r: r
        for c in range(4):
            for r in range(d):
                lo = r * dst_stride + dst_off
                dst_ref[lo:lo + npr, c * LANES:(c + 1) * LANES] = (
                    slabs[c, pl.ds(start(r), npr, stride=stride), :].astype(BF16))

    @pl.when(first)
    def _():
        for r in range(d):
            kp_ref[r * kstride:r * kstride + DSA_BLOCK, :] = jnp.zeros((DSA_BLOCK, DSA_WIDTH), BF16)
            vp_ref[r * kstride:r * kstride + DSA_BLOCK, :] = jnp.zeros((DSA_BLOCK, DSA_WIDTH), BF16)

    regroup(0, qp_ref, npr, 0)
    regroup(1, kp_ref, kstride, DSA_BLOCK)
    regroup(2, vp_ref, kstride, DSA_BLOCK)

    qi = lax.broadcasted_iota(jnp.int32, (DSA_BLOCK, 2 * DSA_BLOCK), 0)
    kj = lax.broadcasted_iota(jnp.int32, (DSA_BLOCK, 2 * DSA_BLOCK), 1)
    band = (kj >= qi) & (kj <= qi + DSA_BLOCK)
    head_a = _lane_iota((DSA_BLOCK, LANES)) < DSA_HEAD_DIM
    lse_head = _lane_iota((DSA_BLOCK, LANES)) // LSE_REP

    def block(blk, carry):
        r, i = blk // n_sub, blk % n_sub
        q0 = pl.multiple_of(r * npr + i * DSA_BLOCK, DSA_BLOCK)
        k0 = pl.multiple_of(r * kstride + i * DSA_BLOCK, DSA_BLOCK)
        nat0 = r + d * DSA_BLOCK * i
        mask = band & (kj >= jnp.where(first & (i == 0), DSA_BLOCK, 0))
        m_tile = jnp.zeros((DSA_BLOCK, LANES), F32)
        den_tile = jnp.ones((DSA_BLOCK, LANES), F32)
        for hp in range(DSA_HEADS // 2):
            cols = slice(hp * LANES, (hp + 1) * LANES)
            q2 = qp_ref[pl.ds(q0, DSA_BLOCK), cols]
            k2 = kp_ref[pl.ds(k0, 2 * DSA_BLOCK), cols]
            v2 = vp_ref[pl.ds(k0, 2 * DSA_BLOCK), cols]
            pvs, ms, dens = [], [], []
            for is_a in (True, False):
                sel = head_a if is_a else jnp.logical_not(head_a)
                qh = jnp.where(sel, q2, jnp.zeros_like(q2))
                s = lax.dot_general(qh, k2, (((1,), (1,)), ((), ())), preferred_element_type=F32)
                s = jnp.where(mask, s, NEG_BIG)
                m = jnp.max(s, axis=-1, keepdims=True)
                p = jnp.exp2(s - m)
                dens.append(jnp.sum(p, axis=-1, keepdims=True))
                pvs.append(jnp.dot(p.astype(BF16), v2, preferred_element_type=F32))
                ms.append(m)
            o_pair = jnp.where(head_a, pvs[0], pvs[1]) / jnp.where(head_a, dens[0], dens[1])
            den_tile = jnp.where(lse_head == 2 * hp, dens[0], jnp.where(lse_head == 2 * hp + 1, dens[1], den_tile))
            m_tile = jnp.where(lse_head == 2 * hp, ms[0], jnp.where(lse_head == 2 * hp + 1, ms[1], m_tile))
            if d == 1:
                o_ref[pl.ds(q0, DSA_BLOCK), cols] = o_pair.astype(o_ref.dtype)
            else:
                stage_ref[hp, pl.ds(nat0, DSA_BLOCK, stride=d), :] = o_pair
        lse_tile = m_tile * LN_2 + jnp.log(den_tile)
        if d == 1:
            lse_ref[pl.ds(q0, DSA_BLOCK), :] = lse_tile
        else:
            lstage_ref[pl.ds(nat0, DSA_BLOCK, stride=d), :] = lse_tile
        return carry

    lax.fori_loop(0, d * n_sub, block, 0, unroll=16)

    for r in range(d):
        lo = r * kstride
        kp_ref[lo:lo + DSA_BLOCK, :] = kp_ref[lo + npr:lo + npr + DSA_BLOCK, :]
        vp_ref[lo:lo + DSA_BLOCK, :] = vp_ref[lo + npr:lo + npr + DSA_BLOCK, :]
    if d > 1:
        for c in range(4):
            o_ref[:, c * LANES:(c + 1) * LANES] = stage_ref[c].astype(o_ref.dtype)
        lse_ref[...] = lstage_ref[...]


def _dsa_attention(qkv, dilation, batch, seq):
    span = min(DSA_SPAN, seq)
    steps = seq // span
    kv_rows = span + dilation * DSA_BLOCK
    row = lambda width: pl.BlockSpec((span, width), lambda b, n: (b * steps + n, 0))
    return pl.pallas_call(
        functools.partial(_dsa_kernel, d=dilation, span=span),
        grid=(batch, steps),
        in_specs=[row(GROUP_COLS)],
        out_specs=[row(DSA_WIDTH), row(LANES)],
        out_shape=[
            jax.ShapeDtypeStruct((batch * seq, DSA_WIDTH), BF16),
            jax.ShapeDtypeStruct((batch * seq, LANES), F32),
        ],
        scratch_shapes=[
            pltpu.VMEM((4, span, LANES), F32),
            pltpu.VMEM((span, DSA_WIDTH), BF16),
            pltpu.VMEM((kv_rows, DSA_WIDTH), BF16),
            pltpu.VMEM((kv_rows, DSA_WIDTH), BF16),
            pltpu.VMEM((span, LANES), F32),
        ] + ([pltpu.VMEM((4, span, LANES), F32)] if dilation > 4 else []),
        compiler_params=pltpu.CompilerParams(
            dimension_semantics=("parallel", "arbitrary"), vmem_limit_bytes=VMEM_LIMIT_BYTES),
        name=f"dsa_attn_d{dilation}",
    )(qkv)


def _silu(x):
    return x * (1.0 / (1.0 + jnp.exp(-x)))


def _sigmoid(x):
    return 1.0 / (1.0 + jnp.exp(-x))


def _out_proj_kernel(o0_ref, o1_ref, o2_ref, l0_ref, l1_ref, l2_ref, g_ref, ob_ref,
                     x_ref, wa_ref, wb_ref, wo_ref, fw_ref, ex_ref, out_ref):
    def gate(block, width):
        return g_ref[:, block * width:(block + 1) * width].astype(F32)

    l0, l1, l2 = l0_ref[...], l1_ref[...], l2_ref[...]
    mx = jnp.maximum(jnp.maximum(l0, l1), l2)
    e0, e1, e2 = jnp.exp(l0 - mx), jnp.exp(l1 - mx), jnp.exp(l2 - mx)
    inv = 1.0 / (e0 + e1 + e2)

    def per_lane(w):
        return jnp.dot(w.astype(BF16), ex_ref[...], preferred_element_type=F32)

    o_a = (per_lane(e0 * inv) * o0_ref[...].astype(F32) + per_lane(e1 * inv) * o1_ref[...].astype(F32)
           + per_lane(e2 * inv) * o2_ref[...].astype(F32))
    a_in = (o_a * _silu(gate(GATE_DSA_Z, DSA_WIDTH))).astype(BF16)
    b_in = (ob_ref[...].astype(F32) * _silu(gate(GATE_GDN_Z, GDN_WIDTH))).astype(BF16)
    y_a = jnp.dot(a_in, wa_ref[...], preferred_element_type=F32)
    y_b = jnp.dot(b_in, wb_ref[...], preferred_element_type=F32)
    merged = _sigmoid(gate(GATE_A, D_MODEL)) * y_a + _sigmoid(gate(GATE_B, D_MODEL)) * y_b
    y = x_ref[...] + jnp.dot(merged.astype(BF16), wo_ref[...], preferred_element_type=F32)
    ms = jnp.mean(y * y, axis=-1, keepdims=True)
    out_ref[...] = y * lax.rsqrt(ms + NORM_EPS) * fw_ref[...]


def _out_proj(o_groups, lse_groups, gates, o_b, x2, w_up_a, w_up_b, w_out, final_w):
    rows = x2.shape[0]
    tm = min(OUT_ROWS, rows)
    row128 = pl.BlockSpec((tm, LANES), lambda i: (i, 0))
    row512 = lambda c: pl.BlockSpec((tm, 512), lambda i: (i, c))
    row1024 = lambda c: pl.BlockSpec((tm, 1024), lambda i: (i, c))
    whole = lambda a: pl.BlockSpec(a.shape, lambda i: (0,) * a.ndim)
    expand = jnp.asarray(np.arange(LANES)[:, None] == LSE_REP * (np.arange(DSA_WIDTH)[None, :] // DSA_HEAD_DIM), BF16)
    return pl.pallas_call(
        _out_proj_kernel,
        grid=(rows // tm,),
        in_specs=[row512(0)] * 3 + [row128] * 3 + [
            pl.BlockSpec((tm, GATE_COLS), lambda i: (i, 0)), row512(0), row1024(0),
            whole(w_up_a), whole(w_up_b), whole(w_out), whole(final_w), whole(expand)],
        out_specs=pl.BlockSpec((tm, D_MODEL), lambda i: (i, 0)),
        out_shape=jax.ShapeDtypeStruct((rows, D_MODEL), F32),
        compiler_params=pltpu.CompilerParams(
            dimension_semantics=("parallel",), vmem_limit_bytes=VMEM_LIMIT_BYTES),
        name="out_proj",
    )(*o_groups, *lse_groups, gates, o_b, x2, w_up_a, w_up_b, w_out, final_w, expand)


QUAD = 4 * GDN_DIM
GDN_TILE_ROWS = 1024


def _split_hi_lo(x):
    hi = x.astype(BF16)
    lo = (x - hi.astype(F32)).astype(BF16)
    return hi, lo


def _softplus(x):
    return jnp.maximum(x, 0.0) + jnp.log1p(jnp.exp(-jnp.abs(x)))


def _bmm(a, b):
    return lax.dot_general(a, b, (((2,), (1,)), ((0,), (0,))), preferred_element_type=F32)


def _bmm_nt(a, b):
    return lax.dot_general(a, b, (((2,), (2,)), ((0,), (0,))), preferred_element_type=F32)


def _block_diag(x, bd_mask):
    t = jnp.concatenate([x, x, x, x], axis=1)
    return jnp.where(bd_mask, t, jnp.zeros_like(t))


def _gdn_kernel(qkv_ref, lg_ref, alog_ref, dtb_ref, nw_ref, e512_ref, eb_ref, eg_ref, tri_ref, o_ref,
                beta_ref, gh_ref, gl_ref, lhs_ref, oloc_ref, snew_ref, dec_ref, oraw_ref, state_ref):
    first = pl.program_id(0) == 0
    batch, rows, _ = qkv_ref.shape
    tg = batch * rows
    nc = rows // GDN_CHUNK
    per_step = 2 * batch
    nb = per_step * nc

    @pl.when(first)
    def _():
        state_ref[...] = jnp.zeros_like(state_ref)

    logits = lg_ref[...].reshape(tg, LOGIT_PAD)
    beta_hi, beta_lo = _split_hi_lo(_sigmoid(logits))
    g_hi, g_lo = _split_hi_lo(-jnp.exp(alog_ref[...]) * _softplus(logits + dtb_ref[...]))
    beta_ref[...] = (jnp.dot(beta_hi, eb_ref[...], preferred_element_type=F32)
                     + jnp.dot(beta_lo, eb_ref[...], preferred_element_type=F32))
    gh_ref[...] = jnp.dot(g_hi, eg_ref[...], preferred_element_type=F32).astype(BF16)
    gl_ref[...] = jnp.dot(g_lo, eg_ref[...], preferred_element_type=F32).astype(BF16)

    row = lax.broadcasted_iota(jnp.int32, (GDN_CHUNK, QUAD), 0)
    col = lax.broadcasted_iota(jnp.int32, (GDN_CHUNK, QUAD), 1) % GDN_CHUNK
    incl = row >= col
    strict = row > col
    eye = (row == col).astype(F32)
    bd_mask = (lax.broadcasted_iota(jnp.int32, (QUAD, QUAD), 0) // GDN_DIM
               == lax.broadcasted_iota(jnp.int32, (QUAD, QUAD), 1) // GDN_DIM)
    tri = tri_ref[...]

    def chunk_quads(tile, dtype):
        a = tile.reshape(batch, nc, GDN_CHUNK, GDN_WIDTH).astype(dtype)
        a = jnp.stack([a[b, j] for j in range(nc) for b in range(batch)])
        return jnp.stack([a[:, :, :QUAD], a[:, :, QUAD:]], axis=1).reshape(nb, GDN_CHUNK, QUAD)

    q, k, v = (chunk_quads(qkv_ref[:, :, part * GDN_WIDTH:(part + 1) * GDN_WIDTH], F32) for part in range(3))
    beta = chunk_quads(beta_ref[...], F32)
    gh, gl = chunk_quads(gh_ref[...], BF16), chunk_quads(gl_ref[...], BF16)

    zero = jnp.zeros_like(gh)
    rhs_hi = jnp.concatenate([gh, jnp.where(strict, gh, zero)], axis=-1)
    rhs_lo = jnp.concatenate([gl, jnp.where(strict, gl, zero)], axis=-1)
    tri_b = jnp.broadcast_to(tri, (nb, GDN_CHUNK, GDN_CHUNK))
    gd = _bmm(tri_b, rhs_hi) + _bmm(tri_b, rhs_lo)
    g_cum, d_pair = gd[:, :, :QUAD], gd[:, :, QUAD:]
    decay_incl = jnp.where(incl, jnp.exp(d_pair), 0.0)
    decay_strict = jnp.where(strict, decay_incl, 0.0)
    exp_g = jnp.exp(g_cum)
    g_last = g_cum[:, GDN_CHUNK - 1:GDN_CHUNK, :]
    k_dec = k * jnp.exp(g_last - g_cum)
    k_beta = k * beta

    k_bd = _block_diag(k.astype(BF16), bd_mask)
    aa = _bmm_nt(jnp.concatenate([k_beta, q], axis=1).astype(BF16), k_bd)
    a = aa[:, :GDN_CHUNK] * decay_strict
    attn = aa[:, GDN_CHUNK:] * decay_incl

    p = a
    t = eye - a
    p = _bmm(p.astype(BF16), _block_diag(p.astype(BF16), bd_mask))
    for _ in range(4):
        r = _bmm(jnp.concatenate([p, t], axis=1).astype(BF16), _block_diag(p.astype(BF16), bd_mask))
        p = r[:, :GDN_CHUNK]
        t = t + r[:, GDN_CHUNK:]
    t = t + _bmm(t.astype(BF16), _block_diag(p.astype(BF16), bd_mask))
    t16 = t.astype(BF16)
    u = _bmm(t16, _block_diag((v * beta).astype(BF16), bd_mask))
    w = _bmm(t16, _block_diag((k_beta * exp_g).astype(BF16), bd_mask))

    w16, u16 = w.astype(BF16), u.astype(BF16)
    kt = _bmm(jnp.swapaxes(k_dec, 1, 2).astype(BF16), jnp.concatenate([w16, u16], axis=-1))
    ao = _bmm(attn.astype(BF16),
              jnp.concatenate([_block_diag(w16, bd_mask), _block_diag(u16, bd_mask)], axis=-1))
    lhs_ref[...] = jnp.concatenate(
        [q * exp_g - ao[:, :, :QUAD], jnp.where(bd_mask, -kt[:, :, :QUAD], 0.0)], axis=1).astype(BF16)
    oloc_ref[...] = ao[:, :, QUAD:]
    snew_ref[...] = jnp.where(bd_mask, kt[:, :, QUAD:], 0.0)
    dec_ref[...] = jnp.broadcast_to(jnp.exp(g_last), (nb, 8, QUAD))

    for j in range(nc):
        sl = slice(j * per_step, (j + 1) * per_step)
        state = state_ref[...]
        r = _bmm(lhs_ref[sl], state.astype(BF16))
        state_ref[...] = state * dec_ref[sl][:, 0:1, :] + r[:, GDN_CHUNK:] + snew_ref[sl]
        o = r[:, :GDN_CHUNK] + oloc_ref[sl]
        for b in range(batch):
            lo = b * rows + j * GDN_CHUNK
            oraw_ref[lo:lo + GDN_CHUNK, 0:QUAD] = o[2 * b]
            oraw_ref[lo:lo + GDN_CHUNK, QUAD:] = o[2 * b + 1]

    o_all = oraw_ref[...]
    ms = jnp.dot((o_all * o_all).astype(BF16), e512_ref[...], preferred_element_type=F32) * (1.0 / GDN_DIM)
    o_ref[...] = (o_all * lax.rsqrt(ms + NORM_EPS) * nw_ref[...]).astype(o_ref.dtype).reshape(batch, rows, GDN_WIDTH)


def _gdn(gdn_qkv, logits, alog_row, dtb_row, norm_row, batch, seq):
    tg = min(GDN_TILE_ROWS, batch * seq)
    rows = tg // batch
    steps = seq // rows
    nb = 2 * (tg // GDN_CHUNK)
    head_of_lane = np.arange(GDN_WIDTH) // GDN_DIM
    e512 = jnp.asarray(head_of_lane[:, None] == head_of_lane[None, :], BF16)
    e_beta = jnp.asarray(np.arange(LOGIT_PAD)[:, None] == head_of_lane[None, :], BF16)
    e_g = jnp.asarray(np.arange(LOGIT_PAD)[:, None] == head_of_lane[None, :] + GDN_HEADS, BF16)
    tri = jnp.asarray(np.tril(np.ones((GDN_CHUNK, GDN_CHUNK))), BF16)

    cur = lambda width, c: pl.BlockSpec((batch, rows, width), lambda n: (0, n, c))
    whole = lambda a: pl.BlockSpec(a.shape, lambda n: (0,) * a.ndim)
    consts = (alog_row, dtb_row, norm_row, e512, e_beta, e_g, tri)
    qkv3 = gdn_qkv.reshape(batch, seq, GROUP_COLS)
    out = pl.pallas_call(
        _gdn_kernel,
        grid=(steps,),
        in_specs=[cur(GROUP_COLS, 0), cur(LOGIT_PAD, 0)]
        + [whole(a) for a in consts],
        out_specs=cur(GDN_WIDTH, 0),
        out_shape=jax.ShapeDtypeStruct((batch, seq, GDN_WIDTH), BF16),
        scratch_shapes=[
            pltpu.VMEM((tg, GDN_WIDTH), F32),
            pltpu.VMEM((tg, GDN_WIDTH), BF16),
            pltpu.VMEM((tg, GDN_WIDTH), BF16),
            pltpu.VMEM((nb, GDN_CHUNK + QUAD, QUAD), BF16),
            pltpu.VMEM((nb, GDN_CHUNK, QUAD), F32),
            pltpu.VMEM((nb, QUAD, QUAD), F32),
            pltpu.VMEM((nb, 8, QUAD), F32),
            pltpu.VMEM((tg, GDN_WIDTH), F32),
            pltpu.VMEM((2 * batch, QUAD, QUAD), F32),
        ],
        compiler_params=pltpu.CompilerParams(
            dimension_semantics=("arbitrary",), vmem_limit_bytes=VMEM_LIMIT_BYTES),
        name="gdn",
    )(qkv3, logits.reshape(batch, seq, LOGIT_PAD), *consts)
    return out.reshape(batch * seq, GDN_WIDTH)


def _rope_tables(seq):
    inv_freq = np.float32(ROPE_THETA) ** (-np.arange(0, DSA_HEAD_DIM, 2, dtype=np.float32) / np.float32(DSA_HEAD_DIM))
    ang = np.arange(seq, dtype=np.float32)[:, None] * inv_freq[None, :]
    cos, sin = np.cos(ang), np.sin(ang)
    return (jnp.asarray(np.concatenate([cos, cos, cos, cos], axis=-1), F32),
            jnp.asarray(np.concatenate([-sin, sin, -sin, sin], axis=-1), F32))


def kernel(x, norm_w, w_in, conv_w, a_log, dt_bias, gdn_norm_w, w_up_a, w_up_b, w_out, final_norm_w):
    batch, seq, _ = x.shape
    assert norm_w.shape[0] == 1, "the final RMSNorm is fused into the (single) layer's output kernel"
    cos_t, sin_t = _rope_tables(seq)
    x2 = x.reshape(batch * seq, D_MODEL)
    w = w_in[0].astype(BF16)
    gates = LOGIT_START + 2 * GDN_HEADS
    dsa_z, gdn_z = QKV_COLS, GDN_IN_START + GROUP_COLS
    w_gates = jnp.concatenate(
        [w[:, gates:], w[:, dsa_z:GDN_IN_START], w[:, gdn_z:LOGIT_START],
         w[:, LOGIT_START:gates], jnp.zeros((D_MODEL, LOGIT_PAD - 2 * GDN_HEADS), BF16)], axis=1)
    qkv, gdn_qkv, gate_cols, logits = _in_proj(x2, norm_w[0][None, :], w, w_gates, conv_w[0], cos_t, sin_t, seq)

    o_groups, lse_groups = [], []
    for g, (_, dilation) in enumerate(DSA_PATTERNS):
        o_g, lse_g = _dsa_attention(qkv[g], dilation, batch, seq)
        o_groups.append(o_g)
        lse_groups.append(lse_g)

    pad8 = lambda p: jnp.pad(p.astype(F32), (GDN_HEADS, LOGIT_PAD - 2 * GDN_HEADS))[None, :]
    o_b = _gdn(gdn_qkv, logits, pad8(a_log[0]), pad8(dt_bias[0]),
               jnp.tile(gdn_norm_w[0].astype(F32), GDN_HEADS)[None, :], batch, seq)

    out = _out_proj(o_groups, lse_groups, gate_cols, o_b, x2,
                    w_up_a[0].astype(BF16), w_up_b[0].astype(BF16), w_out[0].astype(BF16),
                    final_norm_w[None, :])
    return out.reshape(batch, seq, D_MODEL)
```

```python
import functools

import numpy as np
import jax
import jax.numpy as jnp
from jax import lax
from jax.experimental import pallas as pl
from jax.experimental.pallas import tpu as pltpu

F32 = jnp.float32
BF16 = jnp.bfloat16

D_MODEL = 1024
DSA_PATTERNS = ((128, 1), (512, 4), (2048, 16))
DSA_HEADS = 8
DSA_HEAD_DIM = 64
DSA_WIDTH = DSA_HEADS * DSA_HEAD_DIM
DSA_BLOCK = 128
ROPE_THETA = 10000.0
GDN_HEADS = 8
GDN_DIM = 64
GDN_WIDTH = GDN_HEADS * GDN_DIM
GDN_CONV = 4
GDN_CHUNK = 64
NORM_EPS = 1e-6

QKV_COLS = 3 * 3 * DSA_WIDTH
GROUP_COLS = 3 * DSA_WIDTH
GATE_COLS = 3072
GATE_A, GATE_B = 0, 1
GATE_DSA_Z, GATE_GDN_Z = 4, 5
GDN_IN_START = QKV_COLS + DSA_WIDTH
LOGIT_START = 7168
LOGIT_PAD = 128
PROJ_ROWS = 1024
OUT_ROWS = 1024
LOG2_E = 1.4426950408889634
LN_2 = 0.6931471805599453
DSA_Q_SCALE = DSA_HEAD_DIM ** -0.5 * LOG2_E

VMEM_LIMIT_BYTES = 56 * 1024 * 1024
LANES = 128
NEG_BIG = -1e30


def _lane_iota(shape):
    return lax.broadcasted_iota(jnp.int32, shape, len(shape) - 1)


def _in_proj_qkv_kernel(x_ref, nw_ref, w_ref, cos_ref, sin_ref, qkv0_ref, qkv1_ref, qkv2_ref, h_ref):
    x = x_ref[...]
    ms = jnp.mean(x * x, axis=-1, keepdims=True)
    h = (x * lax.rsqrt(ms + NORM_EPS) * nw_ref[...]).astype(BF16)
    h_ref[...] = h
    cos = cos_ref[...]
    sin = sin_ref[...]
    low_half = (_lane_iota(cos.shape) % DSA_HEAD_DIM) < (DSA_HEAD_DIM // 2)
    for g, qkv_ref in enumerate((qkv0_ref, qkv1_ref, qkv2_ref)):
        c0 = g * GROUP_COLS
        acc = jnp.dot(h, w_ref[:, c0:c0 + GROUP_COLS], preferred_element_type=F32)
        for c in range(8):
            t = acc[:, c * LANES:(c + 1) * LANES]
            rot = jnp.where(low_half, pltpu.roll(t, LANES - 32, 1), pltpu.roll(t, 32, 1))
            r = t * cos + rot * sin
            if c < 4:
                r = r * DSA_Q_SCALE
            qkv_ref[:, c * LANES:(c + 1) * LANES] = r.astype(BF16)
        qkv_ref[:, 8 * LANES:] = acc[:, 8 * LANES:].astype(BF16)


def _in_proj_rest_kernel(h_ref, wgq_ref, wgk_ref, wgv_ref, w_ref, wl_ref, cw_ref, e512_ref,
                         gdn_ref, gates_ref, logit_ref, xs_ref, *, tiles_per_seq):
    tm = h_ref.shape[0]
    first = pl.program_id(0) % tiles_per_seq == 0

    @pl.when(first)
    def _():
        xs_ref[0:8, :] = jnp.zeros((8, GROUP_COLS), F32)

    @pl.when(jnp.logical_not(first))
    def _():
        xs_ref[0:8, :] = xs_ref[tm:tm + 8, :]

    h = h_ref[...]
    logit_ref[...] = jnp.dot(h, wl_ref[...], preferred_element_type=F32)
    for part, wg_ref in enumerate((wgq_ref, wgk_ref, wgv_ref)):
        xs_ref[8:8 + tm, part * GDN_WIDTH:(part + 1) * GDN_WIDTH] = jnp.dot(
            h, wg_ref[...], preferred_element_type=F32)
    for g in range(GATE_COLS // GROUP_COLS):
        c0 = g * GROUP_COLS
        gates_ref[:, c0:c0 + GROUP_COLS] = jnp.dot(
            h, w_ref[:, c0:c0 + GROUP_COLS], preferred_element_type=F32).astype(BF16)
    for part in range(3):
        cols = slice(part * GDN_WIDTH, (part + 1) * GDN_WIDTH)
        w = cw_ref[:, cols]
        y = w[0:1, :] * xs_ref[5:5 + tm, cols]
        for j in range(1, GDN_CONV):
            y = y + w[j:j + 1, :] * xs_ref[5 + j:5 + j + tm, cols]
        y = _silu_of_twice(y)
        if part < 2:
            ss = jnp.dot((y * y).astype(BF16), e512_ref[...], preferred_element_type=F32)
            y = y * (lax.rsqrt(ss + NORM_EPS) * (GDN_DIM ** -0.5 if part == 0 else 1.0))
        gdn_ref[:, cols] = y.astype(BF16)


def _resident(shape, col_block=0):
    return pl.BlockSpec(shape, lambda i: (0,) * (len(shape) - 1) + (col_block,), pipeline_mode=pl.Buffered(1))


def _in_proj(x2, norm_w, w_bf16, w_gates, conv_w, cos_t, sin_t, seq):
    assert GDN_IN_START % GDN_WIDTH == 0 and GATE_COLS % LOGIT_PAD == 0
    rows = x2.shape[0]
    tm = min(PROJ_ROWS, seq)
    n_seq_tiles = seq // tm
    row = lambda width: pl.BlockSpec((tm, width), lambda i: (i, 0))
    table = pl.BlockSpec((tm, LANES), lambda i: (i % n_seq_tiles, 0))
    head_of_lane = np.arange(GDN_WIDTH) // GDN_DIM
    e512 = jnp.asarray(head_of_lane[:, None] == head_of_lane[None, :], BF16)
    gdn_block = GDN_IN_START // GDN_WIDTH
    *qkv, h = pl.pallas_call(
        _in_proj_qkv_kernel,
        grid=(rows // tm,),
        in_specs=[row(D_MODEL), _resident((1, D_MODEL)), _resident((D_MODEL, QKV_COLS)), table, table],
        out_specs=[row(GROUP_COLS)] * len(DSA_PATTERNS) + [row(D_MODEL)],
        out_shape=[jax.ShapeDtypeStruct((rows, GROUP_COLS), BF16)] * len(DSA_PATTERNS)
        + [jax.ShapeDtypeStruct((rows, D_MODEL), BF16)],
        compiler_params=pltpu.CompilerParams(dimension_semantics=("parallel",), vmem_limit_bytes=VMEM_LIMIT_BYTES),
        name="in_proj_qkv",
    )(x2, norm_w, w_bf16, cos_t, sin_t)
    gdn_qkv, gates, logits = pl.pallas_call(
        functools.partial(_in_proj_rest_kernel, tiles_per_seq=n_seq_tiles),
        grid=(rows // tm,),
        in_specs=[row(D_MODEL), _resident((D_MODEL, GDN_WIDTH), gdn_block),
                  _resident((D_MODEL, GDN_WIDTH), gdn_block + 1), _resident((D_MODEL, GDN_WIDTH), gdn_block + 2),
                  _resident((D_MODEL, GATE_COLS)), _resident((D_MODEL, LOGIT_PAD), GATE_COLS // LOGIT_PAD),
                  _resident(conv_w.shape), _resident(e512.shape)],
        out_specs=[row(GROUP_COLS), row(GATE_COLS), row(LOGIT_PAD)],
        out_shape=[jax.ShapeDtypeStruct((rows, GROUP_COLS), BF16), jax.ShapeDtypeStruct((rows, GATE_COLS), BF16),
                   jax.ShapeDtypeStruct((rows, LOGIT_PAD), F32)],
        scratch_shapes=[pltpu.VMEM((tm + 8, GROUP_COLS), F32)],
        compiler_params=pltpu.CompilerParams(
            dimension_semantics=("arbitrary",), vmem_limit_bytes=VMEM_LIMIT_BYTES),
        name="in_proj_rest",
    )(h, w_bf16, w_bf16, w_bf16, w_gates, w_gates, conv_w, e512)
    return qkv, gdn_qkv, gates, logits


DSA_SPAN = 2048
LSE_REP = LANES // DSA_HEADS


def _dsa_kernel(qkv_ref, o_ref, lse_ref, stage_ref, qp_ref, kp_ref, vp_ref, lstage_ref, tmp_ref=None,
                *, d, span):
    first = pl.program_id(1) == 0
    npr = span // d
    n_sub = npr // DSA_BLOCK
    kstride = npr + DSA_BLOCK

    def regroup(part, dst_ref, dst_stride, dst_off):
        c_in = part * DSA_WIDTH
        if d == 1:
            dst_ref[dst_off:dst_off + npr, :] = qkv_ref[:, c_in:c_in + DSA_WIDTH]
            return
        for c in range(4):
            stage_ref[c] = qkv_ref[:, c_in + c * LANES:c_in + (c + 1) * LANES].astype(F32)
        if d > 4:
            quarter = span // 4
            for c in range(4):
                for r4 in range(4):
                    tmp_ref[c, r4 * quarter:(r4 + 1) * quarter, :] = stage_ref[c, pl.ds(r4, quarter, stride=4), :]
            slabs, stride, start = tmp_ref, d // 4, lambda r: (r % 4) * quarter + r // 4
        else:
            slabs, stride, start = stage_ref, d, lambda r: r
        for c in range(4):
            for r in range(d):
                lo = r * dst_stride + dst_off
                dst_ref[lo:lo + npr, c * LANES:(c + 1) * LANES] = (
                    slabs[c, pl.ds(start(r), npr, stride=stride), :].astype(BF16))

    @pl.when(first)
    def _():
        for r in range(d):
            kp_ref[r * kstride:r * kstride + DSA_BLOCK, :] = jnp.zeros((DSA_BLOCK, DSA_WIDTH), BF16)
            vp_ref[r * kstride:r * kstride + DSA_BLOCK, :] = jnp.zeros((DSA_BLOCK, DSA_WIDTH), BF16)

    regroup(0, qp_ref, npr, 0)
    regroup(1, kp_ref, kstride, DSA_BLOCK)
    regroup(2, vp_ref, kstride, DSA_BLOCK)

    qi = lax.broadcasted_iota(jnp.int32, (DSA_BLOCK, 2 * DSA_BLOCK), 0)
    kj = lax.broadcasted_iota(jnp.int32, (DSA_BLOCK, 2 * DSA_BLOCK), 1)
    band = (kj >= qi) & (kj <= qi + DSA_BLOCK)
    head_a = _lane_iota((DSA_BLOCK, LANES)) < DSA_HEAD_DIM
    lse_head = _lane_iota((DSA_BLOCK, LANES)) // LSE_REP

    def block(blk, carry):
        r, i = blk // n_sub, blk % n_sub
        q0 = pl.multiple_of(r * npr + i * DSA_BLOCK, DSA_BLOCK)
        k0 = pl.multiple_of(r * kstride + i * DSA_BLOCK, DSA_BLOCK)
        nat0 = r + d * DSA_BLOCK * i
        mask = band & (kj >= jnp.where(first & (i == 0), DSA_BLOCK, 0))
        m_tile = jnp.zeros((DSA_BLOCK, LANES), F32)
        den_tile = jnp.ones((DSA_BLOCK, LANES), F32)
        for hp in range(DSA_HEADS // 2):
            cols = slice(hp * LANES, (hp + 1) * LANES)
            q2 = qp_ref[pl.ds(q0, DSA_BLOCK), cols]
            k2 = kp_ref[pl.ds(k0, 2 * DSA_BLOCK), cols]
            v2 = vp_ref[pl.ds(k0, 2 * DSA_BLOCK), cols]
            pvs, ms, dens = [], [], []
            for is_a in (True, False):
                sel = head_a if is_a else jnp.logical_not(head_a)
                qh = jnp.where(sel, q2, jnp.zeros_like(q2))
                s = lax.dot_general(qh, k2, (((1,), (1,)), ((), ())), preferred_element_type=F32)
                s = jnp.where(mask, s, NEG_BIG)
                m = jnp.max(s, axis=-1, keepdims=True)
                p = jnp.exp2(s - m)
                dens.append(jnp.sum(p, axis=-1, keepdims=True))
                pvs.append(jnp.dot(p.astype(BF16), v2, preferred_element_type=F32))
                ms.append(m)
            o_pair = jnp.where(head_a, pvs[0], pvs[1]) / jnp.where(head_a, dens[0], dens[1])
            den_tile = jnp.where(lse_head == 2 * hp, dens[0], jnp.where(lse_head == 2 * hp + 1, dens[1], den_tile))
            m_tile = jnp.where(lse_head == 2 * hp, ms[0], jnp.where(lse_head == 2 * hp + 1, ms[1], m_tile))
            if d == 1:
                o_ref[pl.ds(q0, DSA_BLOCK), cols] = o_pair.astype(o_ref.dtype)
            else:
                stage_ref[hp, pl.ds(nat0, DSA_BLOCK, stride=d), :] = o_pair
        lse_tile = m_tile * LN_2 + jnp.log(den_tile)
        if d == 1:
            lse_ref[pl.ds(q0, DSA_BLOCK), :] = lse_tile
        else:
            lstage_ref[pl.ds(nat0, DSA_BLOCK, stride=d), :] = lse_tile
        return carry

    lax.fori_loop(0, d * n_sub, block, 0, unroll=16)

    for r in range(d):
        lo = r * kstride
        kp_ref[lo:lo + DSA_BLOCK, :] = kp_ref[lo + npr:lo + npr + DSA_BLOCK, :]
        vp_ref[lo:lo + DSA_BLOCK, :] = vp_ref[lo + npr:lo + npr + DSA_BLOCK, :]
    if d > 1:
        for c in range(4):
            o_ref[:, c * LANES:(c + 1) * LANES] = stage_ref[c].astype(o_ref.dtype)
        lse_ref[...] = lstage_ref[...]


def _dsa_attention(qkv, dilation, batch, seq):
    span = min(DSA_SPAN, seq)
    steps = seq // span
    kv_rows = span + dilation * DSA_BLOCK
    row = lambda width: pl.BlockSpec((span, width), lambda b, n: (b * steps + n, 0))
    return pl.pallas_call(
        functools.partial(_dsa_kernel, d=dilation, span=span),
        grid=(batch, steps),
        in_specs=[row(GROUP_COLS)],
        out_specs=[row(DSA_WIDTH), row(LANES)],
        out_shape=[
            jax.ShapeDtypeStruct((batch * seq, DSA_WIDTH), BF16),
            jax.ShapeDtypeStruct((batch * seq, LANES), F32),
        ],
        scratch_shapes=[
            pltpu.VMEM((4, span, LANES), F32),
            pltpu.VMEM((span, DSA_WIDTH), BF16),
            pltpu.VMEM((kv_rows, DSA_WIDTH), BF16),
            pltpu.VMEM((kv_rows, DSA_WIDTH), BF16),
            pltpu.VMEM((span, LANES), F32),
        ] + ([pltpu.VMEM((4, span, LANES), F32)] if dilation > 4 else []),
        compiler_params=pltpu.CompilerParams(
            dimension_semantics=("parallel", "arbitrary"), vmem_limit_bytes=VMEM_LIMIT_BYTES),
        name=f"dsa_attn_d{dilation}",
    )(qkv)


def _silu_of_twice(h):
    return h + h * jnp.tanh(h)


def _sigmoid(x):
    return 1.0 / (1.0 + jnp.exp(-x))


def _out_proj_kernel(o0_ref, o1_ref, o2_ref, l0_ref, l1_ref, l2_ref, g_ref, ob_ref,
                     x_ref, wa_ref, wb_ref, wo_ref, fw_ref, ex_ref, out_ref):
    def gate(block, width):
        return g_ref[:, block * width:(block + 1) * width].astype(F32)

    l0, l1, l2 = l0_ref[...], l1_ref[...], l2_ref[...]
    mx = jnp.maximum(jnp.maximum(l0, l1), l2)
    e0, e1, e2 = jnp.exp(l0 - mx), jnp.exp(l1 - mx), jnp.exp(l2 - mx)
    inv = 1.0 / (e0 + e1 + e2)

    def per_lane(w):
        return jnp.dot(w.astype(BF16), ex_ref[...], preferred_element_type=F32)

    o_a = (per_lane(e0 * inv) * o0_ref[...].astype(F32) + per_lane(e1 * inv) * o1_ref[...].astype(F32)
           + per_lane(e2 * inv) * o2_ref[...].astype(F32))
    a_in = (o_a * _silu_of_twice(gate(GATE_DSA_Z, DSA_WIDTH))).astype(BF16)
    b_in = (ob_ref[...].astype(F32) * _silu_of_twice(gate(GATE_GDN_Z, GDN_WIDTH))).astype(BF16)
    y_a = jnp.dot(a_in, wa_ref[...], preferred_element_type=F32)
    y_b = jnp.dot(b_in, wb_ref[...], preferred_element_type=F32)
    merged2 = (1.0 + jnp.tanh(gate(GATE_A, D_MODEL))) * y_a + (1.0 + jnp.tanh(gate(GATE_B, D_MODEL))) * y_b
    y = x_ref[...] + jnp.dot(merged2.astype(BF16), wo_ref[...], preferred_element_type=F32)
    ms = jnp.mean(y * y, axis=-1, keepdims=True)
    out_ref[...] = y * lax.rsqrt(ms + NORM_EPS) * fw_ref[...]


def _out_proj(o_groups, lse_groups, gates, o_b, x2, w_up_a, w_up_b, w_out, final_w):
    rows = x2.shape[0]
    tm = min(OUT_ROWS, rows)
    row128 = pl.BlockSpec((tm, LANES), lambda i: (i, 0))
    row512 = lambda c: pl.BlockSpec((tm, 512), lambda i: (i, c))
    row1024 = lambda c: pl.BlockSpec((tm, 1024), lambda i: (i, c))
    whole = lambda a: pl.BlockSpec(a.shape, lambda i: (0,) * a.ndim)
    expand = jnp.asarray(np.arange(LANES)[:, None] == LSE_REP * (np.arange(DSA_WIDTH)[None, :] // DSA_HEAD_DIM), BF16)
    return pl.pallas_call(
        _out_proj_kernel,
        grid=(rows // tm,),
        in_specs=[row512(0)] * 3 + [row128] * 3 + [
            pl.BlockSpec((tm, GATE_COLS), lambda i: (i, 0)), row512(0), row1024(0),
            whole(w_up_a), whole(w_up_b), whole(w_out), whole(final_w), whole(expand)],
        out_specs=pl.BlockSpec((tm, D_MODEL), lambda i: (i, 0)),
        out_shape=jax.ShapeDtypeStruct((rows, D_MODEL), F32),
        compiler_params=pltpu.CompilerParams(
            dimension_semantics=("parallel",), vmem_limit_bytes=VMEM_LIMIT_BYTES),
        name="out_proj",
    )(*o_groups, *lse_groups, gates, o_b, x2, w_up_a, w_up_b, w_out, final_w, expand)


QUAD = 4 * GDN_DIM
GDN_TILE_ROWS = 1024


def _split_hi_lo(x):
    hi = x.astype(BF16)
    lo = (x - hi.astype(F32)).astype(BF16)
    return hi, lo


def _softplus(x):
    return jnp.maximum(x, 0.0) + jnp.log1p(jnp.exp(-jnp.abs(x)))


def _bmm(a, b):
    return lax.dot_general(a, b, (((2,), (1,)), ((0,), (0,))), preferred_element_type=F32)


def _bmm_nt(a, b):
    return lax.dot_general(a, b, (((2,), (2,)), ((0,), (0,))), preferred_element_type=F32)


def _block_diag(x, bd_mask):
    t = jnp.concatenate([x, x, x, x], axis=1)
    return jnp.where(bd_mask, t, jnp.zeros_like(t))


def _gdn_kernel(qkv_ref, lg_ref, alog_ref, dtb_ref, nw_ref, e512_ref, eb_ref, eg_ref, tri_ref, o_ref,
                beta_ref, gh_ref, gl_ref, lhs_ref, oloc_ref, snew_ref, dec_ref, oraw_ref, state_ref):
    first = pl.program_id(0) == 0
    batch, rows, _ = qkv_ref.shape
    tg = batch * rows
    nc = rows // GDN_CHUNK
    per_step = 2 * batch
    nb = per_step * nc

    @pl.when(first)
    def _():
        state_ref[...] = jnp.zeros_like(state_ref)

    logits = lg_ref[...].reshape(tg, LOGIT_PAD)
    beta_hi, beta_lo = _split_hi_lo(_sigmoid(logits))
    g_hi, g_lo = _split_hi_lo(-jnp.exp(alog_ref[...]) * _softplus(logits + dtb_ref[...]))
    beta_ref[...] = (jnp.dot(beta_hi, eb_ref[...], preferred_element_type=F32)
                     + jnp.dot(beta_lo, eb_ref[...], preferred_element_type=F32))
    gh_ref[...] = jnp.dot(g_hi, eg_ref[...], preferred_element_type=F32).astype(BF16)
    gl_ref[...] = jnp.dot(g_lo, eg_ref[...], preferred_element_type=F32).astype(BF16)

    row = lax.broadcasted_iota(jnp.int32, (GDN_CHUNK, QUAD), 0)
    col = lax.broadcasted_iota(jnp.int32, (GDN_CHUNK, QUAD), 1) % GDN_CHUNK
    incl = row >= col
    strict = row > col
    eye = (row == col).astype(F32)
    bd_mask = (lax.broadcasted_iota(jnp.int32, (QUAD, QUAD), 0) // GDN_DIM
               == lax.broadcasted_iota(jnp.int32, (QUAD, QUAD), 1) // GDN_DIM)
    tri = tri_ref[...]

    def chunk_quads(tile, dtype):
        a = tile.reshape(batch, nc, GDN_CHUNK, GDN_WIDTH).astype(dtype)
        a = jnp.stack([a[b, j] for j in range(nc) for b in range(batch)])
        return jnp.stack([a[:, :, :QUAD], a[:, :, QUAD:]], axis=1).reshape(nb, GDN_CHUNK, QUAD)

    q, k, v = (chunk_quads(qkv_ref[:, :, part * GDN_WIDTH:(part + 1) * GDN_WIDTH], F32) for part in range(3))
    beta = chunk_quads(beta_ref[...], F32)
    gh, gl = chunk_quads(gh_ref[...], BF16), chunk_quads(gl_ref[...], BF16)

    zero = jnp.zeros_like(gh)
    rhs_hi = jnp.concatenate([gh, jnp.where(strict, gh, zero)], axis=-1)
    rhs_lo = jnp.concatenate([gl, jnp.where(strict, gl, zero)], axis=-1)
    tri_b = jnp.broadcast_to(tri, (nb, GDN_CHUNK, GDN_CHUNK))
    gd = _bmm(tri_b, rhs_hi) + _bmm(tri_b, rhs_lo)
    g_cum, d_pair = gd[:, :, :QUAD], gd[:, :, QUAD:]
    decay_incl = jnp.where(incl, jnp.exp(d_pair), 0.0)
    decay_strict = jnp.where(strict, decay_incl, 0.0)
    exp_g = jnp.exp(g_cum)
    g_last = g_cum[:, GDN_CHUNK - 1:GDN_CHUNK, :]
    k_dec = k * jnp.exp(g_last - g_cum)
    k_beta = k * beta

    k_bd = _block_diag(k.astype(BF16), bd_mask)
    aa = _bmm_nt(jnp.concatenate([k_beta, q], axis=1).astype(BF16), k_bd)
    a = aa[:, :GDN_CHUNK] * decay_strict
    attn = aa[:, GDN_CHUNK:] * decay_incl

    p = a
    t = eye - a
    p = _bmm(p.astype(BF16), _block_diag(p.astype(BF16), bd_mask))
    for _ in range(4):
        r = _bmm(jnp.concatenate([p, t], axis=1).astype(BF16), _block_diag(p.astype(BF16), bd_mask))
        p = r[:, :GDN_CHUNK]
        t = t + r[:, GDN_CHUNK:]
    t = t + _bmm(t.astype(BF16), _block_diag(p.astype(BF16), bd_mask))
    t16 = t.astype(BF16)
    u = _bmm(t16, _block_diag((v * beta).astype(BF16), bd_mask))
    w = _bmm(t16, _block_diag((k_beta * exp_g).astype(BF16), bd_mask))

    w16, u16 = w.astype(BF16), u.astype(BF16)
    kt = _bmm(jnp.swapaxes(k_dec, 1, 2).astype(BF16), jnp.concatenate([w16, u16], axis=-1))
    ao = _bmm(attn.astype(BF16),
              jnp.concatenate([_block_diag(w16, bd_mask), _block_diag(u16, bd_mask)], axis=-1))
    lhs_ref[...] = jnp.concatenate(
        [q * exp_g - ao[:, :, :QUAD], jnp.where(bd_mask, -kt[:, :, :QUAD], 0.0)], axis=1).astype(BF16)
    oloc_ref[...] = ao[:, :, QUAD:]
    snew_ref[...] = jnp.where(bd_mask, kt[:, :, QUAD:], 0.0)
    dec_ref[...] = jnp.broadcast_to(jnp.exp(g_last), (nb, 8, QUAD))

    for j in range(nc):
        sl = slice(j * per_step, (j + 1) * per_step)
        state = state_ref[...]
        r = _bmm(lhs_ref[sl], state.astype(BF16))
        state_ref[...] = state * dec_ref[sl][:, 0:1, :] + r[:, GDN_CHUNK:] + snew_ref[sl]
        o = r[:, :GDN_CHUNK] + oloc_ref[sl]
        for b in range(batch):
            lo = b * rows + j * GDN_CHUNK
            oraw_ref[lo:lo + GDN_CHUNK, 0:QUAD] = o[2 * b]
            oraw_ref[lo:lo + GDN_CHUNK, QUAD:] = o[2 * b + 1]

    o_all = oraw_ref[...]
    ms = jnp.dot((o_all * o_all).astype(BF16), e512_ref[...], preferred_element_type=F32) * (1.0 / GDN_DIM)
    o_ref[...] = (o_all * lax.rsqrt(ms + NORM_EPS) * nw_ref[...]).astype(o_ref.dtype).reshape(batch, rows, GDN_WIDTH)


def _gdn(gdn_qkv, logits, alog_row, dtb_row, norm_row, batch, seq):
    tg = min(GDN_TILE_ROWS, batch * seq)
    rows = tg // batch
    steps = seq // rows
    nb = 2 * (tg // GDN_CHUNK)
    head_of_lane = np.arange(GDN_WIDTH) // GDN_DIM
    e512 = jnp.asarray(head_of_lane[:, None] == head_of_lane[None, :], BF16)
    e_beta = jnp.asarray(np.arange(LOGIT_PAD)[:, None] == head_of_lane[None, :], BF16)
    e_g = jnp.asarray(np.arange(LOGIT_PAD)[:, None] == head_of_lane[None, :] + GDN_HEADS, BF16)
    tri = jnp.asarray(np.tril(np.ones((GDN_CHUNK, GDN_CHUNK))), BF16)

    cur = lambda width, c: pl.BlockSpec((batch, rows, width), lambda n: (0, n, c))
    whole = lambda a: pl.BlockSpec(a.shape, lambda n: (0,) * a.ndim)
    consts = (alog_row, dtb_row, norm_row, e512, e_beta, e_g, tri)
    qkv3 = gdn_qkv.reshape(batch, seq, GROUP_COLS)
    out = pl.pallas_call(
        _gdn_kernel,
        grid=(steps,),
        in_specs=[cur(GROUP_COLS, 0), cur(LOGIT_PAD, 0)]
        + [whole(a) for a in consts],
        out_specs=cur(GDN_WIDTH, 0),
        out_shape=jax.ShapeDtypeStruct((batch, seq, GDN_WIDTH), BF16),
        scratch_shapes=[
            pltpu.VMEM((tg, GDN_WIDTH), F32),
            pltpu.VMEM((tg, GDN_WIDTH), BF16),
            pltpu.VMEM((tg, GDN_WIDTH), BF16),
            pltpu.VMEM((nb, GDN_CHUNK + QUAD, QUAD), BF16),
            pltpu.VMEM((nb, GDN_CHUNK, QUAD), F32),
            pltpu.VMEM((nb, QUAD, QUAD), F32),
            pltpu.VMEM((nb, 8, QUAD), F32),
            pltpu.VMEM((tg, GDN_WIDTH), F32),
            pltpu.VMEM((2 * batch, QUAD, QUAD), F32),
        ],
        compiler_params=pltpu.CompilerParams(
            dimension_semantics=("arbitrary",), vmem_limit_bytes=VMEM_LIMIT_BYTES),
        name="gdn",
    )(qkv3, logits.reshape(batch, seq, LOGIT_PAD), *consts)
    return out.reshape(batch * seq, GDN_WIDTH)


def _rope_tables(seq):
    inv_freq = np.float32(ROPE_THETA) ** (-np.arange(0, DSA_HEAD_DIM, 2, dtype=np.float32) / np.float32(DSA_HEAD_DIM))
    ang = np.arange(seq, dtype=np.float32)[:, None] * inv_freq[None, :]
    cos, sin = np.cos(ang), np.sin(ang)
    return (jnp.asarray(np.concatenate([cos, cos, cos, cos], axis=-1), F32),
            jnp.asarray(np.concatenate([-sin, sin, -sin, sin], axis=-1), F32))


def kernel(x, norm_w, w_in, conv_w, a_log, dt_bias, gdn_norm_w, w_up_a, w_up_b, w_out, final_norm_w):
    batch, seq, _ = x.shape
    assert norm_w.shape[0] == 1, "the final RMSNorm is fused into the (single) layer's output kernel"
    cos_t, sin_t = _rope_tables(seq)
    x2 = x.reshape(batch * seq, D_MODEL)
    w = w_in[0].astype(BF16)
    gates = LOGIT_START + 2 * GDN_HEADS
    dsa_z, gdn_z = QKV_COLS, GDN_IN_START + GROUP_COLS
    half = lambda a: a * jnp.asarray(0.5, a.dtype)
    w_gates = jnp.concatenate(
        [half(w[:, gates:]), half(w[:, dsa_z:GDN_IN_START]), half(w[:, gdn_z:LOGIT_START]),
         w[:, LOGIT_START:gates], jnp.zeros((D_MODEL, LOGIT_PAD - 2 * GDN_HEADS), BF16)], axis=1)
    qkv, gdn_qkv, gate_cols, logits = _in_proj(
        x2, norm_w[0][None, :], w, w_gates, half(conv_w[0]), cos_t, sin_t, seq)

    o_groups, lse_groups = [], []
    for g, (_, dilation) in enumerate(DSA_PATTERNS):
        o_g, lse_g = _dsa_attention(qkv[g], dilation, batch, seq)
        o_groups.append(o_g)
        lse_groups.append(lse_g)

    pad8 = lambda p: jnp.pad(p.astype(F32), (GDN_HEADS, LOGIT_PAD - 2 * GDN_HEADS))[None, :]
    o_b = _gdn(gdn_qkv, logits, pad8(a_log[0]), pad8(dt_bias[0]),
               jnp.tile(gdn_norm_w[0].astype(F32), GDN_HEADS)[None, :], batch, seq)

    out = _out_proj(o_groups, lse_groups, gate_cols, o_b, x2,
                    w_up_a[0].astype(BF16), w_up_b[0].astype(BF16), half(w_out[0]).astype(BF16),
                    final_norm_w[None, :])
    return out.reshape(batch, seq, D_MODEL)
```

```python
import functools

import numpy as np
import jax
import jax.numpy as jnp
from jax import lax
from jax.experimental import pallas as pl
from jax.experimental.pallas import tpu as pltpu

F32 = jnp.float32
BF16 = jnp.bfloat16

D_MODEL = 1024
DSA_PATTERNS = ((128, 1), (512, 4), (2048, 16))
DSA_HEADS = 8
DSA_HEAD_DIM = 64
DSA_WIDTH = DSA_HEADS * DSA_HEAD_DIM
DSA_BLOCK = 128
ROPE_THETA = 10000.0
GDN_HEADS = 8
GDN_DIM = 64
GDN_WIDTH = GDN_HEADS * GDN_DIM
GDN_CONV = 4
GDN_CHUNK = 64
NORM_EPS = 1e-6

QKV_COLS = 3 * 3 * DSA_WIDTH
GROUP_COLS = 3 * DSA_WIDTH
GATE_COLS = 3072
GATE_A, GATE_B = 0, 1
GATE_DSA_Z, GATE_GDN_Z = 4, 5
GDN_IN_START = QKV_COLS + DSA_WIDTH
LOGIT_START = 7168
LOGIT_PAD = 128
PROJ_ROWS = 1024
OUT_ROWS = 1024
LOG2_E = 1.4426950408889634
LN_2 = 0.6931471805599453
DSA_Q_SCALE = DSA_HEAD_DIM ** -0.5 * LOG2_E

VMEM_LIMIT_BYTES = 56 * 1024 * 1024
LANES = 128
NEG_BIG = -1e30


def _lane_iota(shape):
    return lax.broadcasted_iota(jnp.int32, shape, len(shape) - 1)


def _in_proj_qkv_kernel(x_ref, nw_ref, w_ref, cos_ref, sin_ref, qkv0_ref, qkv1_ref, qkv2_ref, h_ref):
    x = x_ref[...]
    ms = jnp.mean(x * x, axis=-1, keepdims=True)
    h = (x * lax.rsqrt(ms + NORM_EPS) * nw_ref[...]).astype(BF16)
    h_ref[...] = h
    cos = cos_ref[...]
    sin = sin_ref[...]
    low_half = (_lane_iota(cos.shape) % DSA_HEAD_DIM) < (DSA_HEAD_DIM // 2)
    for g, qkv_ref in enumerate((qkv0_ref, qkv1_ref, qkv2_ref)):
        c0 = g * GROUP_COLS
        acc = jnp.dot(h, w_ref[:, c0:c0 + GROUP_COLS], preferred_element_type=F32)
        for c in range(8):
            t = acc[:, c * LANES:(c + 1) * LANES]
            rot = jnp.where(low_half, pltpu.roll(t, LANES - 32, 1), pltpu.roll(t, 32, 1))
            r = t * cos + rot * sin
            if c < 4:
                r = r * DSA_Q_SCALE
            qkv_ref[:, c * LANES:(c + 1) * LANES] = r.astype(BF16)
        qkv_ref[:, 8 * LANES:] = acc[:, 8 * LANES:].astype(BF16)


def _in_proj_rest_kernel(h_ref, wgq_ref, wgk_ref, wgv_ref, w_ref, wl_ref, cw_ref, e512_ref,
                         gdn_ref, gates_ref, logit_ref, xs_ref, *, tiles_per_seq):
    tm = h_ref.shape[0]
    first = pl.program_id(0) % tiles_per_seq == 0

    @pl.when(first)
    def _():
        xs_ref[0:8, :] = jnp.zeros((8, GROUP_COLS), F32)

    @pl.when(jnp.logical_not(first))
    def _():
        xs_ref[0:8, :] = xs_ref[tm:tm + 8, :]

    h = h_ref[...]
    logit_ref[...] = jnp.dot(h, wl_ref[...], preferred_element_type=F32)
    for part, wg_ref in enumerate((wgq_ref, wgk_ref, wgv_ref)):
        xs_ref[8:8 + tm, part * GDN_WIDTH:(part + 1) * GDN_WIDTH] = jnp.dot(
            h, wg_ref[...], preferred_element_type=F32)
    for g in range(GATE_COLS // GROUP_COLS):
        c0 = g * GROUP_COLS
        gates_ref[:, c0:c0 + GROUP_COLS] = jnp.dot(
            h, w_ref[:, c0:c0 + GROUP_COLS], preferred_element_type=F32).astype(BF16)
    for part in range(3):
        cols = slice(part * GDN_WIDTH, (part + 1) * GDN_WIDTH)
        w = cw_ref[:, cols]
        y = w[0:1, :] * xs_ref[5:5 + tm, cols]
        for j in range(1, GDN_CONV):
            y = y + w[j:j + 1, :] * xs_ref[5 + j:5 + j + tm, cols]
        y = _silu_of_twice(y)
        if part < 2:
            ss = jnp.dot((y * y).astype(BF16), e512_ref[...], preferred_element_type=F32)
            y = y * (lax.rsqrt(ss + NORM_EPS) * (GDN_DIM ** -0.5 if part == 0 else 1.0))
        gdn_ref[:, cols] = y.astype(BF16)


def _resident(shape, col_block=0):
    return pl.BlockSpec(shape, lambda i: (0,) * (len(shape) - 1) + (col_block,), pipeline_mode=pl.Buffered(1))


def _in_proj(x2, norm_w, w_bf16, w_gates, conv_w, cos_t, sin_t, seq):
    assert GDN_IN_START % GDN_WIDTH == 0 and GATE_COLS % LOGIT_PAD == 0
    rows = x2.shape[0]
    tm = min(PROJ_ROWS, seq)
    n_seq_tiles = seq // tm
    row = lambda width: pl.BlockSpec((tm, width), lambda i: (i, 0))
    table = pl.BlockSpec((tm, LANES), lambda i: (i % n_seq_tiles, 0))
    head_of_lane = np.arange(GDN_WIDTH) // GDN_DIM
    e512 = jnp.asarray(head_of_lane[:, None] == head_of_lane[None, :], BF16)
    gdn_block = GDN_IN_START // GDN_WIDTH
    *qkv, h = pl.pallas_call(
        _in_proj_qkv_kernel,
        grid=(rows // tm,),
        in_specs=[row(D_MODEL), _resident((1, D_MODEL)), _resident((D_MODEL, QKV_COLS)), table, table],
        out_specs=[row(GROUP_COLS)] * len(DSA_PATTERNS) + [row(D_MODEL)],
        out_shape=[jax.ShapeDtypeStruct((rows, GROUP_COLS), BF16)] * len(DSA_PATTERNS)
        + [jax.ShapeDtypeStruct((rows, D_MODEL), BF16)],
        compiler_params=pltpu.CompilerParams(dimension_semantics=("parallel",), vmem_limit_bytes=VMEM_LIMIT_BYTES),
        name="in_proj_qkv",
    )(x2, norm_w, w_bf16, cos_t, sin_t)
    gdn_qkv, gates, logits = pl.pallas_call(
        functools.partial(_in_proj_rest_kernel, tiles_per_seq=n_seq_tiles),
        grid=(rows // tm,),
        in_specs=[row(D_MODEL), _resident((D_MODEL, GDN_WIDTH), gdn_block),
                  _resident((D_MODEL, GDN_WIDTH), gdn_block + 1), _resident((D_MODEL, GDN_WIDTH), gdn_block + 2),
                  _resident((D_MODEL, GATE_COLS)), _resident((D_MODEL, LOGIT_PAD), GATE_COLS // LOGIT_PAD),
                  _resident(conv_w.shape), _resident(e512.shape)],
        out_specs=[row(GROUP_COLS), row(GATE_COLS), row(LOGIT_PAD)],
        out_shape=[jax.ShapeDtypeStruct((rows, GROUP_COLS), BF16), jax.ShapeDtypeStruct((rows, GATE_COLS), BF16),
                   jax.ShapeDtypeStruct((rows, LOGIT_PAD), F32)],
        scratch_shapes=[pltpu.VMEM((tm + 8, GROUP_COLS), F32)],
        compiler_params=pltpu.CompilerParams(
            dimension_semantics=("arbitrary",), vmem_limit_bytes=VMEM_LIMIT_BYTES),
        name="in_proj_rest",
    )(h, w_bf16, w_bf16, w_bf16, w_gates, w_gates, conv_w, e512)
    return qkv, gdn_qkv, gates, logits


DSA_SPAN = 2048
LSE_GROUP = 16


def _dsa_kernel(qkv_ref, o_ref, lse_ref, stage_ref, qp_ref, kp_ref, vp_ref, lstage_ref, tmp_ref=None,
                *, d, span):
    first = pl.program_id(1) == 0
    npr = span // d
    n_sub = npr // DSA_BLOCK
    kstride = npr + DSA_BLOCK

    def regroup(part, dst_ref, dst_stride, dst_off):
        c_in = part * DSA_WIDTH
        if d == 1:
            dst_ref[dst_off:dst_off + npr, :] = qkv_ref[:, c_in:c_in + DSA_WIDTH]
            return
        for c in range(4):
            stage_ref[c] = qkv_ref[:, c_in + c * LANES:c_in + (c + 1) * LANES].astype(F32)
        if d > 4:
            quarter = span // 4
            for c in range(4):
                for r4 in range(4):
                    tmp_ref[c, r4 * quarter:(r4 + 1) * quarter, :] = stage_ref[c, pl.ds(r4, quarter, stride=4), :]
            slabs, stride, start = tmp_ref, d // 4, lambda r: (r % 4) * quarter + r // 4
        else:
            slabs, stride, start = stage_ref, d, lambda r: r
        for c in range(4):
            for r in range(d):
                lo = r * dst_stride + dst_off
                dst_ref[lo:lo + npr, c * LANES:(c + 1) * LANES] = (
                    slabs[c, pl.ds(start(r), npr, stride=stride), :].astype(BF16))

    @pl.when(first)
    def _():
        for r in range(d):
            kp_ref[r * kstride:r * kstride + DSA_BLOCK, :] = jnp.zeros((DSA_BLOCK, DSA_WIDTH), BF16)
            vp_ref[r * kstride:r * kstride + DSA_BLOCK, :] = jnp.zeros((DSA_BLOCK, DSA_WIDTH), BF16)

    regroup(0, qp_ref, npr, 0)
    regroup(1, kp_ref, kstride, DSA_BLOCK)
    regroup(2, vp_ref, kstride, DSA_BLOCK)

    qi = lax.broadcasted_iota(jnp.int32, (DSA_BLOCK, 2 * DSA_BLOCK), 0)
    kj = lax.broadcasted_iota(jnp.int32, (DSA_BLOCK, 2 * DSA_BLOCK), 1)
    band = (kj >= qi) & (kj <= qi + DSA_BLOCK)
    head_a = _lane_iota((DSA_BLOCK, LANES)) < DSA_HEAD_DIM
    lse_pair = (_lane_iota((DSA_BLOCK, LANES)) % DSA_HEAD_DIM) // LSE_GROUP

    def block(blk, carry):
        r, i = blk // n_sub, blk % n_sub
        q0 = pl.multiple_of(r * npr + i * DSA_BLOCK, DSA_BLOCK)
        k0 = pl.multiple_of(r * kstride + i * DSA_BLOCK, DSA_BLOCK)
        nat0 = r + d * DSA_BLOCK * i
        mask = band & (kj >= jnp.where(first & (i == 0), DSA_BLOCK, 0))
        m_tile = den_tile = None
        for hp in range(DSA_HEADS // 2):
            cols = slice(hp * LANES, (hp + 1) * LANES)
            q2 = qp_ref[pl.ds(q0, DSA_BLOCK), cols]
            k2 = kp_ref[pl.ds(k0, 2 * DSA_BLOCK), cols]
            v2 = vp_ref[pl.ds(k0, 2 * DSA_BLOCK), cols]
            pvs, ms, dens = [], [], []
            for is_a in (True, False):
                sel = head_a if is_a else jnp.logical_not(head_a)
                qh = jnp.where(sel, q2, jnp.zeros_like(q2))
                s = lax.dot_general(qh, k2, (((1,), (1,)), ((), ())), preferred_element_type=F32)
                s = jnp.where(mask, s, NEG_BIG)
                m = jnp.max(s, axis=-1, keepdims=True)
                p = jnp.exp2(s - m)
                dens.append(jnp.sum(p, axis=-1, keepdims=True))
                pvs.append(jnp.dot(p.astype(BF16), v2, preferred_element_type=F32))
                ms.append(m)
            den_pair = jnp.where(head_a, dens[0], dens[1])
            m_pair = jnp.where(head_a, ms[0], ms[1])
            o_pair = jnp.where(head_a, pvs[0], pvs[1]) / den_pair
            den_tile = den_pair if hp == 0 else jnp.where(lse_pair == hp, den_pair, den_tile)
            m_tile = m_pair if hp == 0 else jnp.where(lse_pair == hp, m_pair, m_tile)
            if d == 1:
                o_ref[pl.ds(q0, DSA_BLOCK), cols] = o_pair.astype(o_ref.dtype)
            else:
                stage_ref[hp, pl.ds(nat0, DSA_BLOCK, stride=d), :] = o_pair
        lse_tile = m_tile * LN_2 + jnp.log(den_tile)
        if d == 1:
            lse_ref[pl.ds(q0, DSA_BLOCK), :] = lse_tile
        else:
            lstage_ref[pl.ds(nat0, DSA_BLOCK, stride=d), :] = lse_tile
        return carry

    lax.fori_loop(0, d * n_sub, block, 0, unroll=16)

    for r in range(d):
        lo = r * kstride
        kp_ref[lo:lo + DSA_BLOCK, :] = kp_ref[lo + npr:lo + npr + DSA_BLOCK, :]
        vp_ref[lo:lo + DSA_BLOCK, :] = vp_ref[lo + npr:lo + npr + DSA_BLOCK, :]
    if d > 1:
        for c in range(4):
            o_ref[:, c * LANES:(c + 1) * LANES] = stage_ref[c].astype(o_ref.dtype)
        lse_ref[...] = lstage_ref[...]


def _dsa_attention(qkv, dilation, batch, seq):
    span = min(DSA_SPAN, seq)
    steps = seq // span
    kv_rows = span + dilation * DSA_BLOCK
    row = lambda width: pl.BlockSpec((span, width), lambda b, n: (b * steps + n, 0))
    return pl.pallas_call(
        functools.partial(_dsa_kernel, d=dilation, span=span),
        grid=(batch, steps),
        in_specs=[row(GROUP_COLS)],
        out_specs=[row(DSA_WIDTH), row(LANES)],
        out_shape=[
            jax.ShapeDtypeStruct((batch * seq, DSA_WIDTH), BF16),
            jax.ShapeDtypeStruct((batch * seq, LANES), F32),
        ],
        scratch_shapes=[
            pltpu.VMEM((4, span, LANES), F32),
            pltpu.VMEM((span, DSA_WIDTH), BF16),
            pltpu.VMEM((kv_rows, DSA_WIDTH), BF16),
            pltpu.VMEM((kv_rows, DSA_WIDTH), BF16),
            pltpu.VMEM((span, LANES), F32),
        ] + ([pltpu.VMEM((4, span, LANES), F32)] if dilation > 4 else []),
        compiler_params=pltpu.CompilerParams(
            dimension_semantics=("parallel", "arbitrary"), vmem_limit_bytes=VMEM_LIMIT_BYTES),
        name=f"dsa_attn_d{dilation}",
    )(qkv)


def _silu_of_twice(h):
    return h + h * jnp.tanh(h)


def _sigmoid(x):
    return 1.0 / (1.0 + jnp.exp(-x))


def _out_proj_kernel(o0_ref, o1_ref, o2_ref, l0_ref, l1_ref, l2_ref, g_ref, ob_ref,
                     x_ref, wa_ref, wb_ref, wo_ref, fw_ref, ex_ref, out_ref):
    def gate(block, width):
        return g_ref[:, block * width:(block + 1) * width].astype(F32)

    l0, l1, l2 = l0_ref[...], l1_ref[...], l2_ref[...]
    mx = jnp.maximum(jnp.maximum(l0, l1), l2)
    e0, e1, e2 = jnp.exp(l0 - mx), jnp.exp(l1 - mx), jnp.exp(l2 - mx)
    inv = 1.0 / (e0 + e1 + e2)

    def per_lane(w):
        return jnp.dot(w.astype(BF16), ex_ref[...], preferred_element_type=F32)

    o_a = (per_lane(e0 * inv) * o0_ref[...].astype(F32) + per_lane(e1 * inv) * o1_ref[...].astype(F32)
           + per_lane(e2 * inv) * o2_ref[...].astype(F32))
    a_in = (o_a * _silu_of_twice(gate(GATE_DSA_Z, DSA_WIDTH))).astype(BF16)
    b_in = (ob_ref[...].astype(F32) * _silu_of_twice(gate(GATE_GDN_Z, GDN_WIDTH))).astype(BF16)
    y_a = jnp.dot(a_in, wa_ref[...], preferred_element_type=F32)
    y_b = jnp.dot(b_in, wb_ref[...], preferred_element_type=F32)
    merged2 = (1.0 + jnp.tanh(gate(GATE_A, D_MODEL))) * y_a + (1.0 + jnp.tanh(gate(GATE_B, D_MODEL))) * y_b
    y = x_ref[...] + jnp.dot(merged2.astype(BF16), wo_ref[...], preferred_element_type=F32)
    ms = jnp.mean(y * y, axis=-1, keepdims=True)
    out_ref[...] = y * lax.rsqrt(ms + NORM_EPS) * fw_ref[...]


def _out_proj(o_groups, lse_groups, gates, o_b, x2, w_up_a, w_up_b, w_out, final_w):
    rows = x2.shape[0]
    tm = min(OUT_ROWS, rows)
    row128 = pl.BlockSpec((tm, LANES), lambda i: (i, 0))
    row512 = lambda c: pl.BlockSpec((tm, 512), lambda i: (i, c))
    row1024 = lambda c: pl.BlockSpec((tm, 1024), lambda i: (i, c))
    whole = lambda a: pl.BlockSpec(a.shape, lambda i: (0,) * a.ndim)
    head = np.arange(DSA_WIDTH) // DSA_HEAD_DIM
    expand = jnp.asarray(np.arange(LANES)[:, None] == (head % 2) * DSA_HEAD_DIM + (head // 2) * LSE_GROUP, BF16)
    return pl.pallas_call(
        _out_proj_kernel,
        grid=(rows // tm,),
        in_specs=[row512(0)] * 3 + [row128] * 3 + [
            pl.BlockSpec((tm, GATE_COLS), lambda i: (i, 0)), row512(0), row1024(0),
            whole(w_up_a), whole(w_up_b), whole(w_out), whole(final_w), whole(expand)],
        out_specs=pl.BlockSpec((tm, D_MODEL), lambda i: (i, 0)),
        out_shape=jax.ShapeDtypeStruct((rows, D_MODEL), F32),
        compiler_params=pltpu.CompilerParams(
            dimension_semantics=("parallel",), vmem_limit_bytes=VMEM_LIMIT_BYTES),
        name="out_proj",
    )(*o_groups, *lse_groups, gates, o_b, x2, w_up_a, w_up_b, w_out, final_w, expand)


QUAD = 4 * GDN_DIM
GDN_TILE_ROWS = 1024


def _split_hi_lo(x):
    hi = x.astype(BF16)
    lo = (x - hi.astype(F32)).astype(BF16)
    return hi, lo


def _softplus(x):
    return jnp.maximum(x, 0.0) + jnp.log1p(jnp.exp(-jnp.abs(x)))


def _bmm(a, b):
    return lax.dot_general(a, b, (((2,), (1,)), ((0,), (0,))), preferred_element_type=F32)


def _bmm_nt(a, b):
    return lax.dot_general(a, b, (((2,), (2,)), ((0,), (0,))), preferred_element_type=F32)


def _block_diag(x, bd_mask):
    t = jnp.concatenate([x, x, x, x], axis=1)
    return jnp.where(bd_mask, t, jnp.zeros_like(t))


def _gdn_kernel(qkv_ref, lg_ref, alog_ref, dtb_ref, nw_ref, e512_ref, eb_ref, eg_ref, tri_ref, o_ref,
                beta_ref, gh_ref, gl_ref, lhs_ref, oloc_ref, snew_ref, dec_ref, oraw_ref, state_ref):
    first = pl.program_id(0) == 0
    batch, rows, _ = qkv_ref.shape
    tg = batch * rows
    nc = rows // GDN_CHUNK
    per_step = 2 * batch
    nb = per_step * nc

    @pl.when(first)
    def _():
        state_ref[...] = jnp.zeros_like(state_ref)

    logits = lg_ref[...].reshape(tg, LOGIT_PAD)
    beta_hi, beta_lo = _split_hi_lo(_sigmoid(logits))
    g_hi, g_lo = _split_hi_lo(-jnp.exp(alog_ref[...]) * _softplus(logits + dtb_ref[...]))
    beta_ref[...] = (jnp.dot(beta_hi, eb_ref[...], preferred_element_type=F32)
                     + jnp.dot(beta_lo, eb_ref[...], preferred_element_type=F32))
    gh_ref[...] = jnp.dot(g_hi, eg_ref[...], preferred_element_type=F32).astype(BF16)
    gl_ref[...] = jnp.dot(g_lo, eg_ref[...], preferred_element_type=F32).astype(BF16)

    row = lax.broadcasted_iota(jnp.int32, (GDN_CHUNK, QUAD), 0)
    col = lax.broadcasted_iota(jnp.int32, (GDN_CHUNK, QUAD), 1) % GDN_CHUNK
    incl = row >= col
    strict = row > col
    eye = (row == col).astype(F32)
    bd_mask = (lax.broadcasted_iota(jnp.int32, (QUAD, QUAD), 0) // GDN_DIM
               == lax.broadcasted_iota(jnp.int32, (QUAD, QUAD), 1) // GDN_DIM)
    tri = tri_ref[...]

    def chunk_quads(tile, dtype):
        a = tile.reshape(batch, nc, GDN_CHUNK, GDN_WIDTH).astype(dtype)
        a = jnp.stack([a[b, j] for j in range(nc) for b in range(batch)])
        return jnp.stack([a[:, :, :QUAD], a[:, :, QUAD:]], axis=1).reshape(nb, GDN_CHUNK, QUAD)

    q, k, v = (chunk_quads(qkv_ref[:, :, part * GDN_WIDTH:(part + 1) * GDN_WIDTH], F32) for part in range(3))
    beta = chunk_quads(beta_ref[...], F32)
    gh, gl = chunk_quads(gh_ref[...], BF16), chunk_quads(gl_ref[...], BF16)

    zero = jnp.zeros_like(gh)
    rhs_hi = jnp.concatenate([gh, jnp.where(strict, gh, zero)], axis=-1)
    rhs_lo = jnp.concatenate([gl, jnp.where(strict, gl, zero)], axis=-1)
    tri_b = jnp.broadcast_to(tri, (nb, GDN_CHUNK, GDN_CHUNK))
    gd = _bmm(tri_b, rhs_hi) + _bmm(tri_b, rhs_lo)
    g_cum, d_pair = gd[:, :, :QUAD], gd[:, :, QUAD:]
    decay_incl = jnp.where(incl, jnp.exp(d_pair), 0.0)
    decay_strict = jnp.where(strict, decay_incl, 0.0)
    exp_g = jnp.exp(g_cum)
    g_last = g_cum[:, GDN_CHUNK - 1:GDN_CHUNK, :]
    k_dec = k * jnp.exp(g_last - g_cum)
    k_beta = k * beta

    k_bd = _block_diag(k.astype(BF16), bd_mask)
    aa = _bmm_nt(jnp.concatenate([k_beta, q], axis=1).astype(BF16), k_bd)
    a = aa[:, :GDN_CHUNK] * decay_strict
    attn = aa[:, GDN_CHUNK:] * decay_incl

    p = a
    t = eye - a
    p = _bmm(p.astype(BF16), _block_diag(p.astype(BF16), bd_mask))
    for _ in range(4):
        r = _bmm(jnp.concatenate([p, t], axis=1).astype(BF16), _block_diag(p.astype(BF16), bd_mask))
        p = r[:, :GDN_CHUNK]
        t = t + r[:, GDN_CHUNK:]
    t = t + _bmm(t.astype(BF16), _block_diag(p.astype(BF16), bd_mask))
    t16 = t.astype(BF16)
    u = _bmm(t16, _block_diag((v * beta).astype(BF16), bd_mask))
    w = _bmm(t16, _block_diag((k_beta * exp_g).astype(BF16), bd_mask))

    w16, u16 = w.astype(BF16), u.astype(BF16)
    kt = _bmm(jnp.swapaxes(k_dec, 1, 2).astype(BF16), jnp.concatenate([w16, u16], axis=-1))
    ao = _bmm(attn.astype(BF16),
              jnp.concatenate([_block_diag(w16, bd_mask), _block_diag(u16, bd_mask)], axis=-1))
    lhs_ref[...] = jnp.concatenate(
        [q * exp_g - ao[:, :, :QUAD], jnp.where(bd_mask, -kt[:, :, :QUAD], 0.0)], axis=1).astype(BF16)
    oloc_ref[...] = ao[:, :, QUAD:]
    snew_ref[...] = jnp.where(bd_mask, kt[:, :, QUAD:], 0.0)
    dec_ref[...] = jnp.broadcast_to(jnp.exp(g_last), (nb, 8, QUAD))

    for j in range(nc):
        sl = slice(j * per_step, (j + 1) * per_step)
        state = state_ref[...]
        r = _bmm(lhs_ref[sl], state.astype(BF16))
        state_ref[...] = state * dec_ref[sl][:, 0:1, :] + r[:, GDN_CHUNK:] + snew_ref[sl]
        o = r[:, :GDN_CHUNK] + oloc_ref[sl]
        for b in range(batch):
            lo = b * rows + j * GDN_CHUNK
            oraw_ref[lo:lo + GDN_CHUNK, 0:QUAD] = o[2 * b]
            oraw_ref[lo:lo + GDN_CHUNK, QUAD:] = o[2 * b + 1]

    o_all = oraw_ref[...]
    ms = jnp.dot((o_all * o_all).astype(BF16), e512_ref[...], preferred_element_type=F32) * (1.0 / GDN_DIM)
    o_ref[...] = (o_all * lax.rsqrt(ms + NORM_EPS) * nw_ref[...]).astype(o_ref.dtype).reshape(batch, rows, GDN_WIDTH)


def _gdn(gdn_qkv, logits, alog_row, dtb_row, norm_row, batch, seq):
    tg = min(GDN_TILE_ROWS, batch * seq)
    rows = tg // batch
    steps = seq // rows
    nb = 2 * (tg // GDN_CHUNK)
    head_of_lane = np.arange(GDN_WIDTH) // GDN_DIM
    e512 = jnp.asarray(head_of_lane[:, None] == head_of_lane[None, :], BF16)
    e_beta = jnp.asarray(np.arange(LOGIT_PAD)[:, None] == head_of_lane[None, :], BF16)
    e_g = jnp.asarray(np.arange(LOGIT_PAD)[:, None] == head_of_lane[None, :] + GDN_HEADS, BF16)
    tri = jnp.asarray(np.tril(np.ones((GDN_CHUNK, GDN_CHUNK))), BF16)

    cur = lambda width, c: pl.BlockSpec((batch, rows, width), lambda n: (0, n, c))
    whole = lambda a: pl.BlockSpec(a.shape, lambda n: (0,) * a.ndim)
    consts = (alog_row, dtb_row, norm_row, e512, e_beta, e_g, tri)
    qkv3 = gdn_qkv.reshape(batch, seq, GROUP_COLS)
    out = pl.pallas_call(
        _gdn_kernel,
        grid=(steps,),
        in_specs=[cur(GROUP_COLS, 0), cur(LOGIT_PAD, 0)]
        + [whole(a) for a in consts],
        out_specs=cur(GDN_WIDTH, 0),
        out_shape=jax.ShapeDtypeStruct((batch, seq, GDN_WIDTH), BF16),
        scratch_shapes=[
            pltpu.VMEM((tg, GDN_WIDTH), F32),
            pltpu.VMEM((tg, GDN_WIDTH), BF16),
            pltpu.VMEM((tg, GDN_WIDTH), BF16),
            pltpu.VMEM((nb, GDN_CHUNK + QUAD, QUAD), BF16),
            pltpu.VMEM((nb, GDN_CHUNK, QUAD), F32),
            pltpu.VMEM((nb, QUAD, QUAD), F32),
            pltpu.VMEM((nb, 8, QUAD), F32),
            pltpu.VMEM((tg, GDN_WIDTH), F32),
            pltpu.VMEM((2 * batch, QUAD, QUAD), F32),
        ],
        compiler_params=pltpu.CompilerParams(
            dimension_semantics=("arbitrary",), vmem_limit_bytes=VMEM_LIMIT_BYTES),
        name="gdn",
    )(qkv3, logits.reshape(batch, seq, LOGIT_PAD), *consts)
    return out.reshape(batch * seq, GDN_WIDTH)


def _rope_tables(seq):
    inv_freq = np.float32(ROPE_THETA) ** (-np.arange(0, DSA_HEAD_DIM, 2, dtype=np.float32) / np.float32(DSA_HEAD_DIM))
    ang = np.arange(seq, dtype=np.float32)[:, None] * inv_freq[None, :]
    cos, sin = np.cos(ang), np.sin(ang)
    return (jnp.asarray(np.concatenate([cos, cos, cos, cos], axis=-1), F32),
            jnp.asarray(np.concatenate([-sin, sin, -sin, sin], axis=-1), F32))


def kernel(x, norm_w, w_in, conv_w, a_log, dt_bias, gdn_norm_w, w_up_a, w_up_b, w_out, final_norm_w):
    batch, seq, _ = x.shape
    assert norm_w.shape[0] == 1, "the final RMSNorm is fused into the (single) layer's output kernel"
    cos_t, sin_t = _rope_tables(seq)
    x2 = x.reshape(batch * seq, D_MODEL)
    w = w_in[0].astype(BF16)
    gates = LOGIT_START + 2 * GDN_HEADS
    dsa_z, gdn_z = QKV_COLS, GDN_IN_START + GROUP_COLS
    half = lambda a: a * jnp.asarray(0.5, a.dtype)
    w_gates = jnp.concatenate(
        [half(w[:, gates:]), half(w[:, dsa_z:GDN_IN_START]), half(w[:, gdn_z:LOGIT_START]),
         w[:, LOGIT_START:gates], jnp.zeros((D_MODEL, LOGIT_PAD - 2 * GDN_HEADS), BF16)], axis=1)
    qkv, gdn_qkv, gate_cols, logits = _in_proj(
        x2, norm_w[0][None, :], w, w_gates, half(conv_w[0]), cos_t, sin_t, seq)

    o_groups, lse_groups = [], []
    for g, (_, dilation) in enumerate(DSA_PATTERNS):
        o_g, lse_g = _dsa_attention(qkv[g], dilation, batch, seq)
        o_groups.append(o_g)
        lse_groups.append(lse_g)

    pad8 = lambda p: jnp.pad(p.astype(F32), (GDN_HEADS, LOGIT_PAD - 2 * GDN_HEADS))[None, :]
    o_b = _gdn(gdn_qkv, logits, pad8(a_log[0]), pad8(dt_bias[0]),
               jnp.tile(gdn_norm_w[0].astype(F32), GDN_HEADS)[None, :], batch, seq)

    out = _out_proj(o_groups, lse_groups, gate_cols, o_b, x2,
                    w_up_a[0].astype(BF16), w_up_b[0].astype(BF16), half(w_out[0]).astype(BF16),
                    final_norm_w[None, :])
    return out.reshape(batch, seq, D_MODEL)
```

```python
import functools

import numpy as np
import jax
import jax.numpy as jnp
from jax import lax
from jax.experimental import pallas as pl
from jax.experimental.pallas import tpu as pltpu

F32 = jnp.float32
BF16 = jnp.bfloat16

D_MODEL = 1024
DSA_PATTERNS = ((128, 1), (512, 4), (2048, 16))
DSA_HEADS = 8
DSA_HEAD_DIM = 64
DSA_WIDTH = DSA_HEADS * DSA_HEAD_DIM
DSA_BLOCK = 128
ROPE_THETA = 10000.0
GDN_HEADS = 8
GDN_DIM = 64
GDN_WIDTH = GDN_HEADS * GDN_DIM
GDN_CONV = 4
GDN_CHUNK = 64
NORM_EPS = 1e-6

QKV_COLS = 3 * 3 * DSA_WIDTH
GROUP_COLS = 3 * DSA_WIDTH
GATE_COLS = 3072
GATE_A, GATE_B = 0, 1
GATE_DSA_Z, GATE_GDN_Z = 4, 5
GDN_IN_START = QKV_COLS + DSA_WIDTH
LOGIT_START = 7168
LOGIT_PAD = 128
PROJ_ROWS = 1024
OUT_ROWS = 1024
LOG2_E = 1.4426950408889634
LN_2 = 0.6931471805599453
DSA_Q_SCALE = DSA_HEAD_DIM ** -0.5 * LOG2_E

VMEM_LIMIT_BYTES = 56 * 1024 * 1024
LANES = 128
NEG_BIG = -1e30


def _lane_iota(shape):
    return lax.broadcasted_iota(jnp.int32, shape, len(shape) - 1)


def _in_proj_qkv_kernel(x_ref, nw_ref, w_ref, cos_ref, sin_ref, qkv0_ref, qkv1_ref, qkv2_ref, h_ref):
    x = x_ref[...]
    ms = jnp.mean(x * x, axis=-1, keepdims=True)
    h = (x * lax.rsqrt(ms + NORM_EPS) * nw_ref[...]).astype(BF16)
    h_ref[...] = h
    cos = cos_ref[...]
    sin = sin_ref[...]
    low_half = (_lane_iota(cos.shape) % DSA_HEAD_DIM) < (DSA_HEAD_DIM // 2)
    for g, qkv_ref in enumerate((qkv0_ref, qkv1_ref, qkv2_ref)):
        c0 = g * GROUP_COLS
        acc = jnp.dot(h, w_ref[:, c0:c0 + GROUP_COLS], preferred_element_type=F32)
        for c in range(8):
            t = acc[:, c * LANES:(c + 1) * LANES]
            rot = jnp.where(low_half, pltpu.roll(t, LANES - 32, 1), pltpu.roll(t, 32, 1))
            r = t * cos + rot * sin
            if c < 4:
                r = r * DSA_Q_SCALE
            qkv_ref[:, c * LANES:(c + 1) * LANES] = r.astype(BF16)
        qkv_ref[:, 8 * LANES:] = acc[:, 8 * LANES:].astype(BF16)


def _in_proj_rest_kernel(h_ref, wgq_ref, wgk_ref, wgv_ref, w_ref, wl_ref, cw_ref, e512_ref,
                         gdn_ref, gates_ref, logit_ref, xs_ref, *, tiles_per_seq):
    tm = h_ref.shape[0]
    first = pl.program_id(0) % tiles_per_seq == 0

    @pl.when(first)
    def _():
        xs_ref[0:8, :] = jnp.zeros((8, GROUP_COLS), F32)

    @pl.when(jnp.logical_not(first))
    def _():
        xs_ref[0:8, :] = xs_ref[tm:tm + 8, :]

    h = h_ref[...]
    logit_ref[...] = jnp.dot(h, wl_ref[...], preferred_element_type=F32)
    for part, wg_ref in enumerate((wgq_ref, wgk_ref, wgv_ref)):
        xs_ref[8:8 + tm, part * GDN_WIDTH:(part + 1) * GDN_WIDTH] = jnp.dot(
            h, wg_ref[...], preferred_element_type=F32)
    for g in range(GATE_COLS // GROUP_COLS):
        c0 = g * GROUP_COLS
        gates_ref[:, c0:c0 + GROUP_COLS] = jnp.dot(
            h, w_ref[:, c0:c0 + GROUP_COLS], preferred_element_type=F32).astype(BF16)
    for part in range(3):
        cols = slice(part * GDN_WIDTH, (part + 1) * GDN_WIDTH)
        w = cw_ref[:, cols]
        y = w[0:1, :] * xs_ref[5:5 + tm, cols]
        for j in range(1, GDN_CONV):
            y = y + w[j:j + 1, :] * xs_ref[5 + j:5 + j + tm, cols]
        y = _silu_of_twice(y)
        if part < 2:
            ss = jnp.dot((y * y).astype(BF16), e512_ref[...], preferred_element_type=F32)
            y = y * (lax.rsqrt(ss + NORM_EPS) * (GDN_DIM ** -0.5 if part == 0 else 1.0))
        gdn_ref[:, cols] = y.astype(BF16)


def _resident(shape, col_block=0):
    return pl.BlockSpec(shape, lambda i: (0,) * (len(shape) - 1) + (col_block,), pipeline_mode=pl.Buffered(1))


def _in_proj(x2, norm_w, w_bf16, w_gates, conv_w, cos_t, sin_t, seq):
    assert GDN_IN_START % GDN_WIDTH == 0 and GATE_COLS % LOGIT_PAD == 0
    rows = x2.shape[0]
    tm = min(PROJ_ROWS, seq)
    n_seq_tiles = seq // tm
    row = lambda width: pl.BlockSpec((tm, width), lambda i: (i, 0))
    table = pl.BlockSpec((tm, LANES), lambda i: (i % n_seq_tiles, 0))
    head_of_lane = np.arange(GDN_WIDTH) // GDN_DIM
    e512 = jnp.asarray(head_of_lane[:, None] == head_of_lane[None, :], BF16)
    gdn_block = GDN_IN_START // GDN_WIDTH
    *qkv, h = pl.pallas_call(
        _in_proj_qkv_kernel,
        grid=(rows // tm,),
        in_specs=[row(D_MODEL), _resident((1, D_MODEL)), _resident((D_MODEL, QKV_COLS)), table, table],
        out_specs=[row(GROUP_COLS)] * len(DSA_PATTERNS) + [row(D_MODEL)],
        out_shape=[jax.ShapeDtypeStruct((rows, GROUP_COLS), BF16)] * len(DSA_PATTERNS)
        + [jax.ShapeDtypeStruct((rows, D_MODEL), BF16)],
        compiler_params=pltpu.CompilerParams(dimension_semantics=("parallel",), vmem_limit_bytes=VMEM_LIMIT_BYTES),
        name="in_proj_qkv",
    )(x2, norm_w, w_bf16, cos_t, sin_t)
    gdn_qkv, gates, logits = pl.pallas_call(
        functools.partial(_in_proj_rest_kernel, tiles_per_seq=n_seq_tiles),
        grid=(rows // tm,),
        in_specs=[row(D_MODEL), _resident((D_MODEL, GDN_WIDTH), gdn_block),
                  _resident((D_MODEL, GDN_WIDTH), gdn_block + 1), _resident((D_MODEL, GDN_WIDTH), gdn_block + 2),
                  _resident((D_MODEL, GATE_COLS)), _resident((D_MODEL, LOGIT_PAD), GATE_COLS // LOGIT_PAD),
                  _resident(conv_w.shape), _resident(e512.shape)],
        out_specs=[row(GROUP_COLS), row(GATE_COLS), row(LOGIT_PAD)],
        out_shape=[jax.ShapeDtypeStruct((rows, GROUP_COLS), BF16), jax.ShapeDtypeStruct((rows, GATE_COLS), BF16),
                   jax.ShapeDtypeStruct((rows, LOGIT_PAD), F32)],
        scratch_shapes=[pltpu.VMEM((tm + 8, GROUP_COLS), F32)],
        compiler_params=pltpu.CompilerParams(
            dimension_semantics=("arbitrary",), vmem_limit_bytes=VMEM_LIMIT_BYTES),
        name="in_proj_rest",
    )(h, w_bf16, w_bf16, w_bf16, w_gates, w_gates, conv_w, e512)
    return qkv, gdn_qkv, gates, logits


DSA_SPAN = 2048
LSE_GROUP = 16


def _dsa_kernel(qkv_ref, o_ref, lse_ref, stage_ref, qp_ref, kp_ref, vp_ref, lstage_ref, bias_ref, tmp_ref=None,
                *, d, span):
    first = pl.program_id(1) == 0
    npr = span // d
    n_sub = npr // DSA_BLOCK
    kstride = npr + DSA_BLOCK

    def regroup(part, dst_ref, dst_stride, dst_off):
        c_in = part * DSA_WIDTH
        if d == 1:
            dst_ref[dst_off:dst_off + npr, :] = qkv_ref[:, c_in:c_in + DSA_WIDTH]
            return
        for c in range(4):
            stage_ref[c] = qkv_ref[:, c_in + c * LANES:c_in + (c + 1) * LANES].astype(F32)
        if d > 4:
            quarter = span // 4
            for c in range(4):
                for r4 in range(4):
                    tmp_ref[c, r4 * quarter:(r4 + 1) * quarter, :] = stage_ref[c, pl.ds(r4, quarter, stride=4), :]
            slabs, stride, start = tmp_ref, d // 4, lambda r: (r % 4) * quarter + r // 4
        else:
            slabs, stride, start = stage_ref, d, lambda r: r
        for c in range(4):
            for r in range(d):
                lo = r * dst_stride + dst_off
                dst_ref[lo:lo + npr, c * LANES:(c + 1) * LANES] = (
                    slabs[c, pl.ds(start(r), npr, stride=stride), :].astype(BF16))

    @pl.when(first)
    def _():
        for r in range(d):
            kp_ref[r * kstride:r * kstride + DSA_BLOCK, :] = jnp.zeros((DSA_BLOCK, DSA_WIDTH), BF16)
            vp_ref[r * kstride:r * kstride + DSA_BLOCK, :] = jnp.zeros((DSA_BLOCK, DSA_WIDTH), BF16)

    regroup(0, qp_ref, npr, 0)
    regroup(1, kp_ref, kstride, DSA_BLOCK)
    regroup(2, vp_ref, kstride, DSA_BLOCK)

    qi = lax.broadcasted_iota(jnp.int32, (DSA_BLOCK, 2 * DSA_BLOCK), 0)
    kj = lax.broadcasted_iota(jnp.int32, (DSA_BLOCK, 2 * DSA_BLOCK), 1)
    band = (kj >= qi) & (kj <= qi + DSA_BLOCK)
    bias_ref[0] = jnp.where(band, 0.0, NEG_BIG)
    bias_ref[1] = jnp.where(band & (kj >= DSA_BLOCK), 0.0, NEG_BIG)
    head_a = _lane_iota((DSA_BLOCK, LANES)) < DSA_HEAD_DIM
    lse_pair = (_lane_iota((DSA_BLOCK, LANES)) % DSA_HEAD_DIM) // LSE_GROUP

    def block(blk, carry):
        r, i = blk // n_sub, blk % n_sub
        q0 = pl.multiple_of(r * npr + i * DSA_BLOCK, DSA_BLOCK)
        k0 = pl.multiple_of(r * kstride + i * DSA_BLOCK, DSA_BLOCK)
        if d > 4:
            out_ref, lout_ref, out_stride = tmp_ref, tmp_ref.at[4], d // 4
            nat0 = (r % 4) * (span // 4) + r // 4 + out_stride * DSA_BLOCK * i
        else:
            out_ref, lout_ref, out_stride = stage_ref, lstage_ref, d
            nat0 = r + d * DSA_BLOCK * i
        no_prev = jnp.where(first & (i == 0), 1, 0)
        m_tile = den_tile = None
        for hp in range(DSA_HEADS // 2):
            cols = slice(hp * LANES, (hp + 1) * LANES)
            q2 = qp_ref[pl.ds(q0, DSA_BLOCK), cols]
            k2 = kp_ref[pl.ds(k0, 2 * DSA_BLOCK), cols]
            v2 = vp_ref[pl.ds(k0, 2 * DSA_BLOCK), cols]
            pvs, ms, dens = [], [], []
            for is_a in (True, False):
                sel = head_a if is_a else jnp.logical_not(head_a)
                qh = jnp.where(sel, q2, jnp.zeros_like(q2))
                s = lax.dot_general(qh, k2, (((1,), (1,)), ((), ())), preferred_element_type=F32)
                s = s + bias_ref[no_prev]
                m = jnp.max(s, axis=-1, keepdims=True)
                p = jnp.exp2(s - m)
                dens.append(jnp.sum(p, axis=-1, keepdims=True))
                pvs.append(jnp.dot(p.astype(BF16), v2, preferred_element_type=F32))
                ms.append(m)
            den_pair = jnp.where(head_a, dens[0], dens[1])
            m_pair = jnp.where(head_a, ms[0], ms[1])
            o_pair = jnp.where(head_a, pvs[0], pvs[1]) / den_pair
            den_tile = den_pair if hp == 0 else jnp.where(lse_pair == hp, den_pair, den_tile)
            m_tile = m_pair if hp == 0 else jnp.where(lse_pair == hp, m_pair, m_tile)
            if d == 1:
                o_ref[pl.ds(q0, DSA_BLOCK), cols] = o_pair.astype(o_ref.dtype)
            else:
                out_ref[hp, pl.ds(nat0, DSA_BLOCK, stride=out_stride), :] = o_pair
        lse_tile = m_tile * LN_2 + jnp.log(den_tile)
        if d == 1:
            lse_ref[pl.ds(q0, DSA_BLOCK), :] = lse_tile
        else:
            lout_ref[pl.ds(nat0, DSA_BLOCK, stride=out_stride), :] = lse_tile
        return carry

    lax.fori_loop(0, d * n_sub, block, 0, unroll=16)

    for r in range(d):
        lo = r * kstride
        kp_ref[lo:lo + DSA_BLOCK, :] = kp_ref[lo + npr:lo + npr + DSA_BLOCK, :]
        vp_ref[lo:lo + DSA_BLOCK, :] = vp_ref[lo + npr:lo + npr + DSA_BLOCK, :]
    if d > 4:
        quarter = span // 4
        for r4 in range(4):
            rows = slice(r4 * quarter, (r4 + 1) * quarter)
            for c in range(4):
                stage_ref[c, pl.ds(r4, quarter, stride=4), :] = tmp_ref[c, rows, :]
            lstage_ref[pl.ds(r4, quarter, stride=4), :] = tmp_ref[4, rows, :]
    if d > 1:
        for c in range(4):
            o_ref[:, c * LANES:(c + 1) * LANES] = stage_ref[c].astype(o_ref.dtype)
        lse_ref[...] = lstage_ref[...]


def _dsa_attention(qkv, dilation, batch, seq):
    span = min(DSA_SPAN, seq)
    steps = seq // span
    kv_rows = span + dilation * DSA_BLOCK
    row = lambda width: pl.BlockSpec((span, width), lambda b, n: (b * steps + n, 0))
    return pl.pallas_call(
        functools.partial(_dsa_kernel, d=dilation, span=span),
        grid=(batch, steps),
        in_specs=[row(GROUP_COLS)],
        out_specs=[row(DSA_WIDTH), row(LANES)],
        out_shape=[
            jax.ShapeDtypeStruct((batch * seq, DSA_WIDTH), BF16),
            jax.ShapeDtypeStruct((batch * seq, LANES), F32),
        ],
        scratch_shapes=[
            pltpu.VMEM((4, span, LANES), F32),
            pltpu.VMEM((span, DSA_WIDTH), BF16),
            pltpu.VMEM((kv_rows, DSA_WIDTH), BF16),
            pltpu.VMEM((kv_rows, DSA_WIDTH), BF16),
            pltpu.VMEM((span, LANES), F32),
            pltpu.VMEM((2, DSA_BLOCK, 2 * DSA_BLOCK), F32),
        ] + ([pltpu.VMEM((5, span, LANES), F32)] if dilation > 4 else []),
        compiler_params=pltpu.CompilerParams(
            dimension_semantics=("parallel", "arbitrary"), vmem_limit_bytes=VMEM_LIMIT_BYTES),
        name=f"dsa_attn_d{dilation}",
    )(qkv)


def _silu_of_twice(h):
    return h + h * jnp.tanh(h)


def _sigmoid(x):
    return 1.0 / (1.0 + jnp.exp(-x))


def _out_proj_kernel(o0_ref, o1_ref, o2_ref, l0_ref, l1_ref, l2_ref, g_ref, ob_ref,
                     x_ref, wa_ref, wb_ref, wo_ref, fw_ref, ex_ref, out_ref):
    def gate(block, width):
        return g_ref[:, block * width:(block + 1) * width].astype(F32)

    l0, l1, l2 = l0_ref[...], l1_ref[...], l2_ref[...]
    mx = jnp.maximum(jnp.maximum(l0, l1), l2)
    e0, e1, e2 = jnp.exp(l0 - mx), jnp.exp(l1 - mx), jnp.exp(l2 - mx)
    inv = 1.0 / (e0 + e1 + e2)

    def per_lane(w):
        return jnp.dot(w.astype(BF16), ex_ref[...], preferred_element_type=F32)

    o_a = (per_lane(e0 * inv) * o0_ref[...].astype(F32) + per_lane(e1 * inv) * o1_ref[...].astype(F32)
           + per_lane(e2 * inv) * o2_ref[...].astype(F32))
    a_in = (o_a * _silu_of_twice(gate(GATE_DSA_Z, DSA_WIDTH))).astype(BF16)
    b_in = (ob_ref[...].astype(F32) * _silu_of_twice(gate(GATE_GDN_Z, GDN_WIDTH))).astype(BF16)
    y_a = jnp.dot(a_in, wa_ref[...], preferred_element_type=F32)
    y_b = jnp.dot(b_in, wb_ref[...], preferred_element_type=F32)
    merged2 = (1.0 + jnp.tanh(gate(GATE_A, D_MODEL))) * y_a + (1.0 + jnp.tanh(gate(GATE_B, D_MODEL))) * y_b
    y = x_ref[...] + jnp.dot(merged2.astype(BF16), wo_ref[...], preferred_element_type=F32)
    ms = jnp.mean(y * y, axis=-1, keepdims=True)
    out_ref[...] = y * lax.rsqrt(ms + NORM_EPS) * fw_ref[...]


def _out_proj(o_groups, lse_groups, gates, o_b, x2, w_up_a, w_up_b, w_out, final_w):
    rows = x2.shape[0]
    tm = min(OUT_ROWS, rows)
    row128 = pl.BlockSpec((tm, LANES), lambda i: (i, 0))
    row512 = lambda c: pl.BlockSpec((tm, 512), lambda i: (i, c))
    row1024 = lambda c: pl.BlockSpec((tm, 1024), lambda i: (i, c))
    whole = lambda a: pl.BlockSpec(a.shape, lambda i: (0,) * a.ndim)
    head = np.arange(DSA_WIDTH) // DSA_HEAD_DIM
    expand = jnp.asarray(np.arange(LANES)[:, None] == (head % 2) * DSA_HEAD_DIM + (head // 2) * LSE_GROUP, BF16)
    return pl.pallas_call(
        _out_proj_kernel,
        grid=(rows // tm,),
        in_specs=[row512(0)] * 3 + [row128] * 3 + [
            pl.BlockSpec((tm, GATE_COLS), lambda i: (i, 0)), row512(0), row1024(0),
            whole(w_up_a), whole(w_up_b), whole(w_out), whole(final_w), whole(expand)],
        out_specs=pl.BlockSpec((tm, D_MODEL), lambda i: (i, 0)),
        out_shape=jax.ShapeDtypeStruct((rows, D_MODEL), F32),
        compiler_params=pltpu.CompilerParams(
            dimension_semantics=("parallel",), vmem_limit_bytes=VMEM_LIMIT_BYTES),
        name="out_proj",
    )(*o_groups, *lse_groups, gates, o_b, x2, w_up_a, w_up_b, w_out, final_w, expand)


QUAD = 4 * GDN_DIM
GDN_TILE_ROWS = 1024


def _split_hi_lo(x):
    hi = x.astype(BF16)
    lo = (x - hi.astype(F32)).astype(BF16)
    return hi, lo


def _softplus(x):
    return jnp.maximum(x, 0.0) + jnp.log1p(jnp.exp(-jnp.abs(x)))


def _bmm(a, b):
    return lax.dot_general(a, b, (((2,), (1,)), ((0,), (0,))), preferred_element_type=F32)


def _bmm_nt(a, b):
    return lax.dot_general(a, b, (((2,), (2,)), ((0,), (0,))), preferred_element_type=F32)


def _block_diag(x, bd_mask):
    t = jnp.concatenate([x, x, x, x], axis=1)
    return jnp.where(bd_mask, t, jnp.zeros_like(t))


def _gdn_kernel(qkv_ref, lg_ref, alog_ref, dtb_ref, nw_ref, e512_ref, eb_ref, eg_ref, tri_ref, o_ref,
                beta_ref, gh_ref, gl_ref, lhs_ref, oloc_ref, snew_ref, dec_ref, oraw_ref, state_ref):
    first = pl.program_id(0) == 0
    batch, rows, _ = qkv_ref.shape
    tg = batch * rows
    nc = rows // GDN_CHUNK
    per_step = 2 * batch
    nb = per_step * nc

    @pl.when(first)
    def _():
        state_ref[...] = jnp.zeros_like(state_ref)

    logits = lg_ref[...].reshape(tg, LOGIT_PAD)
    beta_hi, beta_lo = _split_hi_lo(_sigmoid(logits))
    g_hi, g_lo = _split_hi_lo(-jnp.exp(alog_ref[...]) * _softplus(logits + dtb_ref[...]))
    beta_ref[...] = (jnp.dot(beta_hi, eb_ref[...], preferred_element_type=F32)
                     + jnp.dot(beta_lo, eb_ref[...], preferred_element_type=F32))
    gh_ref[...] = jnp.dot(g_hi, eg_ref[...], preferred_element_type=F32).astype(BF16)
    gl_ref[...] = jnp.dot(g_lo, eg_ref[...], preferred_element_type=F32).astype(BF16)

    row = lax.broadcasted_iota(jnp.int32, (GDN_CHUNK, QUAD), 0)
    col = lax.broadcasted_iota(jnp.int32, (GDN_CHUNK, QUAD), 1) % GDN_CHUNK
    incl = row >= col
    strict = row > col
    eye = (row == col).astype(F32)
    bd_mask = (lax.broadcasted_iota(jnp.int32, (QUAD, QUAD), 0) // GDN_DIM
               == lax.broadcasted_iota(jnp.int32, (QUAD, QUAD), 1) // GDN_DIM)
    tri = tri_ref[...]

    def chunk_quads(tile, dtype):
        a = tile.reshape(batch, nc, GDN_CHUNK, GDN_WIDTH).astype(dtype)
        a = jnp.stack([a[b, j] for j in range(nc) for b in range(batch)])
        return jnp.stack([a[:, :, :QUAD], a[:, :, QUAD:]], axis=1).reshape(nb, GDN_CHUNK, QUAD)

    q, k, v = (chunk_quads(qkv_ref[:, :, part * GDN_WIDTH:(part + 1) * GDN_WIDTH], F32) for part in range(3))
    beta = chunk_quads(beta_ref[...], F32)
    gh, gl = chunk_quads(gh_ref[...], BF16), chunk_quads(gl_ref[...], BF16)

    zero = jnp.zeros_like(gh)
    rhs_hi = jnp.concatenate([gh, jnp.where(strict, gh, zero)], axis=-1)
    rhs_lo = jnp.concatenate([gl, jnp.where(strict, gl, zero)], axis=-1)
    tri_b = jnp.broadcast_to(tri, (nb, GDN_CHUNK, GDN_CHUNK))
    gd = _bmm(tri_b, rhs_hi) + _bmm(tri_b, rhs_lo)
    g_cum, d_pair = gd[:, :, :QUAD], gd[:, :, QUAD:]
    decay_incl = jnp.where(incl, jnp.exp(d_pair), 0.0)
    decay_strict = jnp.where(strict, decay_incl, 0.0)
    exp_g = jnp.exp(g_cum)
    g_last = g_cum[:, GDN_CHUNK - 1:GDN_CHUNK, :]
    k_dec = k * jnp.exp(g_last - g_cum)
    k_beta = k * beta

    k_bd = _block_diag(k.astype(BF16), bd_mask)
    aa = _bmm_nt(jnp.concatenate([k_beta, q], axis=1).astype(BF16), k_bd)
    a = aa[:, :GDN_CHUNK] * decay_strict
    attn = aa[:, GDN_CHUNK:] * decay_incl

    p = a
    t = eye - a
    p = _bmm(p.astype(BF16), _block_diag(p.astype(BF16), bd_mask))
    for _ in range(4):
        r = _bmm(jnp.concatenate([p, t], axis=1).astype(BF16), _block_diag(p.astype(BF16), bd_mask))
        p = r[:, :GDN_CHUNK]
        t = t + r[:, GDN_CHUNK:]
    t = t + _bmm(t.astype(BF16), _block_diag(p.astype(BF16), bd_mask))
    t16 = t.astype(BF16)
    u = _bmm(t16, _block_diag((v * beta).astype(BF16), bd_mask))
    w = _bmm(t16, _block_diag((k_beta * exp_g).astype(BF16), bd_mask))

    w16, u16 = w.astype(BF16), u.astype(BF16)
    kt = _bmm(jnp.swapaxes(k_dec, 1, 2).astype(BF16), jnp.concatenate([w16, u16], axis=-1))
    ao = _bmm(attn.astype(BF16),
              jnp.concatenate([_block_diag(w16, bd_mask), _block_diag(u16, bd_mask)], axis=-1))
    lhs_ref[...] = jnp.concatenate(
        [q * exp_g - ao[:, :, :QUAD], jnp.where(bd_mask, -kt[:, :, :QUAD], 0.0)], axis=1).astype(BF16)
    oloc_ref[...] = ao[:, :, QUAD:]
    snew_ref[...] = jnp.where(bd_mask, kt[:, :, QUAD:], 0.0)
    dec_ref[...] = jnp.broadcast_to(jnp.exp(g_last), (nb, 8, QUAD))

    for j in range(nc):
        sl = slice(j * per_step, (j + 1) * per_step)
        state = state_ref[...]
        r = _bmm(lhs_ref[sl], state.astype(BF16))
        state_ref[...] = state * dec_ref[sl][:, 0:1, :] + r[:, GDN_CHUNK:] + snew_ref[sl]
        o = r[:, :GDN_CHUNK] + oloc_ref[sl]
        for b in range(batch):
            lo = b * rows + j * GDN_CHUNK
            oraw_ref[lo:lo + GDN_CHUNK, 0:QUAD] = o[2 * b]
            oraw_ref[lo:lo + GDN_CHUNK, QUAD:] = o[2 * b + 1]

    o_all = oraw_ref[...]
    ms = jnp.dot((o_all * o_all).astype(BF16), e512_ref[...], preferred_element_type=F32) * (1.0 / GDN_DIM)
    o_ref[...] = (o_all * lax.rsqrt(ms + NORM_EPS) * nw_ref[...]).astype(o_ref.dtype).reshape(batch, rows, GDN_WIDTH)


def _gdn(gdn_qkv, logits, alog_row, dtb_row, norm_row, batch, seq):
    tg = min(GDN_TILE_ROWS, batch * seq)
    rows = tg // batch
    steps = seq // rows
    nb = 2 * (tg // GDN_CHUNK)
    head_of_lane = np.arange(GDN_WIDTH) // GDN_DIM
    e512 = jnp.asarray(head_of_lane[:, None] == head_of_lane[None, :], BF16)
    e_beta = jnp.asarray(np.arange(LOGIT_PAD)[:, None] == head_of_lane[None, :], BF16)
    e_g = jnp.asarray(np.arange(LOGIT_PAD)[:, None] == head_of_lane[None, :] + GDN_HEADS, BF16)
    tri = jnp.asarray(np.tril(np.ones((GDN_CHUNK, GDN_CHUNK))), BF16)

    cur = lambda width, c: pl.BlockSpec((batch, rows, width), lambda n: (0, n, c))
    whole = lambda a: pl.BlockSpec(a.shape, lambda n: (0,) * a.ndim)
    consts = (alog_row, dtb_row, norm_row, e512, e_beta, e_g, tri)
    qkv3 = gdn_qkv.reshape(batch, seq, GROUP_COLS)
    out = pl.pallas_call(
        _gdn_kernel,
        grid=(steps,),
        in_specs=[cur(GROUP_COLS, 0), cur(LOGIT_PAD, 0)]
        + [whole(a) for a in consts],
        out_specs=cur(GDN_WIDTH, 0),
        out_shape=jax.ShapeDtypeStruct((batch, seq, GDN_WIDTH), BF16),
        scratch_shapes=[
            pltpu.VMEM((tg, GDN_WIDTH), F32),
            pltpu.VMEM((tg, GDN_WIDTH), BF16),
            pltpu.VMEM((tg, GDN_WIDTH), BF16),
            pltpu.VMEM((nb, GDN_CHUNK + QUAD, QUAD), BF16),
            pltpu.VMEM((nb, GDN_CHUNK, QUAD), F32),
            pltpu.VMEM((nb, QUAD, QUAD), F32),
            pltpu.VMEM((nb, 8, QUAD), F32),
            pltpu.VMEM((tg, GDN_WIDTH), F32),
            pltpu.VMEM((2 * batch, QUAD, QUAD), F32),
        ],
        compiler_params=pltpu.CompilerParams(
            dimension_semantics=("arbitrary",), vmem_limit_bytes=VMEM_LIMIT_BYTES),
        name="gdn",
    )(qkv3, logits.reshape(batch, seq, LOGIT_PAD), *consts)
    return out.reshape(batch * seq, GDN_WIDTH)


def _rope_tables(seq):
    inv_freq = np.float32(ROPE_THETA) ** (-np.arange(0, DSA_HEAD_DIM, 2, dtype=np.float32) / np.float32(DSA_HEAD_DIM))
    ang = np.arange(seq, dtype=np.float32)[:, None] * inv_freq[None, :]
    cos, sin = np.cos(ang), np.sin(ang)
    return (jnp.asarray(np.concatenate([cos, cos, cos, cos], axis=-1), F32),
            jnp.asarray(np.concatenate([-sin, sin, -sin, sin], axis=-1), F32))


def kernel(x, norm_w, w_in, conv_w, a_log, dt_bias, gdn_norm_w, w_up_a, w_up_b, w_out, final_norm_w):
    batch, seq, _ = x.shape
    assert norm_w.shape[0] == 1, "the final RMSNorm is fused into the (single) layer's output kernel"
    cos_t, sin_t = _rope_tables(seq)
    x2 = x.reshape(batch * seq, D_MODEL)
    w = w_in[0].astype(BF16)
    gates = LOGIT_START + 2 * GDN_HEADS
    dsa_z, gdn_z = QKV_COLS, GDN_IN_START + GROUP_COLS
    half = lambda a: a * jnp.asarray(0.5, a.dtype)
    w_gates = jnp.concatenate(
        [half(w[:, gates:]), half(w[:, dsa_z:GDN_IN_START]), half(w[:, gdn_z:LOGIT_START]),
         w[:, LOGIT_START:gates], jnp.zeros((D_MODEL, LOGIT_PAD - 2 * GDN_HEADS), BF16)], axis=1)
    qkv, gdn_qkv, gate_cols, logits = _in_proj(
        x2, norm_w[0][None, :], w, w_gates, half(conv_w[0]), cos_t, sin_t, seq)

    o_groups, lse_groups = [], []
    for g, (_, dilation) in enumerate(DSA_PATTERNS):
        o_g, lse_g = _dsa_attention(qkv[g], dilation, batch, seq)
        o_groups.append(o_g)
        lse_groups.append(lse_g)

    pad8 = lambda p: jnp.pad(p.astype(F32), (GDN_HEADS, LOGIT_PAD - 2 * GDN_HEADS))[None, :]
    o_b = _gdn(gdn_qkv, logits, pad8(a_log[0]), pad8(dt_bias[0]),
               jnp.tile(gdn_norm_w[0].astype(F32), GDN_HEADS)[None, :], batch, seq)

    out = _out_proj(o_groups, lse_groups, gate_cols, o_b, x2,
                    w_up_a[0].astype(BF16), w_up_b[0].astype(BF16), half(w_out[0]).astype(BF16),
                    final_norm_w[None, :])
    return out.reshape(batch, seq, D_MODEL)
```

```python
import functools

import numpy as np
import jax
import jax.numpy as jnp
from jax import lax
from jax.experimental import pallas as pl
from jax.experimental.pallas import tpu as pltpu

F32 = jnp.float32
BF16 = jnp.bfloat16

D_MODEL = 1024
DSA_PATTERNS = ((128, 1), (512, 4), (2048, 16))
DSA_HEADS = 8
DSA_HEAD_DIM = 64
DSA_WIDTH = DSA_HEADS * DSA_HEAD_DIM
DSA_BLOCK = 128
ROPE_THETA = 10000.0
GDN_HEADS = 8
GDN_DIM = 64
GDN_WIDTH = GDN_HEADS * GDN_DIM
GDN_CONV = 4
GDN_CHUNK = 64
NORM_EPS = 1e-6

QKV_COLS = 3 * 3 * DSA_WIDTH
GROUP_COLS = 3 * DSA_WIDTH
GATE_COLS = 3072
GATE_A, GATE_B = 0, 1
GATE_DSA_Z, GATE_GDN_Z = 4, 5
GDN_IN_START = QKV_COLS + DSA_WIDTH
LOGIT_START = 7168
LOGIT_PAD = 128
PROJ_ROWS = 1024
OUT_ROWS = 1024
LOG2_E = 1.4426950408889634
LN_2 = 0.6931471805599453
DSA_Q_SCALE = DSA_HEAD_DIM ** -0.5 * LOG2_E

VMEM_LIMIT_BYTES = 56 * 1024 * 1024
LANES = 128
NEG_BIG = -1e30


def _lane_iota(shape):
    return lax.broadcasted_iota(jnp.int32, shape, len(shape) - 1)


def _in_proj_qkv_kernel(x_ref, nw_ref, w_ref, cos_ref, sin_ref, qkv0_ref, qkv1_ref, qkv2_ref, h_ref):
    x = x_ref[...]
    ms = jnp.mean(x * x, axis=-1, keepdims=True)
    h = (x * lax.rsqrt(ms + NORM_EPS) * nw_ref[...]).astype(BF16)
    h_ref[...] = h
    cos = cos_ref[...]
    sin = sin_ref[...]
    low_half = (_lane_iota(cos.shape) % DSA_HEAD_DIM) < (DSA_HEAD_DIM // 2)
    for g, qkv_ref in enumerate((qkv0_ref, qkv1_ref, qkv2_ref)):
        c0 = g * GROUP_COLS
        acc = jnp.dot(h, w_ref[:, c0:c0 + GROUP_COLS], preferred_element_type=F32)
        for c in range(8):
            t = acc[:, c * LANES:(c + 1) * LANES]
            rot = jnp.where(low_half, pltpu.roll(t, LANES - 32, 1), pltpu.roll(t, 32, 1))
            r = t * cos + rot * sin
            if c < 4:
                r = r * DSA_Q_SCALE
            qkv_ref[:, c * LANES:(c + 1) * LANES] = r.astype(BF16)
        qkv_ref[:, 8 * LANES:] = acc[:, 8 * LANES:].astype(BF16)


def _in_proj_rest_kernel(h_ref, wgq_ref, wgk_ref, wgv_ref, w_ref, wl_ref, cw_ref, e512_ref,
                         gdn_ref, gates_ref, logit_ref, xs_ref, *, tiles_per_seq):
    tm = h_ref.shape[0]
    first = pl.program_id(0) % tiles_per_seq == 0

    @pl.when(first)
    def _():
        xs_ref[0:8, :] = jnp.zeros((8, GROUP_COLS), F32)

    @pl.when(jnp.logical_not(first))
    def _():
        xs_ref[0:8, :] = xs_ref[tm:tm + 8, :]

    h = h_ref[...]
    logit_ref[...] = jnp.dot(h, wl_ref[...], preferred_element_type=F32)
    for part, wg_ref in enumerate((wgq_ref, wgk_ref, wgv_ref)):
        xs_ref[8:8 + tm, part * GDN_WIDTH:(part + 1) * GDN_WIDTH] = jnp.dot(
            h, wg_ref[...], preferred_element_type=F32)
    for g in range(GATE_COLS // GROUP_COLS):
        c0 = g * GROUP_COLS
        gates_ref[:, c0:c0 + GROUP_COLS] = jnp.dot(
            h, w_ref[:, c0:c0 + GROUP_COLS], preferred_element_type=F32).astype(BF16)
    for part in range(3):
        cols = slice(part * GDN_WIDTH, (part + 1) * GDN_WIDTH)
        w = cw_ref[:, cols]
        y = w[0:1, :] * xs_ref[5:5 + tm, cols]
        for j in range(1, GDN_CONV):
            y = y + w[j:j + 1, :] * xs_ref[5 + j:5 + j + tm, cols]
        y = _silu_of_twice(y)
        if part < 2:
            ss = jnp.dot((y * y).astype(BF16), e512_ref[...], preferred_element_type=F32)
            y = y * (lax.rsqrt(ss + NORM_EPS) * (GDN_DIM ** -0.5 if part == 0 else 1.0))
        gdn_ref[:, cols] = y.astype(BF16)


def _resident(shape, col_block=0):
    return pl.BlockSpec(shape, lambda i: (0,) * (len(shape) - 1) + (col_block,), pipeline_mode=pl.Buffered(1))


def _in_proj(x2, norm_w, w_bf16, w_gates, conv_w, cos_t, sin_t, seq):
    assert GDN_IN_START % GDN_WIDTH == 0 and GATE_COLS % LOGIT_PAD == 0
    rows = x2.shape[0]
    tm = min(PROJ_ROWS, seq)
    n_seq_tiles = seq // tm
    row = lambda width: pl.BlockSpec((tm, width), lambda i: (i, 0))
    table = pl.BlockSpec((tm, LANES), lambda i: (i % n_seq_tiles, 0))
    head_of_lane = np.arange(GDN_WIDTH) // GDN_DIM
    e512 = jnp.asarray(head_of_lane[:, None] == head_of_lane[None, :], BF16)
    gdn_block = GDN_IN_START // GDN_WIDTH
    *qkv, h = pl.pallas_call(
        _in_proj_qkv_kernel,
        grid=(rows // tm,),
        in_specs=[row(D_MODEL), _resident((1, D_MODEL)), _resident((D_MODEL, QKV_COLS)), table, table],
        out_specs=[row(GROUP_COLS)] * len(DSA_PATTERNS) + [row(D_MODEL)],
        out_shape=[jax.ShapeDtypeStruct((rows, GROUP_COLS), BF16)] * len(DSA_PATTERNS)
        + [jax.ShapeDtypeStruct((rows, D_MODEL), BF16)],
        compiler_params=pltpu.CompilerParams(dimension_semantics=("parallel",), vmem_limit_bytes=VMEM_LIMIT_BYTES),
        name="in_proj_qkv",
    )(x2, norm_w, w_bf16, cos_t, sin_t)
    gdn_qkv, gates, logits = pl.pallas_call(
        functools.partial(_in_proj_rest_kernel, tiles_per_seq=n_seq_tiles),
        grid=(rows // tm,),
        in_specs=[row(D_MODEL), _resident((D_MODEL, GDN_WIDTH), gdn_block),
                  _resident((D_MODEL, GDN_WIDTH), gdn_block + 1), _resident((D_MODEL, GDN_WIDTH), gdn_block + 2),
                  _resident((D_MODEL, GATE_COLS)), _resident((D_MODEL, LOGIT_PAD), GATE_COLS // LOGIT_PAD),
                  _resident(conv_w.shape), _resident(e512.shape)],
        out_specs=[row(GROUP_COLS), row(GATE_COLS), row(LOGIT_PAD)],
        out_shape=[jax.ShapeDtypeStruct((rows, GROUP_COLS), BF16), jax.ShapeDtypeStruct((rows, GATE_COLS), BF16),
                   jax.ShapeDtypeStruct((rows, LOGIT_PAD), F32)],
        scratch_shapes=[pltpu.VMEM((tm + 8, GROUP_COLS), F32)],
        compiler_params=pltpu.CompilerParams(
            dimension_semantics=("arbitrary",), vmem_limit_bytes=VMEM_LIMIT_BYTES),
        name="in_proj_rest",
    )(h, w_bf16, w_bf16, w_bf16, w_gates, w_gates, conv_w, e512)
    return qkv, gdn_qkv, gates, logits


DSA_SPAN = 2048
LSE_GROUP = 16


def _dsa_kernel(qkv_ref, o_ref, lse_ref, stage_ref, qp_ref, kp_ref, vp_ref, lstage_ref, bias_ref, tmp_ref=None,
                *, d, span):
    first = pl.program_id(1) == 0
    npr = span // d
    n_sub = npr // DSA_BLOCK
    kstride = npr + DSA_BLOCK

    def regroup(part, dst_ref, dst_stride, dst_off):
        c_in = part * DSA_WIDTH
        if d == 1:
            dst_ref[dst_off:dst_off + npr, :] = qkv_ref[:, c_in:c_in + DSA_WIDTH]
            return
        for c in range(4):
            stage_ref[c] = qkv_ref[:, c_in + c * LANES:c_in + (c + 1) * LANES].astype(F32)
        if d > 4:
            quarter = span // 4
            for c in range(4):
                for r4 in range(4):
                    tmp_ref[c, r4 * quarter:(r4 + 1) * quarter, :] = stage_ref[c, pl.ds(r4, quarter, stride=4), :]
            slabs, stride, start = tmp_ref, d // 4, lambda r: (r % 4) * quarter + r // 4
        else:
            slabs, stride, start = stage_ref, d, lambda r: r
        for c in range(4):
            for r in range(d):
                lo = r * dst_stride + dst_off
                dst_ref[lo:lo + npr, c * LANES:(c + 1) * LANES] = (
                    slabs[c, pl.ds(start(r), npr, stride=stride), :].astype(BF16))

    @pl.when(first)
    def _():
        for r in range(d):
            kp_ref[r * kstride:r * kstride + DSA_BLOCK, :] = jnp.zeros((DSA_BLOCK, DSA_WIDTH), BF16)
            vp_ref[r * kstride:r * kstride + DSA_BLOCK, :] = jnp.zeros((DSA_BLOCK, DSA_WIDTH), BF16)

    regroup(0, qp_ref, npr, 0)
    regroup(1, kp_ref, kstride, DSA_BLOCK)
    regroup(2, vp_ref, kstride, DSA_BLOCK)

    qi = lax.broadcasted_iota(jnp.int32, (DSA_BLOCK, 2 * DSA_BLOCK), 0)
    kj = lax.broadcasted_iota(jnp.int32, (DSA_BLOCK, 2 * DSA_BLOCK), 1)
    band = (kj >= qi) & (kj <= qi + DSA_BLOCK)
    bias_ref[0] = jnp.where(band, 0.0, NEG_BIG)
    bias_ref[1] = jnp.where(band & (kj >= DSA_BLOCK), 0.0, NEG_BIG)
    head_a = _lane_iota((DSA_BLOCK, LANES)) < DSA_HEAD_DIM
    lse_pair = (_lane_iota((DSA_BLOCK, LANES)) % DSA_HEAD_DIM) // LSE_GROUP

    def block(blk, carry):
        r, i = blk // n_sub, blk % n_sub
        q0 = pl.multiple_of(r * npr + i * DSA_BLOCK, DSA_BLOCK)
        k0 = pl.multiple_of(r * kstride + i * DSA_BLOCK, DSA_BLOCK)
        if d > 4:
            out_ref, lout_ref, out_stride = tmp_ref, tmp_ref.at[4], d // 4
            nat0 = (r % 4) * (span // 4) + r // 4 + out_stride * DSA_BLOCK * i
        else:
            out_ref, lout_ref, out_stride = stage_ref, lstage_ref, d
            nat0 = r + d * DSA_BLOCK * i
        no_prev = jnp.where(first & (i == 0), 1, 0)
        m_tile = den_tile = None
        for hp in range(DSA_HEADS // 2):
            cols = slice(hp * LANES, (hp + 1) * LANES)
            q2 = qp_ref[pl.ds(q0, DSA_BLOCK), cols]
            k2 = kp_ref[pl.ds(k0, 2 * DSA_BLOCK), cols]
            v2 = vp_ref[pl.ds(k0, 2 * DSA_BLOCK), cols]
            pvs, ms, dens = [], [], []
            for is_a in (True, False):
                sel = head_a if is_a else jnp.logical_not(head_a)
                qh = jnp.where(sel, q2, jnp.zeros_like(q2))
                s = lax.dot_general(qh, k2, (((1,), (1,)), ((), ())), preferred_element_type=F32)
                s = s + bias_ref[no_prev]
                m = jnp.max(s, axis=-1, keepdims=True)
                p = jnp.exp2(s - m)
                dens.append(jnp.sum(p, axis=-1, keepdims=True))
                pvs.append(jnp.dot(p.astype(BF16), v2, preferred_element_type=F32))
                ms.append(m)
            den_pair = jnp.where(head_a, dens[0], dens[1])
            m_pair = jnp.where(head_a, ms[0], ms[1])
            o_pair = jnp.where(head_a, pvs[0], pvs[1]) / den_pair
            den_tile = den_pair if hp == 0 else jnp.where(lse_pair == hp, den_pair, den_tile)
            m_tile = m_pair if hp == 0 else jnp.where(lse_pair == hp, m_pair, m_tile)
            if d == 1:
                o_ref[pl.ds(q0, DSA_BLOCK), cols] = o_pair.astype(o_ref.dtype)
            else:
                out_ref[hp, pl.ds(nat0, DSA_BLOCK, stride=out_stride), :] = o_pair
        lse_tile = m_tile * LN_2 + jnp.log(den_tile)
        if d == 1:
            lse_ref[pl.ds(q0, DSA_BLOCK), :] = lse_tile
        else:
            lout_ref[pl.ds(nat0, DSA_BLOCK, stride=out_stride), :] = lse_tile
        return carry

    lax.fori_loop(0, d * n_sub, block, 0, unroll=16)

    for r in range(d):
        lo = r * kstride
        kp_ref[lo:lo + DSA_BLOCK, :] = kp_ref[lo + npr:lo + npr + DSA_BLOCK, :]
        vp_ref[lo:lo + DSA_BLOCK, :] = vp_ref[lo + npr:lo + npr + DSA_BLOCK, :]
    if d > 4:
        quarter = span // 4
        for r4 in range(4):
            rows = slice(r4 * quarter, (r4 + 1) * quarter)
            for c in range(4):
                stage_ref[c, pl.ds(r4, quarter, stride=4), :] = tmp_ref[c, rows, :]
            lstage_ref[pl.ds(r4, quarter, stride=4), :] = tmp_ref[4, rows, :]
    if d > 1:
        for c in range(4):
            o_ref[:, c * LANES:(c + 1) * LANES] = stage_ref[c].astype(o_ref.dtype)
        lse_ref[...] = lstage_ref[...]


def _dsa_attention(qkv, dilation, batch, seq):
    span = min(DSA_SPAN, seq)
    steps = seq // span
    kv_rows = span + dilation * DSA_BLOCK
    row = lambda width: pl.BlockSpec((span, width), lambda b, n: (b * steps + n, 0))
    return pl.pallas_call(
        functools.partial(_dsa_kernel, d=dilation, span=span),
        grid=(batch, steps),
        in_specs=[row(GROUP_COLS)],
        out_specs=[row(DSA_WIDTH), row(LANES)],
        out_shape=[
            jax.ShapeDtypeStruct((batch * seq, DSA_WIDTH), BF16),
            jax.ShapeDtypeStruct((batch * seq, LANES), F32),
        ],
        scratch_shapes=[
            pltpu.VMEM((4, span, LANES), F32),
            pltpu.VMEM((span, DSA_WIDTH), BF16),
            pltpu.VMEM((kv_rows, DSA_WIDTH), BF16),
            pltpu.VMEM((kv_rows, DSA_WIDTH), BF16),
            pltpu.VMEM((span, LANES), F32),
            pltpu.VMEM((2, DSA_BLOCK, 2 * DSA_BLOCK), F32),
        ] + ([pltpu.VMEM((5, span, LANES), F32)] if dilation > 4 else []),
        compiler_params=pltpu.CompilerParams(
            dimension_semantics=("parallel", "arbitrary"), vmem_limit_bytes=VMEM_LIMIT_BYTES),
        name=f"dsa_attn_d{dilation}",
    )(qkv)


def _silu_of_twice(h):
    return h + h * jnp.tanh(h)


def _sigmoid(x):
    return 1.0 / (1.0 + jnp.exp(-x))


def _out_proj_kernel(o0_ref, o1_ref, o2_ref, l0_ref, l1_ref, l2_ref, g_ref, ob_ref,
                     x_ref, wa_ref, wb_ref, wo_ref, fw_ref, ex_ref, out_ref):
    def gate(block, width):
        return g_ref[:, block * width:(block + 1) * width].astype(F32)

    l0, l1, l2 = l0_ref[...], l1_ref[...], l2_ref[...]
    mx = jnp.maximum(jnp.maximum(l0, l1), l2)
    e0, e1, e2 = jnp.exp(l0 - mx), jnp.exp(l1 - mx), jnp.exp(l2 - mx)
    inv = 1.0 / (e0 + e1 + e2)

    def per_lane(w):
        return jnp.dot(w.astype(BF16), ex_ref[...], preferred_element_type=F32)

    o_a = (per_lane(e0 * inv) * o0_ref[...].astype(F32) + per_lane(e1 * inv) * o1_ref[...].astype(F32)
           + per_lane(e2 * inv) * o2_ref[...].astype(F32))
    a_in = (o_a * _silu_of_twice(gate(GATE_DSA_Z, DSA_WIDTH))).astype(BF16)
    b_in = (ob_ref[...].astype(F32) * _silu_of_twice(gate(GATE_GDN_Z, GDN_WIDTH))).astype(BF16)
    y_a = jnp.dot(a_in, wa_ref[...], preferred_element_type=F32)
    y_b = jnp.dot(b_in, wb_ref[...], preferred_element_type=F32)
    merged2 = (1.0 + jnp.tanh(gate(GATE_A, D_MODEL))) * y_a + (1.0 + jnp.tanh(gate(GATE_B, D_MODEL))) * y_b
    y = x_ref[...] + jnp.dot(merged2.astype(BF16), wo_ref[...], preferred_element_type=F32)
    ms = jnp.mean(y * y, axis=-1, keepdims=True)
    out_ref[...] = y * lax.rsqrt(ms + NORM_EPS) * fw_ref[...]


def _out_proj(o_groups, lse_groups, gates, o_b, x2, w_up_a, w_up_b, w_out, final_w):
    rows = x2.shape[0]
    tm = min(OUT_ROWS, rows)
    row128 = pl.BlockSpec((tm, LANES), lambda i: (i, 0))
    row512 = lambda c: pl.BlockSpec((tm, 512), lambda i: (i, c))
    row1024 = lambda c: pl.BlockSpec((tm, 1024), lambda i: (i, c))
    whole = lambda a: pl.BlockSpec(a.shape, lambda i: (0,) * a.ndim)
    head = np.arange(DSA_WIDTH) // DSA_HEAD_DIM
    expand = jnp.asarray(np.arange(LANES)[:, None] == (head % 2) * DSA_HEAD_DIM + (head // 2) * LSE_GROUP, BF16)
    return pl.pallas_call(
        _out_proj_kernel,
        grid=(rows // tm,),
        in_specs=[row512(0)] * 3 + [row128] * 3 + [
            pl.BlockSpec((tm, GATE_COLS), lambda i: (i, 0)), row512(0), row1024(0),
            whole(w_up_a), whole(w_up_b), whole(w_out), whole(final_w), whole(expand)],
        out_specs=pl.BlockSpec((tm, D_MODEL), lambda i: (i, 0)),
        out_shape=jax.ShapeDtypeStruct((rows, D_MODEL), F32),
        compiler_params=pltpu.CompilerParams(
            dimension_semantics=("parallel",), vmem_limit_bytes=VMEM_LIMIT_BYTES),
        name="out_proj",
    )(*o_groups, *lse_groups, gates, o_b, x2, w_up_a, w_up_b, w_out, final_w, expand)


QUAD = 4 * GDN_DIM
GDN_TILE_ROWS = 1024
LOCAL_CHUNKS = 4


def _split_hi_lo(x):
    hi = x.astype(BF16)
    lo = (x - hi.astype(F32)).astype(BF16)
    return hi, lo


def _softplus(x):
    return jnp.maximum(x, 0.0) + jnp.log1p(jnp.exp(-jnp.abs(x)))


def _bmm(a, b):
    return lax.dot_general(a, b, (((2,), (1,)), ((0,), (0,))), preferred_element_type=F32)


def _bmm_nt(a, b):
    return lax.dot_general(a, b, (((2,), (2,)), ((0,), (0,))), preferred_element_type=F32)


def _block_diag(x, bd_mask):
    t = jnp.concatenate([x, x, x, x], axis=1)
    return jnp.where(bd_mask, t, jnp.zeros_like(t))


def _gdn_kernel(qkv_ref, lg_ref, alog_ref, dtb_ref, nw_ref, e512_ref, eb_ref, eg_ref, tri_ref, o_ref,
                beta_ref, gh_ref, gl_ref, lhs_ref, oloc_ref, snew_ref, dec_ref, oraw_ref, state_ref):
    first = pl.program_id(0) == 0
    batch, rows, _ = qkv_ref.shape
    tg = batch * rows
    nc = rows // GDN_CHUNK
    per_step = 2 * batch
    nb = per_step * nc

    @pl.when(first)
    def _():
        state_ref[...] = jnp.zeros_like(state_ref)

    logits = lg_ref[...].reshape(tg, LOGIT_PAD)
    beta_hi, beta_lo = _split_hi_lo(_sigmoid(logits))
    g_hi, g_lo = _split_hi_lo(-jnp.exp(alog_ref[...]) * _softplus(logits + dtb_ref[...]))
    beta_ref[...] = (jnp.dot(beta_hi, eb_ref[...], preferred_element_type=F32)
                     + jnp.dot(beta_lo, eb_ref[...], preferred_element_type=F32))
    gh_ref[...] = jnp.dot(g_hi, eg_ref[...], preferred_element_type=F32).astype(BF16)
    gl_ref[...] = jnp.dot(g_lo, eg_ref[...], preferred_element_type=F32).astype(BF16)

    row = lax.broadcasted_iota(jnp.int32, (GDN_CHUNK, QUAD), 0)
    col = lax.broadcasted_iota(jnp.int32, (GDN_CHUNK, QUAD), 1) % GDN_CHUNK
    incl = row >= col
    strict = row > col
    eye = (row == col).astype(F32)
    bd_mask = (lax.broadcasted_iota(jnp.int32, (QUAD, QUAD), 0) // GDN_DIM
               == lax.broadcasted_iota(jnp.int32, (QUAD, QUAD), 1) // GDN_DIM)
    tri = tri_ref[...]

    def chunk_quads(read, c0, n, dtype):
        a = [read(b, c0 * GDN_CHUNK, n * GDN_CHUNK).reshape(n, GDN_CHUNK, GDN_WIDTH).astype(dtype) for b in range(batch)]
        a = jnp.stack([a[b][j] for j in range(n) for b in range(batch)])
        return jnp.stack([a[:, :, :QUAD], a[:, :, QUAD:]], axis=1).reshape(n * per_step, GDN_CHUNK, QUAD)

    def scratch_rows(ref):
        return lambda b, lo, size: ref[b * rows + lo:b * rows + lo + size, :]

    def local_phase(c0, n):
        nq = n * per_step
        q, k, v = (chunk_quads(lambda b, lo, size: qkv_ref[b, lo:lo + size, part * GDN_WIDTH:(part + 1) * GDN_WIDTH],
                               c0, n, F32) for part in range(3))
        beta = chunk_quads(scratch_rows(beta_ref), c0, n, F32)
        gh, gl = chunk_quads(scratch_rows(gh_ref), c0, n, BF16), chunk_quads(scratch_rows(gl_ref), c0, n, BF16)

        zero = jnp.zeros_like(gh)
        rhs_hi = jnp.concatenate([gh, jnp.where(strict, gh, zero)], axis=-1)
        rhs_lo = jnp.concatenate([gl, jnp.where(strict, gl, zero)], axis=-1)
        tri_b = jnp.broadcast_to(tri, (nq, GDN_CHUNK, GDN_CHUNK))
        gd = _bmm(tri_b, rhs_hi) + _bmm(tri_b, rhs_lo)
        yield
        g_cum, d_pair = gd[:, :, :QUAD], gd[:, :, QUAD:]
        decay_incl = jnp.where(incl, jnp.exp(d_pair), 0.0)
        decay_strict = jnp.where(strict, decay_incl, 0.0)
        exp_g = jnp.exp(g_cum)
        g_last = g_cum[:, GDN_CHUNK - 1:GDN_CHUNK, :]
        k_dec = k * jnp.exp(g_last - g_cum)
        k_beta = k * beta

        k_bd = _block_diag(k.astype(BF16), bd_mask)
        aa = _bmm_nt(jnp.concatenate([k_beta, q], axis=1).astype(BF16), k_bd)
        a = aa[:, :GDN_CHUNK] * decay_strict
        attn = aa[:, GDN_CHUNK:] * decay_incl
        yield

        p = a
        t = eye - a
        p = _bmm(p.astype(BF16), _block_diag(p.astype(BF16), bd_mask))
        for _ in range(4):
            r = _bmm(jnp.concatenate([p, t], axis=1).astype(BF16), _block_diag(p.astype(BF16), bd_mask))
            p = r[:, :GDN_CHUNK]
            t = t + r[:, GDN_CHUNK:]
        t = t + _bmm(t.astype(BF16), _block_diag(p.astype(BF16), bd_mask))
        t16 = t.astype(BF16)
        yield
        u = _bmm(t16, _block_diag((v * beta).astype(BF16), bd_mask))
        w = _bmm(t16, _block_diag((k_beta * exp_g).astype(BF16), bd_mask))

        w16, u16 = w.astype(BF16), u.astype(BF16)
        kt = _bmm(jnp.swapaxes(k_dec, 1, 2).astype(BF16), jnp.concatenate([w16, u16], axis=-1))
        ao = _bmm(attn.astype(BF16),
                  jnp.concatenate([_block_diag(w16, bd_mask), _block_diag(u16, bd_mask)], axis=-1))
        sl = slice(c0 * per_step, (c0 + n) * per_step)
        lhs_ref[sl] = jnp.concatenate(
            [q * exp_g - ao[:, :, :QUAD], jnp.where(bd_mask, -kt[:, :, :QUAD], 0.0)], axis=1).astype(BF16)
        oloc_ref[sl] = ao[:, :, QUAD:]
        snew_ref[sl] = jnp.where(bd_mask, kt[:, :, QUAD:], 0.0)
        dec_ref[sl] = jnp.broadcast_to(jnp.exp(g_last), (nq, 8, QUAD))
        yield

    def scan_step(j):
        sl = slice(j * per_step, (j + 1) * per_step)
        state = state_ref[...]
        r = _bmm(lhs_ref[sl], state.astype(BF16))
        state_ref[...] = state * dec_ref[sl][:, 0:1, :] + r[:, GDN_CHUNK:] + snew_ref[sl]
        o = r[:, :GDN_CHUNK] + oloc_ref[sl]
        for b in range(batch):
            lo = b * rows + j * GDN_CHUNK
            oraw_ref[lo:lo + GDN_CHUNK, 0:QUAD] = o[2 * b]
            oraw_ref[lo:lo + GDN_CHUNK, QUAD:] = o[2 * b + 1]

    assert nc == 2 * LOCAL_CHUNKS
    ga, gb = local_phase(0, LOCAL_CHUNKS), local_phase(LOCAL_CHUNKS, LOCAL_CHUNKS)
    for g in (ga, ga, gb, ga, gb, ga, gb):
        next(g)
    for j in range(LOCAL_CHUNKS):
        scan_step(j)
    next(gb)
    for j in range(LOCAL_CHUNKS, nc):
        scan_step(j)

    o_all = oraw_ref[...]
    ms = jnp.dot((o_all * o_all).astype(BF16), e512_ref[...], preferred_element_type=F32) * (1.0 / GDN_DIM)
    o_ref[...] = (o_all * lax.rsqrt(ms + NORM_EPS) * nw_ref[...]).astype(o_ref.dtype).reshape(batch, rows, GDN_WIDTH)


def _gdn(gdn_qkv, logits, alog_row, dtb_row, norm_row, batch, seq):
    tg = min(GDN_TILE_ROWS, batch * seq)
    rows = tg // batch
    steps = seq // rows
    nb = 2 * (tg // GDN_CHUNK)
    head_of_lane = np.arange(GDN_WIDTH) // GDN_DIM
    e512 = jnp.asarray(head_of_lane[:, None] == head_of_lane[None, :], BF16)
    e_beta = jnp.asarray(np.arange(LOGIT_PAD)[:, None] == head_of_lane[None, :], BF16)
    e_g = jnp.asarray(np.arange(LOGIT_PAD)[:, None] == head_of_lane[None, :] + GDN_HEADS, BF16)
    tri = jnp.asarray(np.tril(np.ones((GDN_CHUNK, GDN_CHUNK))), BF16)

    cur = lambda width, c: pl.BlockSpec((batch, rows, width), lambda n: (0, n, c))
    whole = lambda a: pl.BlockSpec(a.shape, lambda n: (0,) * a.ndim)
    consts = (alog_row, dtb_row, norm_row, e512, e_beta, e_g, tri)
    qkv3 = gdn_qkv.reshape(batch, seq, GROUP_COLS)
    out = pl.pallas_call(
        _gdn_kernel,
        grid=(steps,),
        in_specs=[cur(GROUP_COLS, 0), cur(LOGIT_PAD, 0)]
        + [whole(a) for a in consts],
        out_specs=cur(GDN_WIDTH, 0),
        out_shape=jax.ShapeDtypeStruct((batch, seq, GDN_WIDTH), BF16),
        scratch_shapes=[
            pltpu.VMEM((tg, GDN_WIDTH), F32),
            pltpu.VMEM((tg, GDN_WIDTH), BF16),
            pltpu.VMEM((tg, GDN_WIDTH), BF16),
            pltpu.VMEM((nb, GDN_CHUNK + QUAD, QUAD), BF16),
            pltpu.VMEM((nb, GDN_CHUNK, QUAD), F32),
            pltpu.VMEM((nb, QUAD, QUAD), F32),
            pltpu.VMEM((nb, 8, QUAD), F32),
            pltpu.VMEM((tg, GDN_WIDTH), F32),
            pltpu.VMEM((2 * batch, QUAD, QUAD), F32),
        ],
        compiler_params=pltpu.CompilerParams(
            dimension_semantics=("arbitrary",), vmem_limit_bytes=VMEM_LIMIT_BYTES),
        name="gdn",
    )(qkv3, logits.reshape(batch, seq, LOGIT_PAD), *consts)
    return out.reshape(batch * seq, GDN_WIDTH)


def _rope_tables(seq):
    inv_freq = np.float32(ROPE_THETA) ** (-np.arange(0, DSA_HEAD_DIM, 2, dtype=np.float32) / np.float32(DSA_HEAD_DIM))
    ang = np.arange(seq, dtype=np.float32)[:, None] * inv_freq[None, :]
    cos, sin = np.cos(ang), np.sin(ang)
    return (jnp.asarray(np.concatenate([cos, cos, cos, cos], axis=-1), F32),
            jnp.asarray(np.concatenate([-sin, sin, -sin, sin], axis=-1), F32))


def kernel(x, norm_w, w_in, conv_w, a_log, dt_bias, gdn_norm_w, w_up_a, w_up_b, w_out, final_norm_w):
    batch, seq, _ = x.shape
    assert norm_w.shape[0] == 1, "the final RMSNorm is fused into the (single) layer's output kernel"
    cos_t, sin_t = _rope_tables(seq)
    x2 = x.reshape(batch * seq, D_MODEL)
    w = w_in[0].astype(BF16)
    gates = LOGIT_START + 2 * GDN_HEADS
    dsa_z, gdn_z = QKV_COLS, GDN_IN_START + GROUP_COLS
    half = lambda a: a * jnp.asarray(0.5, a.dtype)
    w_gates = jnp.concatenate(
        [half(w[:, gates:]), half(w[:, dsa_z:GDN_IN_START]), half(w[:, gdn_z:LOGIT_START]),
         w[:, LOGIT_START:gates], jnp.zeros((D_MODEL, LOGIT_PAD - 2 * GDN_HEADS), BF16)], axis=1)
    qkv, gdn_qkv, gate_cols, logits = _in_proj(
        x2, norm_w[0][None, :], w, w_gates, half(conv_w[0]), cos_t, sin_t, seq)

    o_groups, lse_groups = [], []
    for g, (_, dilation) in enumerate(DSA_PATTERNS):
        o_g, lse_g = _dsa_attention(qkv[g], dilation, batch, seq)
        o_groups.append(o_g)
        lse_groups.append(lse_g)

    pad8 = lambda p: jnp.pad(p.astype(F32), (GDN_HEADS, LOGIT_PAD - 2 * GDN_HEADS))[None, :]
    o_b = _gdn(gdn_qkv, logits, pad8(a_log[0]), pad8(dt_bias[0]),
               jnp.tile(gdn_norm_w[0].astype(F32), GDN_HEADS)[None, :], batch, seq)

    out = _out_proj(o_groups, lse_groups, gate_cols, o_b, x2,
                    w_up_a[0].astype(BF16), w_up_b[0].astype(BF16), half(w_out[0]).astype(BF16),
                    final_norm_w[None, :])
    return out.reshape(batch, seq, D_MODEL)
```
